```python
import numpy as np
import jax
import jax.numpy as jnp
from jax import lax

D_MODEL = 1024
BATCH = 4
SEQ = 8192
DEPTH = 4

N_MIXERS = 3
NORM_EPS = 1e-6
NEG_INF = -1e30

RWKV_HEAD_DIM = 64
RWKV_HEADS = D_MODEL // RWKV_HEAD_DIM
DECAY_LORA = 64
ICLR_LORA = 64
VALUE_LORA = 32
GATE_LORA = 128
RWKV_GN_EPS = 64e-5
N_RWKV_LAYERS = len(range(0, DEPTH, N_MIXERS))
N_VRES = N_RWKV_LAYERS - 1

NSA_HEADS = 16
NSA_KV_GROUPS = 2
NSA_HEAD_DIM = D_MODEL // NSA_HEADS
CMP_BLOCK = 32
CMP_STRIDE = 16
CMP_HIDDEN = 256
SLC_BLOCK = 64
SLC_TOPK = 16
WINDOW = 512
Q_BLOCK = 128
FORCE_BONUS = 1e4
NSA_IN_DIM = NSA_HEADS * NSA_HEAD_DIM + 6 * NSA_KV_GROUPS * NSA_HEAD_DIM + 3 * NSA_HEADS
N_NSA_LAYERS = len(range(1, DEPTH, N_MIXERS))

GLA_HEADS = 4
GLA_KEY_DIM = D_MODEL // 2
GLA_VAL_DIM = D_MODEL
GLA_GATE_LORA = 16
GLA_TAU = 16.0
GLA_CHUNK = 64
GLA_NORM_EPS = 1e-5
GLA_IN_DIM = 2 * GLA_KEY_DIM + 2 * GLA_VAL_DIM + GLA_GATE_LORA
N_GLA_LAYERS = len(range(2, DEPTH, N_MIXERS))

D_FF = 2816
FFN_CONV = 3

kernel_name = 'hybrid_rwkv7_nsa_gla_convffn'


def rms_norm(x, g, eps=NORM_EPS):
    x32 = x.astype(jnp.float32)
    y = x32 * lax.rsqrt(jnp.mean(x32 * x32, axis=-1, keepdims=True) + eps)
    return (y * g.astype(jnp.float32)).astype(x.dtype)


def shift_right(x, n):
    return jnp.pad(x, ((0, 0), (n, 0), (0, 0)))[:, :x.shape[1]]


def rwkv7_step(state, inp):
    r, w, k, v, a, b = inp
    sa = jnp.einsum('bhij,bhj->bhi', state, a)
    state = state * w[:, :, None, :] + sa[..., None] * b[:, :, None, :] + v[..., None] * k[:, :, None, :]
    y = jnp.einsum('bhij,bhj->bhi', state, r)
    return state, y


def rwkv7_time_mix(x, mu, w_rkv, w0, w1, w2, a0, a1, a2, g1, g2, k_k, k_a, r_k, ln_w, ln_b, w_o, v_first, vres):
    B, S, D = x.shape
    H, N = RWKV_HEADS, RWKV_HEAD_DIM
    f32 = jnp.float32
    xx = shift_right(x, 1) - x
    xr, xw, xk, xv, xa, xg = (x + xx * mu[c] for c in range(6))
    r = xr @ w_rkv[0]
    k = xk @ w_rkv[1]
    v = xv @ w_rkv[2]
    w = -jax.nn.softplus(-(w0 + jnp.tanh(xw @ w1) @ w2)) - 0.5
    decay = jnp.exp(-jnp.exp(w.astype(f32)))
    a = jax.nn.sigmoid(a0 + (xa @ a1) @ a2)
    g = jax.nn.sigmoid(xg @ g1) @ g2
    if vres is None:
        v_first = v
    else:
        v0, v1, v2 = vres
        v = v + (v_first - v) * jax.nn.sigmoid(v0 + (xv @ v1) @ v2)
    kk = (k * k_k).reshape(B, S, H, N).astype(f32)
    kk = kk / jnp.maximum(jnp.linalg.norm(kk, axis=-1, keepdims=True), 1e-12)
    k = k * (1 + (a - 1) * k_a)
    heads = lambda t: t.reshape(B, S, H, N).astype(f32)
    r_h, k_h, v_h, a_h = heads(r), heads(k), heads(v), heads(a)
    seq_major = lambda t: jnp.swapaxes(t, 0, 1)
    xs = (seq_major(r_h), seq_major(heads(decay)), seq_major(k_h), seq_major(v_h),
          seq_major(-kk), seq_major(kk * a_h))
    _, y = lax.scan(rwkv7_step, jnp.zeros((B, H, N, N), f32), xs)
    y = seq_major(y)
    mean = jnp.mean(y, axis=-1, keepdims=True)
    var = jnp.var(y, axis=-1, keepdims=True)
    y = ((y - mean) * lax.rsqrt(var + RWKV_GN_EPS)).reshape(B, S, D) * ln_w + ln_b
    bonus = jnp.sum(r_h * k_h * r_k, axis=-1, keepdims=True) * v_h
    out = (y + bonus.reshape(B, S, D)).astype(x.dtype) * g
    return out @ w_o, v_first


def cmp_to_slc_overlap(n_cmp, n_slc):
    start = np.arange(n_cmp) * CMP_STRIDE
    end = start + CMP_BLOCK - 1
    s_start = np.arange(n_slc) * SLC_BLOCK
    s_end = s_start + SLC_BLOCK - 1
    return ((end[:, None] >= s_start[None, :]) & (start[:, None] <= s_end[None, :])).astype(np.float32)


def nsa_mix(x, w_in, cmp_w1, cmp_b1, cmp_w2, cmp_pe, w_o):
    B, S, _ = x.shape
    H, G, dh = NSA_HEADS, NSA_KV_GROUPS, NSA_HEAD_DIM
    hpg = H // G
    f32 = jnp.float32
    q, kv, gates = jnp.split(x @ w_in, [H * dh, H * dh + 6 * G * dh], axis=-1)
    q = q.reshape(B, S, G, hpg, dh) * (dh ** -0.5)
    kv = kv.reshape(B, S, 6, G, dh)
    k_cmp, v_cmp, k_slc, v_slc, k_win, v_win = (kv[:, :, c] for c in range(6))
    gates = jax.nn.sigmoid(gates.reshape(B, S, G, hpg, 3))

    n_cmp = (S - CMP_BLOCK) // CMP_STRIDE + 1
    tok = np.arange(n_cmp)[:, None] * CMP_STRIDE + np.arange(CMP_BLOCK)[None, :]

    def compress(t, c):
        blk = t[:, tok] + cmp_pe[c][:, None, :]
        blk = jnp.swapaxes(blk, 2, 3).reshape(B, n_cmp, G, CMP_BLOCK * dh)
        return jax.nn.gelu(blk @ cmp_w1[c] + cmp_b1[c]) @ cmp_w2[c]

    kc, vc = compress(k_cmp, 0), compress(v_cmp, 1)
    cmp_end = jnp.asarray(tok[:, -1], jnp.int32)

    n_slc = S // SLC_BLOCK
    top_n = min(SLC_TOPK, n_slc)
    overlap = jnp.asarray(cmp_to_slc_overlap(n_cmp, n_slc))
    to_blocks = lambda t: jnp.transpose(t.reshape(B, n_slc, SLC_BLOCK, G, dh), (0, 3, 1, 2, 4))
    ks_b, vs_b = to_blocks(k_slc), to_blocks(v_slc)
    gather = jax.vmap(jax.vmap(lambda blocks, ix: blocks[ix]))
    blk_id = jnp.arange(n_slc, dtype=jnp.int32)
    in_blk = jnp.arange(SLC_BLOCK, dtype=jnp.int32)

    pad = ((0, 0), (WINDOW, 0), (0, 0), (0, 0))
    kw, vw = jnp.pad(k_win, pad), jnp.pad(v_win, pad)
    win_off = jnp.arange(WINDOW + Q_BLOCK, dtype=jnp.int32) - WINDOW

    def query_block(qb):
        t0 = qb * Q_BLOCK
        pos_q = t0 + jnp.arange(Q_BLOCK, dtype=jnp.int32)
        qc = lax.dynamic_slice_in_dim(q, t0, Q_BLOCK, axis=1)
        gc = lax.dynamic_slice_in_dim(gates, t0, Q_BLOCK, axis=1)
        vis = cmp_end[None, :] <= pos_q[:, None]
        s = jnp.einsum('bqghd,bngd->bghqn', qc, kc).astype(f32)
        p_cmp = jax.nn.softmax(jnp.where(vis, s, NEG_INF), axis=-1) * jnp.any(vis, axis=-1)[:, None]
        o_cmp = jnp.einsum('bghqn,bngd->bqghd', p_cmp.astype(vc.dtype), vc)
        imp = jnp.einsum('bghqn,nj->bgqj', p_cmp, overlap)
        cur = pos_q // SLC_BLOCK
        causal = blk_id[None, :] * SLC_BLOCK <= pos_q[:, None]
        forced = (blk_id[None, :] == 0) | (blk_id[None, :] == cur[:, None]) | (blk_id[None, :] == cur[:, None] - 1)
        score = jnp.where(causal, imp + FORCE_BONUS * forced, NEG_INF)
        _, idx = lax.top_k(score, top_n)
        kg, vg = gather(ks_b, idx), gather(vs_b, idx)
        tok_pos = idx[..., None] * SLC_BLOCK + in_blk
        s = jnp.einsum('bqghd,bgqnld->bghqnl', qc, kg).astype(f32)
        s = jnp.where(tok_pos[:, :, None] <= pos_q[:, None, None], s, NEG_INF)
        p = jax.nn.softmax(s.reshape(*s.shape[:4], -1), axis=-1)
        o_slc = jnp.einsum('bghqm,bgqmd->bqghd', p.astype(vg.dtype),
                           vg.reshape(B, G, Q_BLOCK, top_n * SLC_BLOCK, dh))
        kwc = lax.dynamic_slice_in_dim(kw, t0, WINDOW + Q_BLOCK, axis=1)
        vwc = lax.dynamic_slice_in_dim(vw, t0, WINDOW + Q_BLOCK, axis=1)
        pos_k = t0 + win_off
        wmask = ((pos_k[None, :] <= pos_q[:, None]) & (pos_k[None, :] > pos_q[:, None] - WINDOW)
                 & (pos_k[None, :] >= 0))
        s = jnp.einsum('bqghd,bkgd->bghqk', qc, kwc).astype(f32)
        p = jax.nn.softmax(jnp.where(wmask, s, NEG_INF), axis=-1)
        o_win = jnp.einsum('bghqk,bkgd->bqghd', p.astype(vwc.dtype), vwc)
        o = gc[..., 0:1] * o_cmp + gc[..., 1:2] * o_slc + gc[..., 2:3] * o_win
        return o.reshape(B, Q_BLOCK, H * dh)

    out = lax.map(query_block, jnp.arange(S // Q_BLOCK, dtype=jnp.int32))
    out = jnp.swapaxes(out, 0, 1).reshape(B, S, H * dh)
    return out @ w_o


def gla_step(state, inp):
    qe_c, kd_c, v_c, dec_c = inp
    o_c = jnp.einsum('bihd,bhdv->bihv', qe_c, state)
    state = dec_c[..., None] * state + jnp.einsum('bjhd,bjhv->bhdv', kd_c, v_c)
    return state, o_c


def gla_mix(x, w_in, a_up, a_bias, norm_w, w_o):
    B, S, D = x.shape
    H = GLA_HEADS
    dk, dv = GLA_KEY_DIM // H, GLA_VAL_DIM // H
    C = GLA_CHUNK
    nc = S // C
    f32 = jnp.float32
    q, k, v, r, a_dn = jnp.split(x @ w_in, [GLA_KEY_DIM, 2 * GLA_KEY_DIM, 2 * GLA_KEY_DIM + GLA_VAL_DIM,
                                            2 * GLA_KEY_DIM + 2 * GLA_VAL_DIM], axis=-1)
    q = (q.reshape(B, nc, C, H, dk) * (dk ** -0.5)).astype(f32)
    k = k.reshape(B, nc, C, H, dk).astype(f32)
    v = v.reshape(B, nc, C, H, dv).astype(f32)
    log_a = jax.nn.log_sigmoid((a_dn @ a_up + a_bias).astype(f32)) / GLA_TAU
    b = jnp.cumsum(log_a.reshape(B, nc, C, H, dk), axis=2)
    b_last = b[:, :, -1:]
    qe = q * jnp.exp(b)
    ke = k * jnp.exp(-b)
    kd = k * jnp.exp(b_last - b)
    A = jnp.einsum('bnihd,bnjhd->bnhij', qe, ke)
    A = jnp.where(jnp.tril(jnp.ones((C, C), bool)), A, 0.0)
    o_intra = jnp.einsum('bnhij,bnjhv->bnihv', A, v)
    decay = jnp.exp(b_last[:, :, 0])
    xs = tuple(jnp.moveaxis(t, 1, 0) for t in (qe, kd, v, decay))
    _, o_inter = lax.scan(gla_step, jnp.zeros((B, H, dk, dv), f32), xs)
    o = (o_intra + jnp.moveaxis(o_inter, 0, 1)).reshape(B, S, H, dv)
    o = o * lax.rsqrt(jnp.mean(o * o, axis=-1, keepdims=True) + GLA_NORM_EPS) * norm_w.astype(f32)
    o = o * jax.nn.silu(r.reshape(B, S, H, dv).astype(f32))
    return o.reshape(B, S, GLA_VAL_DIM).astype(x.dtype) @ w_o


def conv_ffn(x, w_in, conv_w, conv_b, w_out):
    gate, up = jnp.split(x @ w_in, 2, axis=-1)
    gate = sum(conv_w[FFN_CONV - 1 - s] * shift_right(gate, s) for s in range(FFN_CONV)) + conv_b
    return (jax.nn.silu(gate) * up) @ w_out


def setup_inputs(seed: int = 0) -> dict:
    key = jax.random.key(seed)
    ks = iter(jax.random.split(key, 40))

    def nrm(shape, scale):
        return scale * jax.random.normal(next(ks), shape, jnp.float32)

    D, F = D_MODEL, D_FF
    R, V, NN, NG = N_RWKV_LAYERS, N_VRES, N_NSA_LAYERS, N_GLA_LAYERS
    H, N = RWKV_HEADS, RWKV_HEAD_DIM
    dh = NSA_HEAD_DIM
    gla_dv = GLA_VAL_DIM // GLA_HEADS
    return {
        'x': nrm((BATCH, SEQ, D), 1.0),
        'norm_g': 1.0 + nrm((DEPTH, 4, D), 0.02),
        'ffn_w_in': nrm((DEPTH, D, 2 * F), D ** -0.5),
        'ffn_conv_w': nrm((DEPTH, FFN_CONV, F), FFN_CONV ** -0.5),
        'ffn_conv_b': nrm((DEPTH, F), 0.02),
        'ffn_w_out': nrm((DEPTH, F, D), F ** -0.5),
        'rwkv_mu': jax.random.uniform(next(ks), (R, 6, D), jnp.float32),
        'rwkv_w_rkv': nrm((R, 3, D, D), D ** -0.5),
        'rwkv_w0': nrm((R, D), 0.5),
        'rwkv_w1': nrm((R, D, DECAY_LORA), D ** -0.5),
        'rwkv_w2': nrm((R, DECAY_LORA, D), 0.1 * DECAY_LORA ** -0.5),
        'rwkv_a0': nrm((R, D), 0.5),
        'rwkv_a1': nrm((R, D, ICLR_LORA), D ** -0.5),
        'rwkv_a2': nrm((R, ICLR_LORA, D), 0.1 * ICLR_LORA ** -0.5),
        'rwkv_g1': nrm((R, D, GATE_LORA), D ** -0.5),
        'rwkv_g2': nrm((R, GATE_LORA, D), GATE_LORA ** -0.5),
        'rwkv_k_k': 0.85 + nrm((R, D), 0.05),
        'rwkv_k_a': 1.0 + nrm((R, D), 0.05),
        'rwkv_r_k': nrm((R, H, N), 0.1),
        'rwkv_ln_w': 1.0 + nrm((R, D), 0.02),
        'rwkv_ln_b': nrm((R, D), 0.02),
        'rwkv_w_o': nrm((R, D, D), D ** -0.5),
        'rwkv_v0': nrm((V, D), 0.5),
        'rwkv_v1': nrm((V, D, VALUE_LORA), D ** -0.5),
        'rwkv_v2': nrm((V, VALUE_LORA, D), 0.1 * VALUE_LORA ** -0.5),
        'nsa_w_in': nrm((NN, D, NSA_IN_DIM), D ** -0.5),
        'nsa_cmp_w1': nrm((NN, 2, CMP_BLOCK * dh, CMP_HIDDEN), (CMP_BLOCK * dh) ** -0.5),
        'nsa_cmp_b1': nrm((NN, 2, CMP_HIDDEN), 0.02),
        'nsa_cmp_w2': nrm((NN, 2, CMP_HIDDEN, dh), CMP_HIDDEN ** -0.5),
        'nsa_cmp_pe': nrm((NN, 2, CMP_BLOCK, dh), 0.1),
        'nsa_w_o': nrm((NN, NSA_HEADS * dh, D), (NSA_HEADS * dh) ** -0.5),
        'gla_w_in': nrm((NG, D, GLA_IN_DIM), D ** -0.5),
        'gla_a_up': nrm((NG, GLA_GATE_LORA, GLA_KEY_DIM), GLA_GATE_LORA ** -0.5),
        'gla_a_bias': nrm((NG, GLA_KEY_DIM), 0.5),
        'gla_norm_w': 1.0 + nrm((NG, gla_dv), 0.02),
        'gla_w_o': nrm((NG, GLA_VAL_DIM, D), GLA_VAL_DIM ** -0.5),
    }


def reference(x, norm_g, ffn_w_in, ffn_conv_w, ffn_conv_b, ffn_w_out,
              rwkv_mu, rwkv_w_rkv, rwkv_w0, rwkv_w1, rwkv_w2, rwkv_a0, rwkv_a1, rwkv_a2,
              rwkv_g1, rwkv_g2, rwkv_k_k, rwkv_k_a, rwkv_r_k, rwkv_ln_w, rwkv_ln_b, rwkv_w_o,
              rwkv_v0, rwkv_v1, rwkv_v2,
              nsa_w_in, nsa_cmp_w1, nsa_cmp_b1, nsa_cmp_w2, nsa_cmp_pe, nsa_w_o,
              gla_w_in, gla_a_up, gla_a_bias, gla_norm_w, gla_w_o):
    h = x
    v_first = None
    for i in range(DEPTH):
        kind, j = i % N_MIXERS, i // N_MIXERS
        y = rms_norm(h, norm_g[i, 0])
        if kind == 0:
            vres = None if j == 0 else (rwkv_v0[j - 1], rwkv_v1[j - 1], rwkv_v2[j - 1])
            y, v_first = rwkv7_time_mix(y, rwkv_mu[j], rwkv_w_rkv[j], rwkv_w0[j], rwkv_w1[j], rwkv_w2[j],
                                        rwkv_a0[j], rwkv_a1[j], rwkv_a2[j], rwkv_g1[j], rwkv_g2[j],
                                        rwkv_k_k[j], rwkv_k_a[j], rwkv_r_k[j], rwkv_ln_w[j], rwkv_ln_b[j],
                                        rwkv_w_o[j], v_first, vres)
        elif kind == 1:
            y = nsa_mix(y, nsa_w_in[j], nsa_cmp_w1[j], nsa_cmp_b1[j], nsa_cmp_w2[j], nsa_cmp_pe[j], nsa_w_o[j])
        else:
            y = gla_mix(y, gla_w_in[j], gla_a_up[j], gla_a_bias[j], gla_norm_w[j], gla_w_o[j])
        h = h + rms_norm(y, norm_g[i, 1])
        y = conv_ffn(rms_norm(h, norm_g[i, 2]), ffn_w_in[i], ffn_conv_w[i], ffn_conv_b[i], ffn_w_out[i])
        h = h + rms_norm(y, norm_g[i, 3])
    return h
```

```python
import functools

import numpy as np
import jax
import jax.numpy as jnp
from jax import lax
from jax.experimental import pallas as pl
from jax.experimental.pallas import tpu as pltpu

F32 = jnp.float32
BF16 = jnp.bfloat16

V7X_VMEM_BYTES = 64 * 1024 * 1024
VMEM_LIMIT_BYTES = V7X_VMEM_BYTES - 8 * 1024 * 1024
LANES = 128
SUBLANES = 8

D_MODEL = 1024
NORM_EPS = 1e-6
NEG_INF = -1e30

RWKV_HEAD_DIM = 64
RWKV_HEADS = D_MODEL // RWKV_HEAD_DIM
RWKV_GN_EPS = 64e-5
RWKV_CHUNK = 64
LORA_PAD = 128

NSA_HEADS = 16
NSA_KV_GROUPS = 2
NSA_HEAD_DIM = D_MODEL // NSA_HEADS
NSA_HPG = NSA_HEADS // NSA_KV_GROUPS
CMP_BLOCK = 32
CMP_STRIDE = 16
CMP_HIDDEN = 256
SLC_BLOCK = 64
SLC_TOPK = 16
WINDOW = 512
FORCE_BONUS = 1e4
NSA_TQ = 128
NSA_TK = 512

GLA_HEADS = 4
GLA_KEY_DIM = D_MODEL // 2
GLA_VAL_DIM = D_MODEL
GLA_GATE_LORA = 16
GLA_TAU = 16.0
GLA_CHUNK = 64
GLA_NORM_EPS = 1e-5

D_FF = 2816
FFN_CHUNK = 256
FFN_TM = 512
PROJ_TM = 256
POST_TM = 512


def _dot(a, b):
    return jnp.dot(a, b, preferred_element_type=F32)


def _dot_nt(a, b):
    return lax.dot_general(a, b, (((1,), (1,)), ((), ())), preferred_element_type=F32)


def _dot_tn(a, b):
    return lax.dot_general(a, b, (((0,), (0,)), ((), ())), preferred_element_type=F32)


def _split2(x):
    hi = x.astype(BF16)
    lo = (x - hi.astype(F32)).astype(BF16)
    return hi, lo


def _dot_hilo(x, m):
    hi, lo = _split2(x)
    return _dot(hi, m) + _dot(lo, m)


def _dot_tri3(tri, x):
    hi = x.astype(BF16)
    r1 = x - hi.astype(F32)
    mid = r1.astype(BF16)
    lo = (r1 - mid.astype(F32)).astype(BF16)
    return _dot(tri, hi) + _dot(tri, mid) + _dot(tri, lo)


def _rms(x, g, eps):
    return x * lax.rsqrt(jnp.mean(x * x, axis=-1, keepdims=True) + eps) * g


def _sigmoid(x):
    return 1.0 / (1.0 + jnp.exp(-x))


def _softplus(x):
    return jnp.maximum(x, 0.0) + jnp.log(1.0 + jnp.exp(-jnp.abs(x)))


def _gelu_tanh(x):
    c = np.float32(np.sqrt(2.0 / np.pi))
    return 0.5 * x * (1.0 + jnp.tanh(c * (x + 0.044715 * (x * x * x))))


def _shift_rows(x, n, prev8):
    rolled = pltpu.roll(x, n, 0)
    row = lax.broadcasted_iota(jnp.int32, x.shape, 0)
    for j in range(n):
        rolled = jnp.where(row == j, prev8[SUBLANES - n + j:SUBLANES - n + j + 1, :], rolled)
    return rolled


def _const_spec(shape):
    nd = len(shape)
    return pl.BlockSpec(shape, lambda *_: (0,) * nd, pipeline_mode=pl.Buffered(1))


def _tok_spec(tm, width):
    return pl.BlockSpec((1, tm, width), lambda b, i: (b, i, 0))


def _params():
    return pltpu.CompilerParams(dimension_semantics=("arbitrary", "arbitrary"),
                                vmem_limit_bytes=VMEM_LIMIT_BYTES)


def _row(v):
    return v.reshape(1, -1).astype(F32)


def _ffn_kernel(h_ref, gpre_ref, wg_ref, wu_ref, cw_ref, cb_ref, wo_ref, gpost_ref, out_ref,
                xn_ref, acc_ref, halo_ref, *, nc):
    @pl.when(pl.program_id(1) == 0)
    def _():
        halo_ref[...] = jnp.zeros_like(halo_ref)

    h = h_ref[0]
    xn_ref[...] = _rms(h, gpre_ref[...], NORM_EPS).astype(BF16)
    acc_ref[...] = jnp.zeros_like(acc_ref)
    tm = h.shape[0]

    def body(c, carry):
        xn = xn_ref[...]
        gate = _dot(xn, wg_ref[c])
        up = _dot(xn, wu_ref[c])
        prev = halo_ref[c]
        g1 = _shift_rows(gate, 1, prev)
        g2 = _shift_rows(gate, 2, prev)
        cw = cw_ref[c]
        z = cw[2:3] * gate + cw[1:2] * g1 + cw[0:1] * g2 + cb_ref[c]
        halo_ref[c] = gate[tm - SUBLANES:tm, :]
        act = (z * _sigmoid(z) * up).astype(BF16)
        acc_ref[...] += _dot(act, wo_ref[c])
        return carry

    lax.fori_loop(0, nc, body, 0)
    out_ref[0] = h + _rms(acc_ref[...], gpost_ref[...], NORM_EPS)


def _ffn(h, g_pre, w_in, conv_w, conv_b, w_out, g_post):
    B, S, D = h.shape
    F = w_out.shape[0]
    fc = FFN_CHUNK
    nc = F // fc
    tm = min(FFN_TM, S)
    wg = w_in[:, :F].reshape(D, nc, fc).transpose(1, 0, 2).astype(BF16)
    wu = w_in[:, F:].reshape(D, nc, fc).transpose(1, 0, 2).astype(BF16)
    cw = conv_w.reshape(3, nc, fc).transpose(1, 0, 2)
    cb = conv_b.reshape(nc, 1, fc)
    wo = w_out.reshape(nc, fc, D).astype(BF16)
    return pl.pallas_call(
        functools.partial(_ffn_kernel, nc=nc),
        grid=(B, S // tm),
        in_specs=[_tok_spec(tm, D), _const_spec((1, D)), _const_spec((nc, D, fc)),
                  _const_spec((nc, D, fc)), _const_spec((nc, 3, fc)), _const_spec((nc, 1, fc)),
                  _const_spec((nc, fc, D)), _const_spec((1, D))],
        out_specs=_tok_spec(tm, D),
        out_shape=jax.ShapeDtypeStruct((B, S, D), F32),
        scratch_shapes=[pltpu.VMEM((tm, D), BF16), pltpu.VMEM((tm, D), F32),
                        pltpu.VMEM((nc, SUBLANES, fc), F32)],
        compiler_params=_params(),
        name="conv_ffn",
    )(h, _row(g_pre), wg, wu, cw, cb, wo, _row(g_post))


def _post_kernel(h_ref, o_ref, wo_ref, g_ref, out_ref):
    y = _dot(o_ref[0].astype(BF16), wo_ref[...])
    out_ref[0] = h_ref[0] + _rms(y, g_ref[...], NORM_EPS)


def _post(h, o, w_o, g_post):
    B, S, D = h.shape
    K = o.shape[-1]
    tm = min(POST_TM, S)
    return pl.pallas_call(
        _post_kernel,
        grid=(B, S // tm),
        in_specs=[_tok_spec(tm, D), _tok_spec(tm, K), _const_spec((K, D)), _const_spec((1, D))],
        out_specs=_tok_spec(tm, D),
        out_shape=jax.ShapeDtypeStruct((B, S, D), F32),
        compiler_params=_params(),
        name="mixer_out",
    )(h, o, w_o.astype(BF16), _row(g_post))


def _rwkv_proj_kernel(*refs, has_vres):
    if has_vres:
        (h_ref, gn_ref, mu_ref, wr_ref, wk_ref, wv_ref, w1_ref, w2_ref, a1_ref, a2_ref, g1_ref,
         g2_ref, vec_ref, bsum_ref, vf_ref, v1_ref, v2_ref,
         r_ref, ld_ref, k_ref, v_ref, kkn_ref, b_ref, g_ref, bonus_ref, carry_ref) = refs
    else:
        (h_ref, gn_ref, mu_ref, wr_ref, wk_ref, wv_ref, w1_ref, w2_ref, a1_ref, a2_ref, g1_ref,
         g2_ref, vec_ref, bsum_ref,
         r_ref, ld_ref, k_ref, v_ref, kkn_ref, b_ref, g_ref, bonus_ref, carry_ref) = refs

    @pl.when(pl.program_id(1) == 0)
    def _():
        carry_ref[...] = jnp.zeros_like(carry_ref)

    x = _rms(h_ref[0], gn_ref[...], NORM_EPS)
    tm = x.shape[0]
    xx = _shift_rows(x, 1, carry_ref[...]) - x
    carry_ref[...] = x[tm - SUBLANES:tm, :]
    mu = mu_ref[...]
    xr, xw, xk, xv, xa, xg = ((x + xx * mu[c:c + 1]).astype(BF16) for c in range(6))
    vec = vec_ref[...]
    w0, a0, k_k, k_a, r_k, v0 = (vec[c:c + 1] for c in range(6))

    r = _dot(xr, wr_ref[...])
    k = _dot(xk, wk_ref[...])
    v = _dot(xv, wv_ref[...])
    wl = -_softplus(-(w0 + _dot(jnp.tanh(_dot(xw, w1_ref[...])).astype(BF16), w2_ref[...]))) - 0.5
    ld_ref[0] = -jnp.exp(wl)
    a = _sigmoid(a0 + _dot(_dot(xa, a1_ref[...]).astype(BF16), a2_ref[...]))
    g_ref[0] = _dot(_sigmoid(_dot(xg, g1_ref[...])).astype(BF16), g2_ref[...])
    if has_vres:
        mix = _sigmoid(v0 + _dot(_dot(xv, v1_ref[...]).astype(BF16), v2_ref[...]))
        v = v + (vf_ref[0] - v) * mix
    kk = k * k_k
    norm = jnp.sqrt(_dot_hilo(kk * kk, bsum_ref[...]))
    kkn = kk / jnp.maximum(norm, 1e-12)
    k = k * (1.0 + (a - 1.0) * k_a)
    r_ref[0] = r
    k_ref[0] = k
    v_ref[0] = v
    kkn_ref[0] = kkn
    b_ref[0] = kkn * a
    bonus_ref[0] = _dot_hilo(r * k * r_k, bsum_ref[...]) * v


def _pad_cols(w, n):
    return jnp.pad(w, ((0, 0), (0, n - w.shape[1])))


def _pad_rows(w, n):
    return jnp.pad(w, ((0, n - w.shape[0]), (0, 0)))


def _head_block_matrix(d, hd, value):
    m = (np.arange(d)[:, None] // hd == np.arange(d)[None, :] // hd).astype(np.float32) * value
    return jnp.asarray(m, BF16)


def _rwkv_proj(h, g_norm, mu, w_rkv, w0, w1, w2, a0, a1, a2, g1, g2, k_k, k_a, r_k, v_first, vres):
    B, S, D = h.shape
    tm = min(PROJ_TM, S)
    has_vres = vres is not None
    v0 = vres[0] if has_vres else jnp.zeros((D,), F32)
    vec = jnp.stack([w0, a0, k_k, k_a, r_k.reshape(-1), v0]).astype(F32)
    bsum = _head_block_matrix(D, RWKV_HEAD_DIM, 1.0)
    lora_in = lambda w: _pad_cols(w, LORA_PAD).astype(BF16)
    lora_out = lambda w: _pad_rows(w, LORA_PAD).astype(BF16)
    args = [h, _row(g_norm), mu, w_rkv[0].astype(BF16), w_rkv[1].astype(BF16), w_rkv[2].astype(BF16),
            lora_in(w1), lora_out(w2), lora_in(a1), lora_out(a2), lora_in(g1), lora_out(g2), vec, bsum]
    specs = [_tok_spec(tm, D), _const_spec((1, D)), _const_spec((6, D)), _const_spec((D, D)),
             _const_spec((D, D)), _const_spec((D, D)), _const_spec((D, LORA_PAD)),
             _const_spec((LORA_PAD, D)), _const_spec((D, LORA_PAD)), _const_spec((LORA_PAD, D)),
             _const_spec((D, LORA_PAD)), _const_spec((LORA_PAD, D)), _const_spec((6, D)),
             _const_spec((D, D))]
    if has_vres:
        args += [v_first, lora_in(vres[1]), lora_out(vres[2])]
        specs += [_tok_spec(tm, D), _const_spec((D, LORA_PAD)), _const_spec((LORA_PAD, D))]
    outs = pl.pallas_call(
        functools.partial(_rwkv_proj_kernel, has_vres=has_vres),
        grid=(B, S // tm),
        in_specs=specs,
        out_specs=[_tok_spec(tm, D)] * 8,
        out_shape=[jax.ShapeDtypeStruct((B, S, D), F32)] * 8,
        scratch_shapes=[pltpu.VMEM((SUBLANES, D), F32)],
        compiler_params=_params(),
        name="rwkv_proj",
    )(*args)
    return outs


def _rwkv_scan_kernel(r_ref, ld_ref, k_ref, v_ref, kkn_ref, b_ref, y_ref, st_ref):
    @pl.when(pl.program_id(1) == 0)
    def _():
        st_ref[...] = jnp.zeros_like(st_ref)

    C = RWKV_CHUNK
    N = RWKV_HEAD_DIM
    rowi = lax.broadcasted_iota(jnp.int32, (C, C), 0)
    coli = lax.broadcasted_iota(jnp.int32, (C, C), 1)
    low_incl = rowi >= coli
    low_strict = rowi > coli
    tri = low_incl.astype(BF16)
    eye = (rowi == coli).astype(F32)

    ld = ld_ref[0]
    cum = _dot_tri3(tri, ld)
    cum_last = cum[C - 1:C, :]
    p_inv = jnp.exp(-cum)
    p_rem = jnp.exp(cum_last - cum)
    p_all = jnp.exp(cum_last)
    kkn = kkn_ref[0]
    b = b_ref[0]
    k = k_ref[0]
    a_t = (-kkn * jnp.exp(cum - ld)).astype(BF16)
    r_t = (r_ref[0] * jnp.exp(cum)).astype(BF16)
    b_t = (b * p_inv).astype(BF16)
    k_t = (k * p_inv).astype(BF16)
    b_h = (b * p_rem).astype(BF16)
    k_h = (k * p_rem).astype(BF16)
    vb = v_ref[0].astype(BF16)

    for h in range(RWKV_HEADS):
        sl = slice(h * N, (h + 1) * N)
        ar = jnp.concatenate([a_t[:, sl], r_t[:, sl]], axis=0)
        s_b = _dot_nt(ar, b_t[:, sl])
        s_k = _dot_nt(ar, k_t[:, sl])
        a_ab = jnp.where(low_strict, s_b[:C], 0.0)
        a_rb = jnp.where(low_incl, s_b[C:], 0.0)
        a_ak = jnp.where(low_strict, s_k[:C], 0.0)
        a_rk = jnp.where(low_incl, s_k[C:], 0.0)
        pw = a_ab.astype(BF16)
        tinv = eye + a_ab
        span = 2
        while span < C:
            pw_f = _dot(pw, pw)
            pw = pw_f.astype(BF16)
            tinv = tinv + _dot(pw, tinv.astype(BF16))
            span *= 2
        tb = tinv.astype(BF16)
        v_h = vb[:, sl]
        a_hat = _dot(tb, a_t[:, sl])
        u_loc = _dot(tb, _dot(a_ak.astype(BF16), v_h).astype(BF16))
        st = st_ref[h]
        stb = st.astype(BF16)
        u = _dot_nt(a_hat.astype(BF16), stb) + u_loc
        ub = u.astype(BF16)
        y = _dot_nt(r_t[:, sl], stb) + _dot(a_rb.astype(BF16), ub) + _dot(a_rk.astype(BF16), v_h)
        st_ref[h] = st * p_all[:, sl] + _dot_tn(ub, b_h[:, sl]) + _dot_tn(v_h, k_h[:, sl])
        y_ref[0, :, sl] = y


def _rwkv_scan(r, ld, k, v, kkn, b):
    B, S, D = r.shape
    C = RWKV_CHUNK
    return pl.pallas_call(
        _rwkv_scan_kernel,
        grid=(B, S // C),
        in_specs=[_tok_spec(C, D)] * 6,
        out_specs=_tok_spec(C, D),
        out_shape=jax.ShapeDtypeStruct((B, S, D), F32),
        scratch_shapes=[pltpu.VMEM((RWKV_HEADS, RWKV_HEAD_DIM, RWKV_HEAD_DIM), F32)],
        compiler_params=_params(),
        name="rwkv_scan",
    )(r, ld, k, v, kkn, b)


def _rwkv_post_kernel(h_ref, y_ref, bonus_ref, g_ref, vec_ref, bavg_ref, wo_ref, gpost_ref, out_ref):
    y = y_ref[0]
    bavg = bavg_ref[...]
    d = y - _dot_hilo(y, bavg)
    var = _dot_hilo(d * d, bavg)
    vec = vec_ref[...]
    yn = d * lax.rsqrt(var + RWKV_GN_EPS) * vec[0:1] + vec[1:2]
    o = ((yn + bonus_ref[0]) * g_ref[0]).astype(BF16)
    out_ref[0] = h_ref[0] + _rms(_dot(o, wo_ref[...]), gpost_ref[...], NORM_EPS)


def _rwkv_post(h, y, bonus, g, ln_w, ln_b, w_o, g_post):
    B, S, D = h.shape
    tm = min(PROJ_TM, S)
    vec = jnp.stack([ln_w, ln_b]).astype(F32)
    bavg = _head_block_matrix(D, RWKV_HEAD_DIM, 1.0 / RWKV_HEAD_DIM)
    return pl.pallas_call(
        _rwkv_post_kernel,
        grid=(B, S // tm),
        in_specs=[_tok_spec(tm, D)] * 4 + [_const_spec((2, D)), _const_spec((D, D)),
                                           _const_spec((D, D)), _const_spec((1, D))],
        out_specs=_tok_spec(tm, D),
        out_shape=jax.ShapeDtypeStruct((B, S, D), F32),
        compiler_params=_params(),
        name="rwkv_out",
    )(h, y, bonus, g, vec, bavg, w_o.astype(BF16), _row(g_post))


def _rwkv_layer(h, g_pre, g_post, mu, w_rkv, w0, w1, w2, a0, a1, a2, g1, g2, k_k, k_a, r_k,
                ln_w, ln_b, w_o, v_first, vres):
    r, ld, k, v, kkn, b, g, bonus = _rwkv_proj(h, g_pre, mu, w_rkv, w0, w1, w2, a0, a1, a2, g1, g2,
                                               k_k, k_a, r_k, v_first, vres)
    if vres is None:
        v_first = v
    y = _rwkv_scan(r, ld, k, v, kkn, b)
    return _rwkv_post(h, y, bonus, g, ln_w, ln_b, w_o, g_post), v_first


def _gla_proj_kernel(h_ref, gn_ref, w_ref, wa_ref, aup_ref, abias_ref,
                     q_ref, k_ref, v_ref, r_ref, la_ref):
    xn = _rms(h_ref[0], gn_ref[...], NORM_EPS).astype(BF16)
    qkvr = _dot(xn, w_ref[...])
    dk = GLA_KEY_DIM
    dv = GLA_VAL_DIM
    q_ref[0] = qkvr[:, :dk] * np.float32((dk // GLA_HEADS) ** -0.5)
    k_ref[0] = qkvr[:, dk:2 * dk]
    v_ref[0] = qkvr[:, 2 * dk:2 * dk + dv]
    r_ref[0] = qkvr[:, 2 * dk + dv:]
    z = _dot(_dot(xn, wa_ref[...]).astype(BF16), aup_ref[...]) + abias_ref[...]
    la_ref[0] = -_softplus(-z) * np.float32(1.0 / GLA_TAU)


def _gla_chunk_kernel(q_ref, k_ref, v_ref, r_ref, la_ref, nw_ref, o_ref, st_ref, *, chunks):
    @pl.when(pl.program_id(1) == 0)
    def _():
        st_ref[...] = jnp.zeros_like(st_ref)

    C = GLA_CHUNK
    dk = GLA_KEY_DIM // GLA_HEADS
    dv = GLA_VAL_DIM // GLA_HEADS
    rowi = lax.broadcasted_iota(jnp.int32, (C, C), 0)
    coli = lax.broadcasted_iota(jnp.int32, (C, C), 1)
    low_incl = rowi >= coli
    tri = low_incl.astype(BF16)
    nw = nw_ref[...]

    for ci in range(chunks):
        rows = pl.ds(ci * C, C)
        bcum = _dot_tri3(tri, la_ref[0, rows, :])
        b_last = bcum[C - 1:C, :]
        k = k_ref[0, rows, :]
        qe = (q_ref[0, rows, :] * jnp.exp(bcum)).astype(BF16)
        ke = (k * jnp.exp(-bcum)).astype(BF16)
        kd = (k * jnp.exp(b_last - bcum)).astype(BF16)
        dec = jnp.exp(b_last)
        vb = v_ref[0, rows, :].astype(BF16)
        r = r_ref[0, rows, :]
        for h in range(GLA_HEADS):
            ks = slice(h * dk, (h + 1) * dk)
            vs = slice(h * dv, (h + 1) * dv)
            a = jnp.where(low_incl, _dot_nt(qe[:, ks], ke[:, ks]), 0.0)
            st = st_ref[h]
            o = _dot(a.astype(BF16), vb[:, vs]) + _dot_nt(qe[:, ks], st.astype(BF16))
            st_ref[h] = st * dec[:, ks] + _dot_tn(vb[:, vs], kd[:, ks])
            o = o * lax.rsqrt(jnp.mean(o * o, axis=-1, keepdims=True) + GLA_NORM_EPS) * nw
            rh = r[:, vs]
            o_ref[0, rows, vs] = (o * (rh * _sigmoid(rh))).astype(o_ref.dtype)


def _gla_layer(h, g_pre, g_post, w_in, a_up, a_bias, norm_w, w_o):
    B, S, D = h.shape
    dk, dv = GLA_KEY_DIM, GLA_VAL_DIM
    tm = min(PROJ_TM, S)
    n_main = 2 * dk + 2 * dv
    w_main = w_in[:, :n_main].astype(BF16)
    w_a = _pad_cols(w_in[:, n_main:], LORA_PAD).astype(BF16)
    aup = _pad_rows(a_up, LORA_PAD).astype(BF16)
    q, k, v, r, la = pl.pallas_call(
        _gla_proj_kernel,
        grid=(B, S // tm),
        in_specs=[_tok_spec(tm, D), _const_spec((1, D)), _const_spec((D, n_main)),
                  _const_spec((D, LORA_PAD)), _const_spec((LORA_PAD, dk)), _const_spec((1, dk))],
        out_specs=[_tok_spec(tm, dk), _tok_spec(tm, dk), _tok_spec(tm, dv), _tok_spec(tm, dv),
                   _tok_spec(tm, dk)],
        out_shape=[jax.ShapeDtypeStruct((B, S, w), F32) for w in (dk, dk, dv, dv, dk)],
        compiler_params=_params(),
        name="gla_proj",
    )(h, _row(g_pre), w_main, w_a, aup, _row(a_bias))

    chunks = 4 if S % (4 * GLA_CHUNK) == 0 else 1
    tc = chunks * GLA_CHUNK
    o = pl.pallas_call(
        functools.partial(_gla_chunk_kernel, chunks=chunks),
        grid=(B, S // tc),
        in_specs=[_tok_spec(tc, dk), _tok_spec(tc, dk), _tok_spec(tc, dv), _tok_spec(tc, dv),
                  _tok_spec(tc, dk), _const_spec((1, dv // GLA_HEADS))],
        out_specs=_tok_spec(tc, dv),
        out_shape=jax.ShapeDtypeStruct((B, S, dv), BF16),
        scratch_shapes=[pltpu.VMEM((GLA_HEADS, dv // GLA_HEADS, dk // GLA_HEADS), F32)],
        compiler_params=_params(),
        name="gla_chunk",
    )(q, k, v, r, la, _row(norm_w))
    return _post(h, o, w_o, g_post)


def _nsa_head_perm():
    perm = np.zeros((D_MODEL,), np.int32)
    for p in range(NSA_HPG):
        for g in range(NSA_KV_GROUPS):
            src = (g * NSA_HPG + p) * NSA_HEAD_DIM
            dst = p * LANES + g * NSA_HEAD_DIM
            perm[dst:dst + NSA_HEAD_DIM] = np.arange(src, src + NSA_HEAD_DIM)
    return perm


def _nsa_proj_kernel(h_ref, gn_ref, wq_ref, wkv_ref, wg_ref, q_ref, kv_ref, gates_ref):
    xn = _rms(h_ref[0], gn_ref[...], NORM_EPS).astype(BF16)
    q_ref[0] = _dot(xn, wq_ref[...]).astype(BF16)
    kv_ref[0] = _dot(xn, wkv_ref[...]).astype(BF16)
    gates_ref[0] = _sigmoid(_dot(xn, wg_ref[...]))


def _nsa_compress_kernel(r_ref, w1_ref, pe_ref, b1_ref, w2_ref, out_ref, *, n_cmp):
    r = r_ref[0, 0].astype(F32)
    rows = r.shape[0]
    pe = pe_ref[0]
    ra = (r + pe[0]).astype(BF16)
    rb = (pltpu.roll(r, rows - 1, 0) + pe[1]).astype(BF16)
    out = jnp.zeros((rows, LANES), F32)
    for g in range(NSA_KV_GROUPS):
        hid = _gelu_tanh(_dot(ra, w1_ref[0, g, 0]) + _dot(rb, w1_ref[0, g, 1]) + b1_ref[0])
        out = out + _dot(hid.astype(BF16), w2_ref[0, g])
    rowi = lax.broadcasted_iota(jnp.int32, out.shape, 0)
    out_ref[0, 0] = jnp.where(rowi < n_cmp, out, 0.0).astype(BF16)


def _group_queries(q, g):
    lane = lax.broadcasted_iota(jnp.int32, (q.shape[0], LANES), 1)
    keep = (lane < NSA_HEAD_DIM) if g == 0 else (lane >= NSA_HEAD_DIM)
    parts = [jnp.where(keep, q[:, p * LANES:(p + 1) * LANES], jnp.zeros((), q.dtype))
             for p in range(NSA_HPG)]
    return jnp.concatenate(parts, axis=0)


def _topk_mask(score, k):
    nb = score.shape[-1]
    lane = lax.broadcasted_iota(jnp.int32, score.shape, 1).astype(F32)

    def body(_, carry):
        sc, sel = carry
        m = jnp.max(sc, axis=-1, keepdims=True)
        idx = jnp.min(jnp.where(sc == m, lane, np.float32(nb)), axis=-1, keepdims=True)
        hit = lane == idx
        return jnp.where(hit, np.float32(-3e38), sc), jnp.where(hit, 1.0, sel)

    _, sel = lax.fori_loop(0, k, body, (score, jnp.zeros_like(score)))
    return sel


def _nsa_cmp_kernel(q_ref, kc_ref, vc_ref, ov_ref, ocmp_ref, sel_ref, *, n_cmp, n_slc, top_n):
    TQ = q_ref.shape[1]
    ncp = kc_ref.shape[1]
    nbp = ov_ref.shape[1]
    t0 = pl.program_id(1) * TQ
    q = q_ref[0]
    kc = kc_ref[0]
    vc = vc_ref[0]
    pos = t0 + lax.broadcasted_iota(jnp.int32, (TQ, 1), 0)
    blk_c = lax.broadcasted_iota(jnp.int32, (TQ, ncp), 1)
    vis = (blk_c * CMP_STRIDE + (CMP_BLOCK - 1) <= pos) & (blk_c < n_cmp)
    any_vis = pos >= CMP_BLOCK - 1
    blk_s = lax.broadcasted_iota(jnp.int32, (TQ, nbp), 1)
    cur = lax.shift_right_logical(pos, 6)
    causal = (blk_s * SLC_BLOCK <= pos) & (blk_s < n_slc)
    forced = (blk_s == 0) | (blk_s == cur) | (blk_s == cur - 1)
    o_groups = []
    for g in range(NSA_KV_GROUPS):
        qg = _group_queries(q, g)
        s = _dot_nt(qg, kc).reshape(NSA_HPG, TQ, ncp)
        s = jnp.where(vis[None], s, NEG_INF)
        e = jnp.exp(s - jnp.max(s, axis=-1, keepdims=True))
        p = e / jnp.sum(e, axis=-1, keepdims=True)
        p = jnp.where(any_vis[None], p, 0.0)
        o_groups.append(_dot(p.reshape(NSA_HPG * TQ, ncp).astype(BF16), vc))
        imp = _dot_hilo(jnp.sum(p, axis=0), ov_ref[...])
        score = jnp.where(causal, imp + np.float32(FORCE_BONUS) * forced.astype(F32), NEG_INF)
        sel_ref[0, :, g * nbp:(g + 1) * nbp] = _topk_mask(score, top_n).astype(sel_ref.dtype)
    lane = lax.broadcasted_iota(jnp.int32, (TQ, LANES), 1)
    for p_i in range(NSA_HPG):
        rows = slice(p_i * TQ, (p_i + 1) * TQ)
        ocmp_ref[0, :, p_i * LANES:(p_i + 1) * LANES] = jnp.where(
            lane < NSA_HEAD_DIM, o_groups[0][rows], o_groups[1][rows])


def _nsa_attn_kernel(q_ref, ks_ref, vs_ref, kw_ref, vw_ref, sel_ref, ocmp_ref, gates_ref, ge_ref,
                     o_ref, m_ref, l_ref, acc_ref, *, win_rows):
    TQ = q_ref.shape[1]
    S = ks_ref.shape[1]
    TK = min(NSA_TK, S)
    nbp = sel_ref.shape[2] // NSA_KV_GROUPS
    R = NSA_HPG * TQ
    t0 = pl.program_id(1) * TQ
    q = q_ref[0]
    pos = t0 + lax.broadcasted_iota(jnp.int32, (TQ, 1), 0)
    n_kt = (t0 + TQ + TK - 1) // TK
    blk_j = lax.broadcasted_iota(jnp.int32, (nbp, TK), 0)
    key_c = lax.broadcasted_iota(jnp.int32, (nbp, TK), 1)
    key_row = lax.broadcasted_iota(jnp.int32, (1, TK), 1)
    w_start = pl.multiple_of(jnp.maximum(t0 - WINDOW, 0), TQ)
    wpos = w_start + lax.broadcasted_iota(jnp.int32, (1, win_rows), 1)
    wvalid = (wpos <= pos) & (wpos > pos - WINDOW)

    o_slc = []
    o_win = []
    for g in range(NSA_KV_GROUPS):
        qg = _group_queries(q, g)
        selg = sel_ref[0, :, g * nbp:(g + 1) * nbp]

        m_ref[...] = jnp.full(m_ref.shape, NEG_INF, F32)
        l_ref[...] = jnp.zeros_like(l_ref)
        acc_ref[...] = jnp.zeros_like(acc_ref)

        def body(kt, carry):
            k0 = pl.multiple_of(kt * TK, TK)
            expand = (blk_j == lax.shift_right_logical(k0 + key_c, 6)).astype(BF16)
            chosen = _dot(selg, expand)
            valid = (chosen > 0.5) & (k0 + key_row <= pos)
            s = _dot_nt(qg, ks_ref[0, pl.ds(k0, TK), :]).reshape(NSA_HPG, TQ, TK)
            s = jnp.where(valid[None], s, NEG_INF)
            m_old = m_ref[...]
            m_new = jnp.maximum(m_old, jnp.max(s, axis=-1, keepdims=True))
            alpha = jnp.exp(m_old - m_new)
            p = jnp.exp(s - m_new)
            l_ref[...] = l_ref[...] * alpha + jnp.sum(p, axis=-1, keepdims=True)
            pv = _dot(p.reshape(R, TK).astype(BF16), vs_ref[0, pl.ds(k0, TK), :])
            acc_ref[...] = acc_ref[...] * alpha + pv.reshape(NSA_HPG, TQ, LANES)
            m_ref[...] = m_new
            return carry

        lax.fori_loop(0, n_kt, body, 0)
        o_slc.append(acc_ref[...] / l_ref[...])

        s = _dot_nt(qg, kw_ref[0, pl.ds(w_start, win_rows), :]).reshape(NSA_HPG, TQ, win_rows)
        s = jnp.where(wvalid[None], s, NEG_INF)
        e = jnp.exp(s - jnp.max(s, axis=-1, keepdims=True))
        p = e / jnp.sum(e, axis=-1, keepdims=True)
        pv = _dot(p.reshape(R, win_rows).astype(BF16), vw_ref[0, pl.ds(w_start, win_rows), :])
        o_win.append(pv.reshape(NSA_HPG, TQ, LANES))

    gates = gates_ref[0]
    g_cmp = _dot_hilo(gates, ge_ref[0])
    g_slc = _dot_hilo(gates, ge_ref[1])
    g_win = _dot_hilo(gates, ge_ref[2])
    lane = lax.broadcasted_iota(jnp.int32, (TQ, LANES), 1)
    first = lane < NSA_HEAD_DIM
    for p_i in range(NSA_HPG):
        cols = slice(p_i * LANES, (p_i + 1) * LANES)
        slc = jnp.where(first, o_slc[0][p_i], o_slc[1][p_i])
        win = jnp.where(first, o_win[0][p_i], o_win[1][p_i])
        o_ref[0, :, cols] = (g_cmp[:, cols] * ocmp_ref[0, :, cols] + g_slc[:, cols] * slc
                             + g_win[:, cols] * win).astype(o_ref.dtype)


def _nsa_layer(h, g_pre, g_post, w_in, cmp_w1, cmp_b1, cmp_w2, cmp_pe, w_o):
    B, S, D = h.shape
    H, G, dh, hpg = NSA_HEADS, NSA_KV_GROUPS, NSA_HEAD_DIM, NSA_HPG
    perm = _nsa_head_perm()
    n_q = H * dh
    n_kv = 6 * G * dh
    wq = (w_in[:, :n_q] * np.float32(dh ** -0.5))[:, perm].astype(BF16)
    wkv = w_in[:, n_q:n_q + n_kv].astype(BF16)
    wgt = _pad_cols(w_in[:, n_q + n_kv:], LANES).astype(BF16)
    tm = min(PROJ_TM, S)
    q, kv, gates = pl.pallas_call(
        _nsa_proj_kernel,
        grid=(B, S // tm),
        in_specs=[_tok_spec(tm, D), _const_spec((1, D)), _const_spec((D, n_q)),
                  _const_spec((D, n_kv)), _const_spec((D, LANES))],
        out_specs=[_tok_spec(tm, n_q), _tok_spec(tm, n_kv), _tok_spec(tm, LANES)],
        out_shape=[jax.ShapeDtypeStruct((B, S, n_q), BF16), jax.ShapeDtypeStruct((B, S, n_kv), BF16),
                   jax.ShapeDtypeStruct((B, S, LANES), F32)],
        compiler_params=_params(),
        name="nsa_proj",
    )(h, _row(g_pre), wq, wkv, wgt)

    n_cmp = (S - CMP_BLOCK) // CMP_STRIDE + 1
    ncp = S // CMP_STRIDE
    half = CMP_BLOCK // 2
    assert CMP_STRIDE == half
    rk = jnp.stack([kv[:, :, :LANES], kv[:, :, LANES:2 * LANES]]).reshape(2, B, ncp, half * LANES)
    w1 = cmp_w1.reshape(2, 2, half, dh, CMP_HIDDEN)
    w1x = jnp.zeros((2, G, 2, half, G, dh, CMP_HIDDEN), F32)
    for g in range(G):
        w1x = w1x.at[:, g, :, :, g].set(w1)
    w1x = w1x.reshape(2, G, 2, half * LANES, CMP_HIDDEN).astype(BF16)
    pe = jnp.broadcast_to(cmp_pe.reshape(2, 2, half, 1, dh), (2, 2, half, G, dh))
    pe = pe.reshape(2, 2, 1, half * LANES).astype(F32)
    w2x = jnp.zeros((2, G, CMP_HIDDEN, G, dh), F32)
    for g in range(G):
        w2x = w2x.at[:, g, :, g].set(cmp_w2)
    w2x = w2x.reshape(2, G, CMP_HIDDEN, LANES).astype(BF16)
    kvc = pl.pallas_call(
        functools.partial(_nsa_compress_kernel, n_cmp=n_cmp),
        grid=(2, B),
        in_specs=[pl.BlockSpec((1, 1, ncp, half * LANES), lambda c, b: (c, b, 0, 0)),
                  pl.BlockSpec((1, G, 2, half * LANES, CMP_HIDDEN), lambda c, b: (c, 0, 0, 0, 0)),
                  pl.BlockSpec((1, 2, 1, half * LANES), lambda c, b: (c, 0, 0, 0)),
                  pl.BlockSpec((1, 1, CMP_HIDDEN), lambda c, b: (c, 0, 0)),
                  pl.BlockSpec((1, G, CMP_HIDDEN, LANES), lambda c, b: (c, 0, 0, 0))],
        out_specs=pl.BlockSpec((1, 1, ncp, LANES), lambda c, b: (c, b, 0, 0)),
        out_shape=jax.ShapeDtypeStruct((2, B, ncp, LANES), BF16),
        compiler_params=_params(),
        name="nsa_compress",
    )(rk, w1x, pe, cmp_b1.reshape(2, 1, CMP_HIDDEN).astype(F32), w2x)

    n_slc = S // SLC_BLOCK
    top_n = min(SLC_TOPK, n_slc)
    nbp = max(LANES, n_slc)
    start = np.arange(ncp) * CMP_STRIDE
    end = start + CMP_BLOCK - 1
    s_start = np.arange(nbp) * SLC_BLOCK
    s_end = s_start + SLC_BLOCK - 1
    overlap = ((end[:, None] >= s_start[None, :]) & (start[:, None] <= s_end[None, :])
               & (np.arange(ncp)[:, None] < n_cmp) & (np.arange(nbp)[None, :] < n_slc))
    overlap = jnp.asarray(overlap.astype(np.float32), BF16)
    TQ = min(NSA_TQ, S)
    o_cmp, sel = pl.pallas_call(
        functools.partial(_nsa_cmp_kernel, n_cmp=n_cmp, n_slc=n_slc, top_n=top_n),
        grid=(B, S // TQ),
        in_specs=[_tok_spec(TQ, n_q),
                  pl.BlockSpec((None, 1, ncp, LANES), lambda b, i: (0, b, 0, 0)),
                  pl.BlockSpec((None, 1, ncp, LANES), lambda b, i: (1, b, 0, 0)),
                  _const_spec((ncp, nbp))],
        out_specs=[_tok_spec(TQ, n_q), _tok_spec(TQ, G * nbp)],
        out_shape=[jax.ShapeDtypeStruct((B, S, n_q), F32),
                   jax.ShapeDtypeStruct((B, S, G * nbp), BF16)],
        compiler_params=_params(),
        name="nsa_compressed_attn",
    )(q, kvc, kvc, overlap)

    ge = np.zeros((3, LANES, n_q), np.float32)
    for g in range(G):
        for p in range(hpg):
            for j in range(3):
                ge[j, g * hpg * 3 + p * 3 + j, p * LANES + g * dh:p * LANES + (g + 1) * dh] = 1.0
    ge = jnp.asarray(ge, BF16)
    win_rows = min(WINDOW + TQ, S)
    kv_spec = lambda c: pl.BlockSpec((1, S, LANES), lambda b, i, c=c: (b, 0, c))
    o = pl.pallas_call(
        functools.partial(_nsa_attn_kernel, win_rows=win_rows),
        grid=(B, S // TQ),
        in_specs=[_tok_spec(TQ, n_q), kv_spec(2), kv_spec(3), kv_spec(4), kv_spec(5),
                  _tok_spec(TQ, G * nbp), _tok_spec(TQ, n_q), _tok_spec(TQ, LANES),
                  _const_spec((3, LANES, n_q))],
        out_specs=_tok_spec(TQ, n_q),
        out_shape=jax.ShapeDtypeStruct((B, S, n_q), BF16),
        scratch_shapes=[pltpu.VMEM((hpg, TQ, 1), F32), pltpu.VMEM((hpg, TQ, 1), F32),
                        pltpu.VMEM((hpg, TQ, LANES), F32)],
        compiler_params=_params(),
        name="nsa_selected_window_attn",
    )(q, kv, kv, kv, kv, sel, o_cmp, gates, ge)
    return _post(h, o, w_o[perm, :], g_post)


def kernel(x, norm_g, ffn_w_in, ffn_conv_w, ffn_conv_b, ffn_w_out, rwkv_mu, rwkv_w_rkv, rwkv_w0, rwkv_w1, rwkv_w2, rwkv_a0, rwkv_a1, rwkv_a2, rwkv_g1, rwkv_g2, rwkv_k_k, rwkv_k_a, rwkv_r_k, rwkv_ln_w, rwkv_ln_b, rwkv_w_o, rwkv_v0, rwkv_v1, rwkv_v2, nsa_w_in, nsa_cmp_w1, nsa_cmp_b1, nsa_cmp_w2, nsa_cmp_pe, nsa_w_o, gla_w_in, gla_a_up, gla_a_bias, gla_norm_w, gla_w_o):
    depth = norm_g.shape[0]
    h = x
    v_first = None
    for i in range(depth):
        kind, j = i % 3, i // 3
        if kind == 0:
            vres = None if j == 0 else (rwkv_v0[j - 1], rwkv_v1[j - 1], rwkv_v2[j - 1])
            h, v_first = _rwkv_layer(
                h, norm_g[i, 0], norm_g[i, 1], rwkv_mu[j], rwkv_w_rkv[j], rwkv_w0[j], rwkv_w1[j],
                rwkv_w2[j], rwkv_a0[j], rwkv_a1[j], rwkv_a2[j], rwkv_g1[j], rwkv_g2[j], rwkv_k_k[j],
                rwkv_k_a[j], rwkv_r_k[j], rwkv_ln_w[j], rwkv_ln_b[j], rwkv_w_o[j], v_first, vres)
        elif kind == 1:
            h = _nsa_layer(h, norm_g[i, 0], norm_g[i, 1], nsa_w_in[j], nsa_cmp_w1[j], nsa_cmp_b1[j],
                           nsa_cmp_w2[j], nsa_cmp_pe[j], nsa_w_o[j])
        else:
            h = _gla_layer(h, norm_g[i, 0], norm_g[i, 1], gla_w_in[j], gla_a_up[j], gla_a_bias[j],
                           gla_norm_w[j], gla_w_o[j])
        h = _ffn(h, norm_g[i, 2], ffn_w_in[i], ffn_conv_w[i], ffn_conv_b[i], ffn_w_out[i],
                 norm_g[i, 3])
    return h
```

```python
import functools

import numpy as np
import jax
import jax.numpy as jnp
from jax import lax
from jax.experimental import pallas as pl
from jax.experimental.pallas import tpu as pltpu

F32 = jnp.float32
BF16 = jnp.bfloat16

V7X_VMEM_BYTES = 64 * 1024 * 1024
VMEM_LIMIT_BYTES = V7X_VMEM_BYTES - 8 * 1024 * 1024
LANES = 128
SUBLANES = 8

D_MODEL = 1024
NORM_EPS = 1e-6
NEG_INF = -1e30

RWKV_HEAD_DIM = 64
RWKV_HEADS = D_MODEL // RWKV_HEAD_DIM
RWKV_GN_EPS = 64e-5
RWKV_CHUNK = 64
LORA_PAD = 128

NSA_HEADS = 16
NSA_KV_GROUPS = 2
NSA_HEAD_DIM = D_MODEL // NSA_HEADS
NSA_HPG = NSA_HEADS // NSA_KV_GROUPS
CMP_BLOCK = 32
CMP_STRIDE = 16
CMP_HIDDEN = 256
SLC_BLOCK = 64
SLC_TOPK = 16
WINDOW = 512
FORCE_BONUS = 1e4
NSA_TQ = 128
NSA_TK = 512
NSA_TOPK_ROWS = 1024

GLA_HEADS = 4
GLA_KEY_DIM = D_MODEL // 2
GLA_VAL_DIM = D_MODEL
GLA_GATE_LORA = 16
GLA_TAU = 16.0
GLA_CHUNK = 64
GLA_NORM_EPS = 1e-5

D_FF = 2816
FFN_CHUNK = 256
FFN_TM = 512
PROJ_TM = 256
POST_TM = 512


def _dot(a, b):
    return jnp.dot(a, b, preferred_element_type=F32)


def _dot_nt(a, b):
    return lax.dot_general(a, b, (((1,), (1,)), ((), ())), preferred_element_type=F32)


def _dot_tn(a, b):
    return lax.dot_general(a, b, (((0,), (0,)), ((), ())), preferred_element_type=F32)


def _split2(x):
    hi = x.astype(BF16)
    lo = (x - hi.astype(F32)).astype(BF16)
    return hi, lo


def _dot_hilo(x, m):
    hi, lo = _split2(x)
    return _dot(hi, m) + _dot(lo, m)


def _dot_tri3(tri, x):
    hi = x.astype(BF16)
    r1 = x - hi.astype(F32)
    mid = r1.astype(BF16)
    lo = (r1 - mid.astype(F32)).astype(BF16)
    return _dot(tri, hi) + _dot(tri, mid) + _dot(tri, lo)


def _rms(x, g, eps):
    return x * lax.rsqrt(jnp.mean(x * x, axis=-1, keepdims=True) + eps) * g


def _sigmoid(x):
    return 1.0 / (1.0 + jnp.exp(-x))


def _softplus(x):
    return jnp.maximum(x, 0.0) + jnp.log(1.0 + jnp.exp(-jnp.abs(x)))


def _gelu_tanh(x):
    c = np.float32(np.sqrt(2.0 / np.pi))
    return 0.5 * x * (1.0 + jnp.tanh(c * (x + 0.044715 * (x * x * x))))


def _shift_rows(x, n, prev8):
    rolled = pltpu.roll(x, n, 0)
    row = lax.broadcasted_iota(jnp.int32, x.shape, 0)
    for j in range(n):
        rolled = jnp.where(row == j, prev8[SUBLANES - n + j:SUBLANES - n + j + 1, :], rolled)
    return rolled


def _const_spec(shape):
    nd = len(shape)
    return pl.BlockSpec(shape, lambda *_: (0,) * nd, pipeline_mode=pl.Buffered(1))


def _tok_spec(tm, width):
    return pl.BlockSpec((1, tm, width), lambda b, i: (b, i, 0))


def _params():
    return pltpu.CompilerParams(dimension_semantics=("arbitrary", "arbitrary"),
                                vmem_limit_bytes=VMEM_LIMIT_BYTES)


def _row(v):
    return v.reshape(1, -1).astype(F32)


def _ffn_kernel(h_ref, gpre_ref, wg_ref, wu_ref, cw_ref, cb_ref, wo_ref, gpost_ref, out_ref,
                xn_ref, acc_ref, halo_ref, *, nc):
    @pl.when(pl.program_id(1) == 0)
    def _():
        halo_ref[...] = jnp.zeros_like(halo_ref)

    h = h_ref[0]
    xn_ref[...] = _rms(h, gpre_ref[...], NORM_EPS).astype(BF16)
    acc_ref[...] = jnp.zeros_like(acc_ref)
    tm = h.shape[0]

    def body(c, carry):
        xn = xn_ref[...]
        gate = _dot(xn, wg_ref[c])
        up = _dot(xn, wu_ref[c])
        prev = halo_ref[c]
        g1 = _shift_rows(gate, 1, prev)
        g2 = _shift_rows(gate, 2, prev)
        cw = cw_ref[c]
        z = cw[2:3] * gate + cw[1:2] * g1 + cw[0:1] * g2 + cb_ref[c]
        halo_ref[c] = gate[tm - SUBLANES:tm, :]
        act = (z * _sigmoid(z) * up).astype(BF16)
        acc_ref[...] += _dot(act, wo_ref[c])
        return carry

    lax.fori_loop(0, nc, body, 0)
    out_ref[0] = h + _rms(acc_ref[...], gpost_ref[...], NORM_EPS)


def _ffn(h, g_pre, w_in, conv_w, conv_b, w_out, g_post):
    B, S, D = h.shape
    F = w_out.shape[0]
    fc = FFN_CHUNK
    nc = F // fc
    tm = min(FFN_TM, S)
    wg = w_in[:, :F].reshape(D, nc, fc).transpose(1, 0, 2).astype(BF16)
    wu = w_in[:, F:].reshape(D, nc, fc).transpose(1, 0, 2).astype(BF16)
    cw = conv_w.reshape(3, nc, fc).transpose(1, 0, 2)
    cb = conv_b.reshape(nc, 1, fc)
    wo = w_out.reshape(nc, fc, D).astype(BF16)
    return pl.pallas_call(
        functools.partial(_ffn_kernel, nc=nc),
        grid=(B, S // tm),
        in_specs=[_tok_spec(tm, D), _const_spec((1, D)), _const_spec((nc, D, fc)),
                  _const_spec((nc, D, fc)), _const_spec((nc, 3, fc)), _const_spec((nc, 1, fc)),
                  _const_spec((nc, fc, D)), _const_spec((1, D))],
        out_specs=_tok_spec(tm, D),
        out_shape=jax.ShapeDtypeStruct((B, S, D), F32),
        scratch_shapes=[pltpu.VMEM((tm, D), BF16), pltpu.VMEM((tm, D), F32),
                        pltpu.VMEM((nc, SUBLANES, fc), F32)],
        compiler_params=_params(),
        name="conv_ffn",
    )(h, _row(g_pre), wg, wu, cw, cb, wo, _row(g_post))


def _post_kernel(h_ref, o_ref, wo_ref, g_ref, out_ref):
    y = _dot(o_ref[0].astype(BF16), wo_ref[...])
    out_ref[0] = h_ref[0] + _rms(y, g_ref[...], NORM_EPS)


def _post(h, o, w_o, g_post):
    B, S, D = h.shape
    K = o.shape[-1]
    tm = min(POST_TM, S)
    return pl.pallas_call(
        _post_kernel,
        grid=(B, S // tm),
        in_specs=[_tok_spec(tm, D), _tok_spec(tm, K), _const_spec((K, D)), _const_spec((1, D))],
        out_specs=_tok_spec(tm, D),
        out_shape=jax.ShapeDtypeStruct((B, S, D), F32),
        compiler_params=_params(),
        name="mixer_out",
    )(h, o, w_o.astype(BF16), _row(g_post))


def _rwkv_proj_kernel(*refs, has_vres):
    if has_vres:
        (h_ref, gn_ref, mu_ref, wr_ref, wk_ref, wv_ref, w1_ref, w2_ref, a1_ref, a2_ref, g1_ref,
         g2_ref, vec_ref, bsum_ref, vf_ref, v1_ref, v2_ref,
         r_ref, ld_ref, k_ref, v_ref, kkn_ref, b_ref, g_ref, bonus_ref, carry_ref) = refs
    else:
        (h_ref, gn_ref, mu_ref, wr_ref, wk_ref, wv_ref, w1_ref, w2_ref, a1_ref, a2_ref, g1_ref,
         g2_ref, vec_ref, bsum_ref,
         r_ref, ld_ref, k_ref, v_ref, kkn_ref, b_ref, g_ref, bonus_ref, carry_ref) = refs

    @pl.when(pl.program_id(1) == 0)
    def _():
        carry_ref[...] = jnp.zeros_like(carry_ref)

    x = _rms(h_ref[0], gn_ref[...], NORM_EPS)
    tm = x.shape[0]
    xx = _shift_rows(x, 1, carry_ref[...]) - x
    carry_ref[...] = x[tm - SUBLANES:tm, :]
    mu = mu_ref[...]
    xr, xw, xk, xv, xa, xg = ((x + xx * mu[c:c + 1]).astype(BF16) for c in range(6))
    vec = vec_ref[...]
    w0, a0, k_k, k_a, r_k, v0 = (vec[c:c + 1] for c in range(6))

    r = _dot(xr, wr_ref[...])
    k = _dot(xk, wk_ref[...])
    v = _dot(xv, wv_ref[...])
    wl = -_softplus(-(w0 + _dot(jnp.tanh(_dot(xw, w1_ref[...])).astype(BF16), w2_ref[...]))) - 0.5
    ld_ref[0] = -jnp.exp(wl)
    a = _sigmoid(a0 + _dot(_dot(xa, a1_ref[...]).astype(BF16), a2_ref[...]))
    g_ref[0] = _dot(_sigmoid(_dot(xg, g1_ref[...])).astype(BF16), g2_ref[...])
    if has_vres:
        mix = _sigmoid(v0 + _dot(_dot(xv, v1_ref[...]).astype(BF16), v2_ref[...]))
        v = v + (vf_ref[0] - v) * mix
    kk = k * k_k
    norm = jnp.sqrt(_dot_hilo(kk * kk, bsum_ref[...]))
    kkn = kk / jnp.maximum(norm, 1e-12)
    k = k * (1.0 + (a - 1.0) * k_a)
    r_ref[0] = r
    k_ref[0] = k
    v_ref[0] = v
    kkn_ref[0] = kkn
    b_ref[0] = kkn * a
    bonus_ref[0] = _dot_hilo(r * k * r_k, bsum_ref[...]) * v


def _pad_cols(w, n):
    return jnp.pad(w, ((0, 0), (0, n - w.shape[1])))


def _pad_rows(w, n):
    return jnp.pad(w, ((0, n - w.shape[0]), (0, 0)))


def _head_block_matrix(d, hd, value):
    m = (np.arange(d)[:, None] // hd == np.arange(d)[None, :] // hd).astype(np.float32) * value
    return jnp.asarray(m, BF16)


def _rwkv_proj(h, g_norm, mu, w_rkv, w0, w1, w2, a0, a1, a2, g1, g2, k_k, k_a, r_k, v_first, vres):
    B, S, D = h.shape
    tm = min(PROJ_TM, S)
    has_vres = vres is not None
    v0 = vres[0] if has_vres else jnp.zeros((D,), F32)
    vec = jnp.stack([w0, a0, k_k, k_a, r_k.reshape(-1), v0]).astype(F32)
    bsum = _head_block_matrix(D, RWKV_HEAD_DIM, 1.0)
    lora_in = lambda w: _pad_cols(w, LORA_PAD).astype(BF16)
    lora_out = lambda w: _pad_rows(w, LORA_PAD).astype(BF16)
    args = [h, _row(g_norm), mu, w_rkv[0].astype(BF16), w_rkv[1].astype(BF16), w_rkv[2].astype(BF16),
            lora_in(w1), lora_out(w2), lora_in(a1), lora_out(a2), lora_in(g1), lora_out(g2), vec, bsum]
    specs = [_tok_spec(tm, D), _const_spec((1, D)), _const_spec((6, D)), _const_spec((D, D)),
             _const_spec((D, D)), _const_spec((D, D)), _const_spec((D, LORA_PAD)),
             _const_spec((LORA_PAD, D)), _const_spec((D, LORA_PAD)), _const_spec((LORA_PAD, D)),
             _const_spec((D, LORA_PAD)), _const_spec((LORA_PAD, D)), _const_spec((6, D)),
             _const_spec((D, D))]
    if has_vres:
        args += [v_first, lora_in(vres[1]), lora_out(vres[2])]
        specs += [_tok_spec(tm, D), _const_spec((D, LORA_PAD)), _const_spec((LORA_PAD, D))]
    outs = pl.pallas_call(
        functools.partial(_rwkv_proj_kernel, has_vres=has_vres),
        grid=(B, S // tm),
        in_specs=specs,
        out_specs=[_tok_spec(tm, D)] * 8,
        out_shape=[jax.ShapeDtypeStruct((B, S, D), F32)] * 8,
        scratch_shapes=[pltpu.VMEM((SUBLANES, D), F32)],
        compiler_params=_params(),
        name="rwkv_proj",
    )(*args)
    return outs


def _rwkv_scan_kernel(r_ref, ld_ref, k_ref, v_ref, kkn_ref, b_ref, y_ref, st_ref):
    @pl.when(pl.program_id(1) == 0)
    def _():
        st_ref[...] = jnp.zeros_like(st_ref)

    C = RWKV_CHUNK
    N = RWKV_HEAD_DIM
    rowi = lax.broadcasted_iota(jnp.int32, (C, C), 0)
    coli = lax.broadcasted_iota(jnp.int32, (C, C), 1)
    low_incl = rowi >= coli
    low_strict = rowi > coli
    tri = low_incl.astype(BF16)
    eye = (rowi == coli).astype(F32)

    ld = ld_ref[0]
    cum = _dot_tri3(tri, ld)
    cum_last = cum[C - 1:C, :]
    p_inv = jnp.exp(-cum)
    p_rem = jnp.exp(cum_last - cum)
    p_all = jnp.exp(cum_last)
    kkn = kkn_ref[0]
    b = b_ref[0]
    k = k_ref[0]
    a_t = (-kkn * jnp.exp(cum - ld)).astype(BF16)
    r_t = (r_ref[0] * jnp.exp(cum)).astype(BF16)
    b_t = (b * p_inv).astype(BF16)
    k_t = (k * p_inv).astype(BF16)
    b_h = (b * p_rem).astype(BF16)
    k_h = (k * p_rem).astype(BF16)
    vb = v_ref[0].astype(BF16)

    heads = range(RWKV_HEADS)
    sls = [slice(h * N, (h + 1) * N) for h in heads]
    ar = [jnp.concatenate([a_t[:, sl], r_t[:, sl]], axis=0) for sl in sls]
    s_b = [_dot_nt(ar[h], b_t[:, sls[h]]) for h in heads]
    s_k = [_dot_nt(ar[h], k_t[:, sls[h]]) for h in heads]
    a_ab = [jnp.where(low_strict, s_b[h][:C], 0.0) for h in heads]
    a_rb = [jnp.where(low_incl, s_b[h][C:], 0.0).astype(BF16) for h in heads]
    a_ak = [jnp.where(low_strict, s_k[h][:C], 0.0).astype(BF16) for h in heads]
    a_rk = [jnp.where(low_incl, s_k[h][C:], 0.0).astype(BF16) for h in heads]
    v_h = [vb[:, sl] for sl in sls]
    akv = [_dot(a_ak[h], v_h[h]).astype(BF16) for h in heads]
    pw = [a_ab[h].astype(BF16) for h in heads]
    tinv = [eye + a_ab[h] for h in heads]
    span = 2
    while span < C:
        pw = [_dot(pw[h], pw[h]).astype(BF16) for h in heads]
        tinv = [tinv[h] + _dot(pw[h], tinv[h].astype(BF16)) for h in heads]
        span *= 2
    tb = [tinv[h].astype(BF16) for h in heads]
    a_hat = [_dot(tb[h], a_t[:, sls[h]]).astype(BF16) for h in heads]
    u_loc = [_dot(tb[h], akv[h]) for h in heads]
    st = [st_ref[h] for h in heads]
    stb = [st[h].astype(BF16) for h in heads]
    ub = [(_dot_nt(a_hat[h], stb[h]) + u_loc[h]).astype(BF16) for h in heads]
    for h in heads:
        y_ref[0, :, sls[h]] = (_dot_nt(r_t[:, sls[h]], stb[h]) + _dot(a_rb[h], ub[h])
                               + _dot(a_rk[h], v_h[h]))
    for h in heads:
        st_ref[h] = (st[h] * p_all[:, sls[h]] + _dot_tn(ub[h], b_h[:, sls[h]])
                     + _dot_tn(v_h[h], k_h[:, sls[h]]))


def _rwkv_scan(r, ld, k, v, kkn, b):
    B, S, D = r.shape
    C = RWKV_CHUNK
    return pl.pallas_call(
        _rwkv_scan_kernel,
        grid=(B, S // C),
        in_specs=[_tok_spec(C, D)] * 6,
        out_specs=_tok_spec(C, D),
        out_shape=jax.ShapeDtypeStruct((B, S, D), F32),
        scratch_shapes=[pltpu.VMEM((RWKV_HEADS, RWKV_HEAD_DIM, RWKV_HEAD_DIM), F32)],
        compiler_params=_params(),
        name="rwkv_scan",
    )(r, ld, k, v, kkn, b)


def _rwkv_post_kernel(h_ref, y_ref, bonus_ref, g_ref, vec_ref, bavg_ref, wo_ref, gpost_ref, out_ref):
    y = y_ref[0]
    bavg = bavg_ref[...]
    d = y - _dot_hilo(y, bavg)
    var = _dot_hilo(d * d, bavg)
    vec = vec_ref[...]
    yn = d * lax.rsqrt(var + RWKV_GN_EPS) * vec[0:1] + vec[1:2]
    o = ((yn + bonus_ref[0]) * g_ref[0]).astype(BF16)
    out_ref[0] = h_ref[0] + _rms(_dot(o, wo_ref[...]), gpost_ref[...], NORM_EPS)


def _rwkv_post(h, y, bonus, g, ln_w, ln_b, w_o, g_post):
    B, S, D = h.shape
    tm = min(PROJ_TM, S)
    vec = jnp.stack([ln_w, ln_b]).astype(F32)
    bavg = _head_block_matrix(D, RWKV_HEAD_DIM, 1.0 / RWKV_HEAD_DIM)
    return pl.pallas_call(
        _rwkv_post_kernel,
        grid=(B, S // tm),
        in_specs=[_tok_spec(tm, D)] * 4 + [_const_spec((2, D)), _const_spec((D, D)),
                                           _const_spec((D, D)), _const_spec((1, D))],
        out_specs=_tok_spec(tm, D),
        out_shape=jax.ShapeDtypeStruct((B, S, D), F32),
        compiler_params=_params(),
        name="rwkv_out",
    )(h, y, bonus, g, vec, bavg, w_o.astype(BF16), _row(g_post))


def _rwkv_layer(h, g_pre, g_post, mu, w_rkv, w0, w1, w2, a0, a1, a2, g1, g2, k_k, k_a, r_k,
                ln_w, ln_b, w_o, v_first, vres):
    r, ld, k, v, kkn, b, g, bonus = _rwkv_proj(h, g_pre, mu, w_rkv, w0, w1, w2, a0, a1, a2, g1, g2,
                                               k_k, k_a, r_k, v_first, vres)
    if vres is None:
        v_first = v
    y = _rwkv_scan(r, ld, k, v, kkn, b)
    return _rwkv_post(h, y, bonus, g, ln_w, ln_b, w_o, g_post), v_first


def _gla_proj_kernel(h_ref, gn_ref, w_ref, wa_ref, aup_ref, abias_ref,
                     q_ref, k_ref, v_ref, r_ref, la_ref):
    xn = _rms(h_ref[0], gn_ref[...], NORM_EPS).astype(BF16)
    qkvr = _dot(xn, w_ref[...])
    dk = GLA_KEY_DIM
    dv = GLA_VAL_DIM
    q_ref[0] = qkvr[:, :dk] * np.float32((dk // GLA_HEADS) ** -0.5)
    k_ref[0] = qkvr[:, dk:2 * dk]
    v_ref[0] = qkvr[:, 2 * dk:2 * dk + dv]
    r_ref[0] = qkvr[:, 2 * dk + dv:]
    z = _dot(_dot(xn, wa_ref[...]).astype(BF16), aup_ref[...]) + abias_ref[...]
    la_ref[0] = -_softplus(-z) * np.float32(1.0 / GLA_TAU)


def _gla_chunk_kernel(q_ref, k_ref, v_ref, r_ref, la_ref, nw_ref, o_ref, st_ref, *, chunks):
    @pl.when(pl.program_id(1) == 0)
    def _():
        st_ref[...] = jnp.zeros_like(st_ref)

    C = GLA_CHUNK
    dk = GLA_KEY_DIM // GLA_HEADS
    dv = GLA_VAL_DIM // GLA_HEADS
    rowi = lax.broadcasted_iota(jnp.int32, (C, C), 0)
    coli = lax.broadcasted_iota(jnp.int32, (C, C), 1)
    low_incl = rowi >= coli
    tri = low_incl.astype(BF16)
    nw = nw_ref[...]

    for ci in range(chunks):
        rows = pl.ds(ci * C, C)
        bcum = _dot_tri3(tri, la_ref[0, rows, :])
        b_last = bcum[C - 1:C, :]
        k = k_ref[0, rows, :]
        qe = (q_ref[0, rows, :] * jnp.exp(bcum)).astype(BF16)
        ke = (k * jnp.exp(-bcum)).astype(BF16)
        kd = (k * jnp.exp(b_last - bcum)).astype(BF16)
        dec = jnp.exp(b_last)
        vb = v_ref[0, rows, :].astype(BF16)
        r = r_ref[0, rows, :]
        for h in range(GLA_HEADS):
            ks = slice(h * dk, (h + 1) * dk)
            vs = slice(h * dv, (h + 1) * dv)
            a = jnp.where(low_incl, _dot_nt(qe[:, ks], ke[:, ks]), 0.0)
            st = st_ref[h]
            o = _dot(a.astype(BF16), vb[:, vs]) + _dot_nt(qe[:, ks], st.astype(BF16))
            st_ref[h] = st * dec[:, ks] + _dot_tn(vb[:, vs], kd[:, ks])
            o = o * lax.rsqrt(jnp.mean(o * o, axis=-1, keepdims=True) + GLA_NORM_EPS) * nw
            rh = r[:, vs]
            o_ref[0, rows, vs] = (o * (rh * _sigmoid(rh))).astype(o_ref.dtype)


def _gla_layer(h, g_pre, g_post, w_in, a_up, a_bias, norm_w, w_o):
    B, S, D = h.shape
    dk, dv = GLA_KEY_DIM, GLA_VAL_DIM
    tm = min(PROJ_TM, S)
    n_main = 2 * dk + 2 * dv
    w_main = w_in[:, :n_main].astype(BF16)
    w_a = _pad_cols(w_in[:, n_main:], LORA_PAD).astype(BF16)
    aup = _pad_rows(a_up, LORA_PAD).astype(BF16)
    q, k, v, r, la = pl.pallas_call(
        _gla_proj_kernel,
        grid=(B, S // tm),
        in_specs=[_tok_spec(tm, D), _const_spec((1, D)), _const_spec((D, n_main)),
                  _const_spec((D, LORA_PAD)), _const_spec((LORA_PAD, dk)), _const_spec((1, dk))],
        out_specs=[_tok_spec(tm, dk), _tok_spec(tm, dk), _tok_spec(tm, dv), _tok_spec(tm, dv),
                   _tok_spec(tm, dk)],
        out_shape=[jax.ShapeDtypeStruct((B, S, w), F32) for w in (dk, dk, dv, dv, dk)],
        compiler_params=_params(),
        name="gla_proj",
    )(h, _row(g_pre), w_main, w_a, aup, _row(a_bias))

    chunks = 4 if S % (4 * GLA_CHUNK) == 0 else 1
    tc = chunks * GLA_CHUNK
    o = pl.pallas_call(
        functools.partial(_gla_chunk_kernel, chunks=chunks),
        grid=(B, S // tc),
        in_specs=[_tok_spec(tc, dk), _tok_spec(tc, dk), _tok_spec(tc, dv), _tok_spec(tc, dv),
                  _tok_spec(tc, dk), _const_spec((1, dv // GLA_HEADS))],
        out_specs=_tok_spec(tc, dv),
        out_shape=jax.ShapeDtypeStruct((B, S, dv), BF16),
        scratch_shapes=[pltpu.VMEM((GLA_HEADS, dv // GLA_HEADS, dk // GLA_HEADS), F32)],
        compiler_params=_params(),
        name="gla_chunk",
    )(q, k, v, r, la, _row(norm_w))
    return _post(h, o, w_o, g_post)


def _nsa_head_perm():
    perm = np.zeros((D_MODEL,), np.int32)
    for p in range(NSA_HPG):
        for g in range(NSA_KV_GROUPS):
            src = (g * NSA_HPG + p) * NSA_HEAD_DIM
            dst = p * LANES + g * NSA_HEAD_DIM
            perm[dst:dst + NSA_HEAD_DIM] = np.arange(src, src + NSA_HEAD_DIM)
    return perm


def _nsa_proj_kernel(h_ref, gn_ref, wq_ref, wkv_ref, wg_ref, q_ref, kv_ref, gates_ref):
    xn = _rms(h_ref[0], gn_ref[...], NORM_EPS).astype(BF16)
    q_ref[0] = _dot(xn, wq_ref[...]).astype(BF16)
    kv_ref[0] = _dot(xn, wkv_ref[...]).astype(BF16)
    gates_ref[0] = _sigmoid(_dot(xn, wg_ref[...]))


def _nsa_compress_kernel(r_ref, w1_ref, pe_ref, b1_ref, w2_ref, out_ref, *, n_cmp):
    r = r_ref[0, 0].astype(F32)
    rows = r.shape[0]
    pe = pe_ref[0]
    ra = (r + pe[0]).astype(BF16)
    rb = (pltpu.roll(r, rows - 1, 0) + pe[1]).astype(BF16)
    out = jnp.zeros((rows, LANES), F32)
    for g in range(NSA_KV_GROUPS):
        hid = _gelu_tanh(_dot(ra, w1_ref[0, g, 0]) + _dot(rb, w1_ref[0, g, 1]) + b1_ref[0])
        out = out + _dot(hid.astype(BF16), w2_ref[0, g])
    rowi = lax.broadcasted_iota(jnp.int32, out.shape, 0)
    out_ref[0, 0] = jnp.where(rowi < n_cmp, out, 0.0).astype(BF16)


def _group_queries(q, g):
    lane = lax.broadcasted_iota(jnp.int32, (q.shape[0], LANES), 1)
    keep = (lane < NSA_HEAD_DIM) if g == 0 else (lane >= NSA_HEAD_DIM)
    return [jnp.where(keep, q[:, p * LANES:(p + 1) * LANES], jnp.zeros((), q.dtype))
            for p in range(NSA_HPG)]


def _ones_in_other_group(v, g):
    lane = lax.broadcasted_iota(jnp.int32, v.shape, 1)
    other = (lane >= NSA_HEAD_DIM) if g == 0 else (lane < NSA_HEAD_DIM)
    return jnp.where(other, jnp.ones((), v.dtype), v)


def _normalize_groups(acc0, acc1):
    lane = lax.broadcasted_iota(jnp.int32, acc0.shape, 1)
    half = NSA_HEAD_DIM
    return jnp.where(lane < half, acc0 / pltpu.roll(acc0, half, 1), acc1 / pltpu.roll(acc1, half, 1))


def _nsa_cmp_kernel(q_ref, kc_ref, vc_ref, ov_ref, ocmp_ref, score_ref, *, n_cmp, n_slc):
    TQ = q_ref.shape[1]
    ncp = kc_ref.shape[1]
    nbp = ov_ref.shape[1]
    t0 = pl.program_id(1) * TQ
    q = q_ref[0]
    kc = kc_ref[0]
    vc = vc_ref[0]
    pos = t0 + lax.broadcasted_iota(jnp.int32, (TQ, 1), 0)
    blk_c = lax.broadcasted_iota(jnp.int32, (TQ, ncp), 1)
    vis = (blk_c * CMP_STRIDE + (CMP_BLOCK - 1) <= pos) & (blk_c < n_cmp)
    bias = jnp.where(vis, 0.0, NEG_INF)
    any_vis = pos >= CMP_BLOCK - 1
    blk_s = lax.broadcasted_iota(jnp.int32, (TQ, nbp), 1)
    cur = lax.shift_right_logical(pos, 6)
    causal = (blk_s * SLC_BLOCK <= pos) & (blk_s < n_slc)
    forced = (blk_s == 0) | (blk_s == cur) | (blk_s == cur - 1)
    bonus = np.float32(FORCE_BONUS) * forced.astype(F32)
    o_groups = []
    for g in range(NSA_KV_GROUPS):
        qh = _group_queries(q, g)
        psum = jnp.zeros((TQ, ncp), F32)
        outs = []
        s_next = _dot_nt(qh[0], kc)
        for p_i in range(NSA_HPG):
            s = s_next + bias
            if p_i + 1 < NSA_HPG:
                s_next = _dot_nt(qh[p_i + 1], kc)
            e = jnp.exp2(s - jnp.max(s, axis=-1, keepdims=True))
            p = e / jnp.sum(e, axis=-1, keepdims=True)
            p = jnp.where(any_vis, p, 0.0)
            psum = psum + p
            outs.append(_dot(p.astype(BF16), vc))
        o_groups.append(outs)
        imp = _dot_hilo(psum, ov_ref[...])
        score_ref[0, :, g * nbp:(g + 1) * nbp] = jnp.where(causal, imp + bonus, NEG_INF)
    lane = lax.broadcasted_iota(jnp.int32, (TQ, LANES), 1)
    for p_i in range(NSA_HPG):
        ocmp_ref[0, :, p_i * LANES:(p_i + 1) * LANES] = jnp.where(
            lane < NSA_HEAD_DIM, o_groups[0][p_i], o_groups[1][p_i])


def _nsa_topk_kernel(score_ref, sel_ref, *, top_n):
    nbp = score_ref.shape[2] // NSA_KV_GROUPS
    sc0 = jnp.concatenate([score_ref[0, :, g * nbp:(g + 1) * nbp] for g in range(NSA_KV_GROUPS)],
                          axis=0)
    lane = lax.broadcasted_iota(jnp.int32, sc0.shape, 1).astype(F32)

    def body(_, carry):
        sc, sel = carry
        m = jnp.max(sc, axis=-1, keepdims=True)
        idx = jnp.min(jnp.where(sc == m, lane, np.float32(nbp)), axis=-1, keepdims=True)
        hit = lane == idx
        return jnp.where(hit, np.float32(-3e38), sc), jnp.where(hit, 1.0, sel)

    _, sel = lax.fori_loop(0, top_n, body, (sc0, jnp.zeros_like(sc0)))
    rows = score_ref.shape[1]
    for g in range(NSA_KV_GROUPS):
        sel_ref[0, :, g * nbp:(g + 1) * nbp] = sel[g * rows:(g + 1) * rows].astype(sel_ref.dtype)


def _nsa_attn_kernel(q_ref, ks_ref, vs_ref, kw_ref, vw_ref, sel_ref, ocmp_ref, gates_ref, ge_ref,
                     o_ref, m_ref, acc_ref, *, win_rows):
    TQ = q_ref.shape[1]
    S = ks_ref.shape[1]
    TK = min(NSA_TK, S)
    nbp = sel_ref.shape[2] // NSA_KV_GROUPS
    R = NSA_HPG * TQ
    CW = 2 * TQ
    n_cw = R // CW
    t0 = pl.program_id(1) * TQ
    q = q_ref[0]
    pos = t0 + lax.broadcasted_iota(jnp.int32, (1, TQ), 1)
    n_kt = (t0 + TQ + TK - 1) // TK
    blk_l = lax.broadcasted_iota(jnp.int32, (TK, nbp), 1)
    key_s = lax.broadcasted_iota(jnp.int32, (TK, 1), 0)
    w_start = pl.multiple_of(jnp.maximum(t0 - WINDOW, 0), TQ)
    wkey = w_start + lax.broadcasted_iota(jnp.int32, (win_rows, 1), 0)
    wbias = jnp.where((wkey <= pos) & (wkey > pos - WINDOW), 0.0, NEG_INF)
    wbias = jnp.concatenate([wbias] * (CW // TQ), axis=1)

    def scores(k_tile, qall, bias):
        return [_dot_nt(k_tile, qall[c * CW:(c + 1) * CW]) + bias for c in range(n_cw)]

    o_t = {"slc": [], "win": []}
    for g in range(NSA_KV_GROUPS):
        qall = jnp.concatenate(_group_queries(q, g), axis=0)
        selg = sel_ref[0, :, g * nbp:(g + 1) * nbp]

        m_ref[...] = jnp.full(m_ref.shape, NEG_INF, F32)
        acc_ref[...] = jnp.zeros_like(acc_ref)

        def body(kt, carry):
            k0 = pl.multiple_of(kt * TK, TK)
            expand = (blk_l == lax.shift_right_logical(k0 + key_s, 6)).astype(BF16)
            chosen = _dot_nt(expand, selg)
            bias = jnp.where((chosen > 0.5) & (k0 + key_s <= pos), 0.0, NEG_INF)
            bias = jnp.concatenate([bias] * (CW // TQ), axis=1)
            v_tile = _ones_in_other_group(vs_ref[0, pl.ds(k0, TK), :], g)
            s = scores(ks_ref[0, pl.ds(k0, TK), :], qall, bias)
            m_old = [m_ref[:, c * CW:(c + 1) * CW] for c in range(n_cw)]
            m_new = [jnp.maximum(m_old[c], jnp.max(s[c], axis=0, keepdims=True))
                     for c in range(n_cw)]
            p = [jnp.exp2(s[c] - m_new[c]).astype(BF16) for c in range(n_cw)]
            pv = [_dot_tn(v_tile, p[c]) for c in range(n_cw)]
            for c in range(n_cw):
                cols = slice(c * CW, (c + 1) * CW)
                acc_ref[:, cols] = acc_ref[:, cols] * jnp.exp2(m_old[c] - m_new[c]) + pv[c]
                m_ref[:, cols] = m_new[c]
            return carry

        lax.fori_loop(0, n_kt, body, 0)
        o_t["slc"].append(acc_ref[...])

        vw = _ones_in_other_group(vw_ref[0, pl.ds(w_start, win_rows), :], g)
        s = scores(kw_ref[0, pl.ds(w_start, win_rows), :], qall, wbias)
        p = [jnp.exp2(s[c] - jnp.max(s[c], axis=0, keepdims=True)).astype(BF16)
             for c in range(n_cw)]
        o_t["win"].append(jnp.concatenate([_dot_tn(vw, p[c]) for c in range(n_cw)], axis=1))

    half = NSA_HEAD_DIM
    normed = {}
    for name, (a0, a1) in o_t.items():
        normed[name] = jnp.concatenate([a0[:half] / a0[half:half + 1], a1[half:] / a1[0:1]], axis=0)
    gates = gates_ref[0]
    g_cmp = _dot_hilo(gates, ge_ref[0])
    g_slc = _dot_hilo(gates, ge_ref[1])
    g_win = _dot_hilo(gates, ge_ref[2])
    for p_i in range(NSA_HPG):
        cols = slice(p_i * LANES, (p_i + 1) * LANES)
        slc = normed["slc"][:, p_i * TQ:(p_i + 1) * TQ].T
        win = normed["win"][:, p_i * TQ:(p_i + 1) * TQ].T
        o_ref[0, :, cols] = (g_cmp[:, cols] * ocmp_ref[0, :, cols] + g_slc[:, cols] * slc
                             + g_win[:, cols] * win).astype(o_ref.dtype)


def _nsa_layer(h, g_pre, g_post, w_in, cmp_w1, cmp_b1, cmp_w2, cmp_pe, w_o):
    B, S, D = h.shape
    H, G, dh, hpg = NSA_HEADS, NSA_KV_GROUPS, NSA_HEAD_DIM, NSA_HPG
    perm = _nsa_head_perm()
    n_q = H * dh
    n_kv = 6 * G * dh
    wq = (w_in[:, :n_q] * np.float32(dh ** -0.5 * np.log2(np.e)))[:, perm].astype(BF16)
    wkv = w_in[:, n_q:n_q + n_kv].astype(BF16)
    wgt = _pad_cols(w_in[:, n_q + n_kv:], LANES).astype(BF16)
    tm = min(PROJ_TM, S)
    q, kv, gates = pl.pallas_call(
        _nsa_proj_kernel,
        grid=(B, S // tm),
        in_specs=[_tok_spec(tm, D), _const_spec((1, D)), _const_spec((D, n_q)),
                  _const_spec((D, n_kv)), _const_spec((D, LANES))],
        out_specs=[_tok_spec(tm, n_q), _tok_spec(tm, n_kv), _tok_spec(tm, LANES)],
        out_shape=[jax.ShapeDtypeStruct((B, S, n_q), BF16), jax.ShapeDtypeStruct((B, S, n_kv), BF16),
                   jax.ShapeDtypeStruct((B, S, LANES), F32)],
        compiler_params=_params(),
        name="nsa_proj",
    )(h, _row(g_pre), wq, wkv, wgt)

    n_cmp = (S - CMP_BLOCK) // CMP_STRIDE + 1
    ncp = S // CMP_STRIDE
    half = CMP_BLOCK // 2
    assert CMP_STRIDE == half
    rk = jnp.stack([kv[:, :, :LANES], kv[:, :, LANES:2 * LANES]]).reshape(2, B, ncp, half * LANES)
    w1 = cmp_w1.reshape(2, 2, half, dh, CMP_HIDDEN)
    w1x = jnp.zeros((2, G, 2, half, G, dh, CMP_HIDDEN), F32)
    for g in range(G):
        w1x = w1x.at[:, g, :, :, g].set(w1)
    w1x = w1x.reshape(2, G, 2, half * LANES, CMP_HIDDEN).astype(BF16)
    pe = jnp.broadcast_to(cmp_pe.reshape(2, 2, half, 1, dh), (2, 2, half, G, dh))
    pe = pe.reshape(2, 2, 1, half * LANES).astype(F32)
    w2x = jnp.zeros((2, G, CMP_HIDDEN, G, dh), F32)
    for g in range(G):
        w2x = w2x.at[:, g, :, g].set(cmp_w2)
    w2x = w2x.reshape(2, G, CMP_HIDDEN, LANES).astype(BF16)
    kvc = pl.pallas_call(
        functools.partial(_nsa_compress_kernel, n_cmp=n_cmp),
        grid=(2, B),
        in_specs=[pl.BlockSpec((1, 1, ncp, half * LANES), lambda c, b: (c, b, 0, 0)),
                  pl.BlockSpec((1, G, 2, half * LANES, CMP_HIDDEN), lambda c, b: (c, 0, 0, 0, 0)),
                  pl.BlockSpec((1, 2, 1, half * LANES), lambda c, b: (c, 0, 0, 0)),
                  pl.BlockSpec((1, 1, CMP_HIDDEN), lambda c, b: (c, 0, 0)),
                  pl.BlockSpec((1, G, CMP_HIDDEN, LANES), lambda c, b: (c, 0, 0, 0))],
        out_specs=pl.BlockSpec((1, 1, ncp, LANES), lambda c, b: (c, b, 0, 0)),
        out_shape=jax.ShapeDtypeStruct((2, B, ncp, LANES), BF16),
        compiler_params=_params(),
        name="nsa_compress",
    )(rk, w1x, pe, cmp_b1.reshape(2, 1, CMP_HIDDEN).astype(F32), w2x)

    n_slc = S // SLC_BLOCK
    top_n = min(SLC_TOPK, n_slc)
    nbp = max(LANES, n_slc)
    start = np.arange(ncp) * CMP_STRIDE
    end = start + CMP_BLOCK - 1
    s_start = np.arange(nbp) * SLC_BLOCK
    s_end = s_start + SLC_BLOCK - 1
    overlap = ((end[:, None] >= s_start[None, :]) & (start[:, None] <= s_end[None, :])
               & (np.arange(ncp)[:, None] < n_cmp) & (np.arange(nbp)[None, :] < n_slc))
    overlap = jnp.asarray(overlap.astype(np.float32), BF16)
    TQ = min(NSA_TQ, S)
    o_cmp, score = pl.pallas_call(
        functools.partial(_nsa_cmp_kernel, n_cmp=n_cmp, n_slc=n_slc),
        grid=(B, S // TQ),
        in_specs=[_tok_spec(TQ, n_q),
                  pl.BlockSpec((None, 1, ncp, LANES), lambda b, i: (0, b, 0, 0)),
                  pl.BlockSpec((None, 1, ncp, LANES), lambda b, i: (1, b, 0, 0)),
                  _const_spec((ncp, nbp))],
        out_specs=[_tok_spec(TQ, n_q), _tok_spec(TQ, G * nbp)],
        out_shape=[jax.ShapeDtypeStruct((B, S, n_q), F32),
                   jax.ShapeDtypeStruct((B, S, G * nbp), F32)],
        compiler_params=_params(),
        name="nsa_compressed_attn",
    )(q, kvc, kvc, overlap)
    tr = min(NSA_TOPK_ROWS, S)
    sel = pl.pallas_call(
        functools.partial(_nsa_topk_kernel, top_n=top_n),
        grid=(B, S // tr),
        in_specs=[_tok_spec(tr, G * nbp)],
        out_specs=_tok_spec(tr, G * nbp),
        out_shape=jax.ShapeDtypeStruct((B, S, G * nbp), BF16),
        compiler_params=_params(),
        name="nsa_topk",
    )(score)

    ge = np.zeros((3, LANES, n_q), np.float32)
    for g in range(G):
        for p in range(hpg):
            for j in range(3):
                ge[j, g * hpg * 3 + p * 3 + j, p * LANES + g * dh:p * LANES + (g + 1) * dh] = 1.0
    ge = jnp.asarray(ge, BF16)
    win_rows = min(WINDOW + TQ, S)
    kv_spec = lambda c: pl.BlockSpec((1, S, LANES), lambda b, i, c=c: (b, 0, c))
    o = pl.pallas_call(
        functools.partial(_nsa_attn_kernel, win_rows=win_rows),
        grid=(B, S // TQ),
        in_specs=[_tok_spec(TQ, n_q), kv_spec(2), kv_spec(3), kv_spec(4), kv_spec(5),
                  _tok_spec(TQ, G * nbp), _tok_spec(TQ, n_q), _tok_spec(TQ, LANES),
                  _const_spec((3, LANES, n_q))],
        out_specs=_tok_spec(TQ, n_q),
        out_shape=jax.ShapeDtypeStruct((B, S, n_q), BF16),
        scratch_shapes=[pltpu.VMEM((1, hpg * TQ), F32), pltpu.VMEM((LANES, hpg * TQ), F32)],
        compiler_params=_params(),
        name="nsa_selected_window_attn",
    )(q, kv, kv, kv, kv, sel, o_cmp, gates, ge)
    return _post(h, o, w_o[perm, :], g_post)


def kernel(x, norm_g, ffn_w_in, ffn_conv_w, ffn_conv_b, ffn_w_out, rwkv_mu, rwkv_w_rkv, rwkv_w0, rwkv_w1, rwkv_w2, rwkv_a0, rwkv_a1, rwkv_a2, rwkv_g1, rwkv_g2, rwkv_k_k, rwkv_k_a, rwkv_r_k, rwkv_ln_w, rwkv_ln_b, rwkv_w_o, rwkv_v0, rwkv_v1, rwkv_v2, nsa_w_in, nsa_cmp_w1, nsa_cmp_b1, nsa_cmp_w2, nsa_cmp_pe, nsa_w_o, gla_w_in, gla_a_up, gla_a_bias, gla_norm_w, gla_w_o):
    depth = norm_g.shape[0]
    h = x
    v_first = None
    for i in range(depth):
        kind, j = i % 3, i // 3
        if kind == 0:
            vres = None if j == 0 else (rwkv_v0[j - 1], rwkv_v1[j - 1], rwkv_v2[j - 1])
            h, v_first = _rwkv_layer(
                h, norm_g[i, 0], norm_g[i, 1], rwkv_mu[j], rwkv_w_rkv[j], rwkv_w0[j], rwkv_w1[j],
                rwkv_w2[j], rwkv_a0[j], rwkv_a1[j], rwkv_a2[j], rwkv_g1[j], rwkv_g2[j], rwkv_k_k[j],
                rwkv_k_a[j], rwkv_r_k[j], rwkv_ln_w[j], rwkv_ln_b[j], rwkv_w_o[j], v_first, vres)
        elif kind == 1:
            h = _nsa_layer(h, norm_g[i, 0], norm_g[i, 1], nsa_w_in[j], nsa_cmp_w1[j], nsa_cmp_b1[j],
                           nsa_cmp_w2[j], nsa_cmp_pe[j], nsa_w_o[j])
        else:
            h = _gla_layer(h, norm_g[i, 0], norm_g[i, 1], gla_w_in[j], gla_a_up[j], gla_a_bias[j],
                           gla_norm_w[j], gla_w_o[j])
        h = _ffn(h, norm_g[i, 2], ffn_w_in[i], ffn_conv_w[i], ffn_conv_b[i], ffn_w_out[i],
                 norm_g[i, 3])
    return h
```

```python
import functools

import numpy as np
import jax
import jax.numpy as jnp
from jax import lax
from jax.experimental import pallas as pl
from jax.experimental.pallas import tpu as pltpu

F32 = jnp.float32
BF16 = jnp.bfloat16

V7X_VMEM_BYTES = 64 * 1024 * 1024
VMEM_LIMIT_BYTES = V7X_VMEM_BYTES - 8 * 1024 * 1024
LANES = 128
SUBLANES = 8

D_MODEL = 1024
NORM_EPS = 1e-6
NEG_INF = -1e30

RWKV_HEAD_DIM = 64
RWKV_HEADS = D_MODEL // RWKV_HEAD_DIM
RWKV_GN_EPS = 64e-5
RWKV_CHUNK = 64
LORA_PAD = 128

NSA_HEADS = 16
NSA_KV_GROUPS = 2
NSA_HEAD_DIM = D_MODEL // NSA_HEADS
NSA_HPG = NSA_HEADS // NSA_KV_GROUPS
CMP_BLOCK = 32
CMP_STRIDE = 16
CMP_HIDDEN = 256
SLC_BLOCK = 64
SLC_TOPK = 16
WINDOW = 512
FORCE_BONUS = 1e4
NSA_TQ = 128
NSA_TK = 512
NSA_TOPK_TOKENS = 512

GLA_HEADS = 4
GLA_KEY_DIM = D_MODEL // 2
GLA_VAL_DIM = D_MODEL
GLA_GATE_LORA = 16
GLA_TAU = 16.0
GLA_CHUNK = 64
GLA_NORM_EPS = 1e-5

D_FF = 2816
FFN_CHUNK = 256
FFN_TM = 512
PROJ_TM = 256
POST_TM = 512


def _dot(a, b):
    return jnp.dot(a, b, preferred_element_type=F32)


def _dot_nt(a, b):
    return lax.dot_general(a, b, (((1,), (1,)), ((), ())), preferred_element_type=F32)


def _dot_tn(a, b):
    return lax.dot_general(a, b, (((0,), (0,)), ((), ())), preferred_element_type=F32)


def _split2(x):
    hi = x.astype(BF16)
    lo = (x - hi.astype(F32)).astype(BF16)
    return hi, lo


def _dot_hilo(x, m):
    hi, lo = _split2(x)
    return _dot(hi, m) + _dot(lo, m)


def _dot_tri3(tri, x):
    hi = x.astype(BF16)
    r1 = x - hi.astype(F32)
    mid = r1.astype(BF16)
    lo = (r1 - mid.astype(F32)).astype(BF16)
    return _dot(tri, hi) + _dot(tri, mid) + _dot(tri, lo)


def _head_sum(x, red_ref, exp_ref):
    return _dot_hilo(_dot_hilo(x, red_ref[...]), exp_ref[...])


def _rms(x, g, eps):
    return x * lax.rsqrt(jnp.mean(x * x, axis=-1, keepdims=True) + eps) * g


def _sigmoid(x):
    return 1.0 / (1.0 + jnp.exp(-x))


def _softplus(x):
    return jnp.maximum(x, 0.0) + jnp.log(1.0 + jnp.exp(-jnp.abs(x)))


def _gelu_tanh(x):
    c = np.float32(np.sqrt(2.0 / np.pi))
    return 0.5 * x * (1.0 + jnp.tanh(c * (x + 0.044715 * (x * x * x))))


def _shift_rows(x, n, prev8):
    rolled = pltpu.roll(x, n, 0)
    row = lax.broadcasted_iota(jnp.int32, x.shape, 0)
    for j in range(n):
        rolled = jnp.where(row == j, prev8[SUBLANES - n + j:SUBLANES - n + j + 1, :], rolled)
    return rolled


def _const_spec(shape):
    nd = len(shape)
    return pl.BlockSpec(shape, lambda *_: (0,) * nd, pipeline_mode=pl.Buffered(1))


def _tok_spec(tm, width):
    return pl.BlockSpec((1, tm, width), lambda b, i: (b, i, 0))


def _params():
    return pltpu.CompilerParams(dimension_semantics=("arbitrary", "arbitrary"),
                                vmem_limit_bytes=VMEM_LIMIT_BYTES)


def _row(v):
    return v.reshape(1, -1).astype(F32)


def _ffn_kernel(h_ref, gpre_ref, wg_ref, wu_ref, cw_ref, cb_ref, wo_ref, gpost_ref, out_ref,
                xn_ref, acc_ref, halo_ref, *, nc):
    @pl.when(pl.program_id(1) == 0)
    def _():
        halo_ref[...] = jnp.zeros_like(halo_ref)

    h = h_ref[0]
    xn_ref[...] = _rms(h, gpre_ref[...], NORM_EPS).astype(BF16)
    acc_ref[...] = jnp.zeros_like(acc_ref)
    tm = h.shape[0]

    def gate_up(c):
        xn = xn_ref[...]
        return _dot(xn, wg_ref[c]), _dot(xn, wu_ref[c])

    nxt = gate_up(0)
    for c in range(nc):
        gate, up = nxt
        if c + 1 < nc:
            nxt = gate_up(c + 1)
        prev = halo_ref[c]
        g1 = _shift_rows(gate, 1, prev)
        g2 = _shift_rows(gate, 2, prev)
        cw = cw_ref[c]
        z = cw[2:3] * gate + cw[1:2] * g1 + cw[0:1] * g2 + cb_ref[c]
        halo_ref[c] = gate[tm - SUBLANES:tm, :]
        act = (z * _sigmoid(z) * up).astype(BF16)
        acc_ref[...] += _dot(act, wo_ref[c])
    out_ref[0] = h + _rms(acc_ref[...], gpost_ref[...], NORM_EPS)


def _ffn(h, g_pre, w_in, conv_w, conv_b, w_out, g_post):
    B, S, D = h.shape
    F = w_out.shape[0]
    fc = FFN_CHUNK
    nc = F // fc
    tm = min(FFN_TM, S)
    wg = w_in[:, :F].reshape(D, nc, fc).transpose(1, 0, 2).astype(BF16)
    wu = w_in[:, F:].reshape(D, nc, fc).transpose(1, 0, 2).astype(BF16)
    cw = conv_w.reshape(3, nc, fc).transpose(1, 0, 2)
    cb = conv_b.reshape(nc, 1, fc)
    wo = w_out.reshape(nc, fc, D).astype(BF16)
    return pl.pallas_call(
        functools.partial(_ffn_kernel, nc=nc),
        grid=(B, S // tm),
        in_specs=[_tok_spec(tm, D), _const_spec((1, D)), _const_spec((nc, D, fc)),
                  _const_spec((nc, D, fc)), _const_spec((nc, 3, fc)), _const_spec((nc, 1, fc)),
                  _const_spec((nc, fc, D)), _const_spec((1, D))],
        out_specs=_tok_spec(tm, D),
        out_shape=jax.ShapeDtypeStruct((B, S, D), F32),
        scratch_shapes=[pltpu.VMEM((tm, D), BF16), pltpu.VMEM((tm, D), F32),
                        pltpu.VMEM((nc, SUBLANES, fc), F32)],
        compiler_params=_params(),
        name="conv_ffn",
    )(h, _row(g_pre), wg, wu, cw, cb, wo, _row(g_post))


def _post_kernel(h_ref, o_ref, wo_ref, g_ref, out_ref):
    y = _dot(o_ref[0].astype(BF16), wo_ref[...])
    out_ref[0] = h_ref[0] + _rms(y, g_ref[...], NORM_EPS)


def _post(h, o, w_o, g_post):
    B, S, D = h.shape
    K = o.shape[-1]
    tm = min(POST_TM, S)
    return pl.pallas_call(
        _post_kernel,
        grid=(B, S // tm),
        in_specs=[_tok_spec(tm, D), _tok_spec(tm, K), _const_spec((K, D)), _const_spec((1, D))],
        out_specs=_tok_spec(tm, D),
        out_shape=jax.ShapeDtypeStruct((B, S, D), F32),
        compiler_params=_params(),
        name="mixer_out",
    )(h, o, w_o.astype(BF16), _row(g_post))


def _rwkv_proj_kernel(*refs, has_vres):
    if has_vres:
        (h_ref, gn_ref, mu_ref, wr_ref, wk_ref, wv_ref, w1_ref, w2_ref, a1_ref, a2_ref, g1_ref,
         g2_ref, vec_ref, red_ref, exp_ref, vf_ref, v1_ref, v2_ref,
         r_ref, ld_ref, k_ref, v_ref, kkn_ref, b_ref, g_ref, bonus_ref, carry_ref) = refs
    else:
        (h_ref, gn_ref, mu_ref, wr_ref, wk_ref, wv_ref, w1_ref, w2_ref, a1_ref, a2_ref, g1_ref,
         g2_ref, vec_ref, red_ref, exp_ref,
         r_ref, ld_ref, k_ref, v_ref, kkn_ref, b_ref, g_ref, bonus_ref, carry_ref) = refs

    @pl.when(pl.program_id(1) == 0)
    def _():
        carry_ref[...] = jnp.zeros_like(carry_ref)

    x = _rms(h_ref[0], gn_ref[...], NORM_EPS)
    tm = x.shape[0]
    xx = _shift_rows(x, 1, carry_ref[...]) - x
    carry_ref[...] = x[tm - SUBLANES:tm, :]
    mu = mu_ref[...]
    xr, xw, xk, xv, xa, xg = ((x + xx * mu[c:c + 1]).astype(BF16) for c in range(6))
    vec = vec_ref[...]
    w0, a0, k_k, k_a, r_k, v0 = (vec[c:c + 1] for c in range(6))

    r = _dot(xr, wr_ref[...])
    k = _dot(xk, wk_ref[...])
    v = _dot(xv, wv_ref[...])
    wl = -_softplus(-(w0 + _dot(jnp.tanh(_dot(xw, w1_ref[...])).astype(BF16), w2_ref[...]))) - 0.5
    ld_ref[0] = -jnp.exp(wl)
    a = _sigmoid(a0 + _dot(_dot(xa, a1_ref[...]).astype(BF16), a2_ref[...]))
    g_ref[0] = _dot(_sigmoid(_dot(xg, g1_ref[...])).astype(BF16), g2_ref[...])
    if has_vres:
        mix = _sigmoid(v0 + _dot(_dot(xv, v1_ref[...]).astype(BF16), v2_ref[...]))
        v = v + (vf_ref[0] - v) * mix
    kk = k * k_k
    norm = jnp.sqrt(_head_sum(kk * kk, red_ref, exp_ref))
    kkn = kk / jnp.maximum(norm, 1e-12)
    k = k * (1.0 + (a - 1.0) * k_a)
    r_ref[0] = r
    k_ref[0] = k
    v_ref[0] = v
    kkn_ref[0] = kkn
    b_ref[0] = kkn * a
    bonus_ref[0] = _head_sum(r * k * r_k, red_ref, exp_ref) * v


def _pad_cols(w, n):
    return jnp.pad(w, ((0, 0), (0, n - w.shape[1])))


def _pad_rows(w, n):
    return jnp.pad(w, ((0, n - w.shape[0]), (0, 0)))


def _head_reduce_expand(d, hd, value):
    red = (np.arange(d)[:, None] // hd == np.arange(LANES)[None, :]).astype(np.float32)
    return jnp.asarray(red, BF16), jnp.asarray(red.T * value, BF16)


def _rwkv_proj(h, g_norm, mu, w_rkv, w0, w1, w2, a0, a1, a2, g1, g2, k_k, k_a, r_k, v_first, vres):
    B, S, D = h.shape
    tm = min(PROJ_TM, S)
    has_vres = vres is not None
    v0 = vres[0] if has_vres else jnp.zeros((D,), F32)
    vec = jnp.stack([w0, a0, k_k, k_a, r_k.reshape(-1), v0]).astype(F32)
    red, exp = _head_reduce_expand(D, RWKV_HEAD_DIM, 1.0)
    lora_in = lambda w: _pad_cols(w, LORA_PAD).astype(BF16)
    lora_out = lambda w: _pad_rows(w, LORA_PAD).astype(BF16)
    args = [h, _row(g_norm), mu, w_rkv[0].astype(BF16), w_rkv[1].astype(BF16), w_rkv[2].astype(BF16),
            lora_in(w1), lora_out(w2), lora_in(a1), lora_out(a2), lora_in(g1), lora_out(g2), vec, red, exp]
    specs = [_tok_spec(tm, D), _const_spec((1, D)), _const_spec((6, D)), _const_spec((D, D)),
             _const_spec((D, D)), _const_spec((D, D)), _const_spec((D, LORA_PAD)),
             _const_spec((LORA_PAD, D)), _const_spec((D, LORA_PAD)), _const_spec((LORA_PAD, D)),
             _const_spec((D, LORA_PAD)), _const_spec((LORA_PAD, D)), _const_spec((6, D)),
             _const_spec((D, LANES)), _const_spec((LANES, D))]
    if has_vres:
        args += [v_first, lora_in(vres[1]), lora_out(vres[2])]
        specs += [_tok_spec(tm, D), _const_spec((D, LORA_PAD)), _const_spec((LORA_PAD, D))]
    outs = pl.pallas_call(
        functools.partial(_rwkv_proj_kernel, has_vres=has_vres),
        grid=(B, S // tm),
        in_specs=specs,
        out_specs=[_tok_spec(tm, D)] * 8,
        out_shape=[jax.ShapeDtypeStruct((B, S, D), F32)] * 8,
        scratch_shapes=[pltpu.VMEM((SUBLANES, D), F32)],
        compiler_params=_params(),
        name="rwkv_proj",
    )(*args)
    return outs


def _rwkv_scan_kernel(r_ref, ld_ref, k_ref, v_ref, kkn_ref, b_ref, y_ref, st_ref):
    @pl.when(pl.program_id(1) == 0)
    def _():
        st_ref[...] = jnp.zeros_like(st_ref)

    C = RWKV_CHUNK
    N = RWKV_HEAD_DIM
    assert 2 * N == LANES and C == N
    rowi = lax.broadcasted_iota(jnp.int32, (C, C), 0)
    coli = lax.broadcasted_iota(jnp.int32, (C, C), 1)
    tri = (rowi >= coli).astype(BF16)
    prow = lax.broadcasted_iota(jnp.int32, (C, LANES), 0)
    pcol = lax.broadcasted_iota(jnp.int32, (C, LANES), 1) & (N - 1)
    low_incl = prow >= pcol
    low_strict = prow > pcol
    eye = (prow == pcol).astype(F32)
    srow = lax.broadcasted_iota(jnp.int32, (LANES, LANES), 0)
    scol = lax.broadcasted_iota(jnp.int32, (LANES, LANES), 1)
    same_head = (srow < N) == (scol < N)

    def bdiag(x):
        lane = lax.broadcasted_iota(jnp.int32, x.shape, 1)
        zero = jnp.zeros((), x.dtype)
        return jnp.concatenate([jnp.where(lane < N, x, zero), jnp.where(lane >= N, x, zero)], axis=0)

    ld = ld_ref[0]
    cum = _dot_tri3(tri, ld)
    cum_last = cum[C - 1:C, :]
    p_inv = jnp.exp(-cum)
    p_rem = jnp.exp(cum_last - cum)
    p_all = jnp.exp(cum_last)
    kkn = kkn_ref[0]
    b = b_ref[0]
    k = k_ref[0]
    a_t = (-kkn * jnp.exp(cum - ld)).astype(BF16)
    r_t = (r_ref[0] * jnp.exp(cum)).astype(BF16)
    b_t = (b * p_inv).astype(BF16)
    k_t = (k * p_inv).astype(BF16)
    b_h = (b * p_rem).astype(BF16)
    k_h = (k * p_rem).astype(BF16)
    vb = v_ref[0].astype(BF16)

    pairs = range(RWKV_HEADS // 2)
    sls = [slice(p * LANES, (p + 1) * LANES) for p in pairs]
    ar = [jnp.concatenate([a_t[:, sl], r_t[:, sl]], axis=0) for sl in sls]
    s_b = [_dot_nt(ar[p], bdiag(b_t[:, sls[p]])) for p in pairs]
    s_k = [_dot_nt(ar[p], bdiag(k_t[:, sls[p]])) for p in pairs]
    a_ab = [jnp.where(low_strict, s_b[p][:C], 0.0) for p in pairs]
    a_rb = [jnp.where(low_incl, s_b[p][C:], 0.0).astype(BF16) for p in pairs]
    a_ak = [jnp.where(low_strict, s_k[p][:C], 0.0).astype(BF16) for p in pairs]
    a_rk = [jnp.where(low_incl, s_k[p][C:], 0.0).astype(BF16) for p in pairs]
    v_bd = [bdiag(vb[:, sl]) for sl in sls]
    akv = [_dot(a_ak[p], v_bd[p]).astype(BF16) for p in pairs]
    pw = [a_ab[p].astype(BF16) for p in pairs]
    tinv = [eye + a_ab[p] for p in pairs]
    span = 2
    while span < C:
        pw = [_dot(pw[p], bdiag(pw[p])).astype(BF16) for p in pairs]
        tinv = [tinv[p] + _dot(pw[p], bdiag(tinv[p].astype(BF16))) for p in pairs]
        span *= 2
    tb = [tinv[p].astype(BF16) for p in pairs]
    a_hat = [_dot(tb[p], bdiag(a_t[:, sls[p]])).astype(BF16) for p in pairs]
    u_loc = [_dot(tb[p], bdiag(akv[p])) for p in pairs]
    st = [st_ref[p] for p in pairs]
    stb = [st[p].astype(BF16) for p in pairs]
    ub = [(_dot_nt(a_hat[p], stb[p]) + u_loc[p]).astype(BF16) for p in pairs]
    for p in pairs:
        y_ref[0, :, sls[p]] = (_dot_nt(r_t[:, sls[p]], stb[p]) + _dot(a_rb[p], bdiag(ub[p]))
                               + _dot(a_rk[p], v_bd[p]))
    for p in pairs:
        upd = _dot_tn(jnp.concatenate([ub[p], vb[:, sls[p]]], axis=0),
                      jnp.concatenate([b_h[:, sls[p]], k_h[:, sls[p]]], axis=0))
        st_ref[p] = st[p] * p_all[:, sls[p]] + jnp.where(same_head, upd, 0.0)


def _rwkv_scan(r, ld, k, v, kkn, b):
    B, S, D = r.shape
    C = RWKV_CHUNK
    return pl.pallas_call(
        _rwkv_scan_kernel,
        grid=(B, S // C),
        in_specs=[_tok_spec(C, D)] * 6,
        out_specs=_tok_spec(C, D),
        out_shape=jax.ShapeDtypeStruct((B, S, D), F32),
        scratch_shapes=[pltpu.VMEM((RWKV_HEADS // 2, LANES, LANES), F32)],
        compiler_params=_params(),
        name="rwkv_scan",
    )(r, ld, k, v, kkn, b)


def _rwkv_post_kernel(h_ref, y_ref, bonus_ref, g_ref, vec_ref, red_ref, avg_ref, wo_ref, gpost_ref,
                      out_ref):
    y = y_ref[0]
    d = y - _head_sum(y, red_ref, avg_ref)
    var = _head_sum(d * d, red_ref, avg_ref)
    vec = vec_ref[...]
    yn = d * lax.rsqrt(var + RWKV_GN_EPS) * vec[0:1] + vec[1:2]
    o = ((yn + bonus_ref[0]) * g_ref[0]).astype(BF16)
    out_ref[0] = h_ref[0] + _rms(_dot(o, wo_ref[...]), gpost_ref[...], NORM_EPS)


def _rwkv_post(h, y, bonus, g, ln_w, ln_b, w_o, g_post):
    B, S, D = h.shape
    tm = min(PROJ_TM, S)
    vec = jnp.stack([ln_w, ln_b]).astype(F32)
    red, avg = _head_reduce_expand(D, RWKV_HEAD_DIM, 1.0 / RWKV_HEAD_DIM)
    return pl.pallas_call(
        _rwkv_post_kernel,
        grid=(B, S // tm),
        in_specs=[_tok_spec(tm, D)] * 4 + [_const_spec((2, D)), _const_spec((D, LANES)),
                                           _const_spec((LANES, D)), _const_spec((D, D)),
                                           _const_spec((1, D))],
        out_specs=_tok_spec(tm, D),
        out_shape=jax.ShapeDtypeStruct((B, S, D), F32),
        compiler_params=_params(),
        name="rwkv_out",
    )(h, y, bonus, g, vec, red, avg, w_o.astype(BF16), _row(g_post))


def _rwkv_layer(h, g_pre, g_post, mu, w_rkv, w0, w1, w2, a0, a1, a2, g1, g2, k_k, k_a, r_k,
                ln_w, ln_b, w_o, v_first, vres):
    r, ld, k, v, kkn, b, g, bonus = _rwkv_proj(h, g_pre, mu, w_rkv, w0, w1, w2, a0, a1, a2, g1, g2,
                                               k_k, k_a, r_k, v_first, vres)
    if vres is None:
        v_first = v
    y = _rwkv_scan(r, ld, k, v, kkn, b)
    return _rwkv_post(h, y, bonus, g, ln_w, ln_b, w_o, g_post), v_first


def _gla_proj_kernel(h_ref, gn_ref, w_ref, wa_ref, aup_ref, abias_ref,
                     q_ref, k_ref, v_ref, r_ref, la_ref):
    xn = _rms(h_ref[0], gn_ref[...], NORM_EPS).astype(BF16)
    qkvr = _dot(xn, w_ref[...])
    dk = GLA_KEY_DIM
    dv = GLA_VAL_DIM
    q_ref[0] = qkvr[:, :dk] * np.float32((dk // GLA_HEADS) ** -0.5)
    k_ref[0] = qkvr[:, dk:2 * dk]
    v_ref[0] = qkvr[:, 2 * dk:2 * dk + dv]
    r_ref[0] = qkvr[:, 2 * dk + dv:]
    z = _dot(_dot(xn, wa_ref[...]).astype(BF16), aup_ref[...]) + abias_ref[...]
    la_ref[0] = -_softplus(-z) * np.float32(1.0 / GLA_TAU)


def _gla_chunk_kernel(q_ref, k_ref, v_ref, r_ref, la_ref, nw_ref, o_ref, st_ref, *, chunks):
    @pl.when(pl.program_id(1) == 0)
    def _():
        st_ref[...] = jnp.zeros_like(st_ref)

    C = GLA_CHUNK
    dk = GLA_KEY_DIM // GLA_HEADS
    dv = GLA_VAL_DIM // GLA_HEADS
    rowi = lax.broadcasted_iota(jnp.int32, (C, C), 0)
    coli = lax.broadcasted_iota(jnp.int32, (C, C), 1)
    low_incl = rowi >= coli
    tri = low_incl.astype(BF16)
    nw = nw_ref[...]

    for ci in range(chunks):
        rows = pl.ds(ci * C, C)
        bcum = _dot_tri3(tri, la_ref[0, rows, :])
        b_last = bcum[C - 1:C, :]
        k = k_ref[0, rows, :]
        qe = (q_ref[0, rows, :] * jnp.exp(bcum)).astype(BF16)
        ke = (k * jnp.exp(-bcum)).astype(BF16)
        kd = (k * jnp.exp(b_last - bcum)).astype(BF16)
        dec = jnp.exp(b_last)
        vb = v_ref[0, rows, :].astype(BF16)
        r = r_ref[0, rows, :]
        for h in range(GLA_HEADS):
            ks = slice(h * dk, (h + 1) * dk)
            vs = slice(h * dv, (h + 1) * dv)
            a = jnp.where(low_incl, _dot_nt(qe[:, ks], ke[:, ks]), 0.0)
            st = st_ref[h]
            o = _dot(a.astype(BF16), vb[:, vs]) + _dot_nt(qe[:, ks], st.astype(BF16))
            st_ref[h] = st * dec[:, ks] + _dot_tn(vb[:, vs], kd[:, ks])
            o = o * lax.rsqrt(jnp.mean(o * o, axis=-1, keepdims=True) + GLA_NORM_EPS) * nw
            rh = r[:, vs]
            o_ref[0, rows, vs] = (o * (rh * _sigmoid(rh))).astype(o_ref.dtype)


def _gla_layer(h, g_pre, g_post, w_in, a_up, a_bias, norm_w, w_o):
    B, S, D = h.shape
    dk, dv = GLA_KEY_DIM, GLA_VAL_DIM
    tm = min(PROJ_TM, S)
    n_main = 2 * dk + 2 * dv
    w_main = w_in[:, :n_main].astype(BF16)
    w_a = _pad_cols(w_in[:, n_main:], LORA_PAD).astype(BF16)
    aup = _pad_rows(a_up, LORA_PAD).astype(BF16)
    q, k, v, r, la = pl.pallas_call(
        _gla_proj_kernel,
        grid=(B, S // tm),
        in_specs=[_tok_spec(tm, D), _const_spec((1, D)), _const_spec((D, n_main)),
                  _const_spec((D, LORA_PAD)), _const_spec((LORA_PAD, dk)), _const_spec((1, dk))],
        out_specs=[_tok_spec(tm, dk), _tok_spec(tm, dk), _tok_spec(tm, dv), _tok_spec(tm, dv),
                   _tok_spec(tm, dk)],
        out_shape=[jax.ShapeDtypeStruct((B, S, w), F32) for w in (dk, dk, dv, dv, dk)],
        compiler_params=_params(),
        name="gla_proj",
    )(h, _row(g_pre), w_main, w_a, aup, _row(a_bias))

    chunks = 4 if S % (4 * GLA_CHUNK) == 0 else 1
    tc = chunks * GLA_CHUNK
    o = pl.pallas_call(
        functools.partial(_gla_chunk_kernel, chunks=chunks),
        grid=(B, S // tc),
        in_specs=[_tok_spec(tc, dk), _tok_spec(tc, dk), _tok_spec(tc, dv), _tok_spec(tc, dv),
                  _tok_spec(tc, dk), _const_spec((1, dv // GLA_HEADS))],
        out_specs=_tok_spec(tc, dv),
        out_shape=jax.ShapeDtypeStruct((B, S, dv), BF16),
        scratch_shapes=[pltpu.VMEM((GLA_HEADS, dv // GLA_HEADS, dk // GLA_HEADS), F32)],
        compiler_params=_params(),
        name="gla_chunk",
    )(q, k, v, r, la, _row(norm_w))
    return _post(h, o, w_o, g_post)


def _nsa_head_perm():
    perm = np.zeros((D_MODEL,), np.int32)
    for p in range(NSA_HPG):
        for g in range(NSA_KV_GROUPS):
            src = (g * NSA_HPG + p) * NSA_HEAD_DIM
            dst = p * LANES + g * NSA_HEAD_DIM
            perm[dst:dst + NSA_HEAD_DIM] = np.arange(src, src + NSA_HEAD_DIM)
    return perm


def _nsa_proj_kernel(h_ref, gn_ref, wq_ref, wkv_ref, wg_ref, q_ref, kv_ref, gates_ref):
    xn = _rms(h_ref[0], gn_ref[...], NORM_EPS).astype(BF16)
    q_ref[0] = _dot(xn, wq_ref[...]).astype(BF16)
    kv_ref[0] = _dot(xn, wkv_ref[...]).astype(BF16)
    gates_ref[0] = _sigmoid(_dot(xn, wg_ref[...]))


def _nsa_compress_kernel(r_ref, w1_ref, pe_ref, b1_ref, w2_ref, out_ref, *, n_cmp):
    r = r_ref[0, 0].astype(F32)
    rows = r.shape[0]
    pe = pe_ref[0]
    ra = (r + pe[0]).astype(BF16)
    rb = (pltpu.roll(r, rows - 1, 0) + pe[1]).astype(BF16)
    out = jnp.zeros((rows, LANES), F32)
    for g in range(NSA_KV_GROUPS):
        hid = _gelu_tanh(_dot(ra, w1_ref[0, g, 0]) + _dot(rb, w1_ref[0, g, 1]) + b1_ref[0])
        out = out + _dot(hid.astype(BF16), w2_ref[0, g])
    rowi = lax.broadcasted_iota(jnp.int32, out.shape, 0)
    out_ref[0, 0] = jnp.where(rowi < n_cmp, out, 0.0).astype(BF16)


def _group_queries(q, g):
    lane = lax.broadcasted_iota(jnp.int32, (q.shape[0], LANES), 1)
    keep = (lane < NSA_HEAD_DIM) if g == 0 else (lane >= NSA_HEAD_DIM)
    return [jnp.where(keep, q[:, p * LANES:(p + 1) * LANES], jnp.zeros((), q.dtype))
            for p in range(NSA_HPG)]


def _ones_in_other_group(v, g):
    lane = lax.broadcasted_iota(jnp.int32, v.shape, 1)
    other = (lane >= NSA_HEAD_DIM) if g == 0 else (lane < NSA_HEAD_DIM)
    return jnp.where(other, jnp.ones((), v.dtype), v)


def _normalize_groups(acc0, acc1):
    lane = lax.broadcasted_iota(jnp.int32, acc0.shape, 1)
    half = NSA_HEAD_DIM
    return jnp.where(lane < half, acc0 / pltpu.roll(acc0, half, 1), acc1 / pltpu.roll(acc1, half, 1))


def _nsa_cmp_kernel(q_ref, kc_ref, vc_ref, ov_ref, ocmp_ref, score_ref, *, n_cmp, n_slc):
    TQ = q_ref.shape[1]
    ncp = kc_ref.shape[1]
    nbp = ov_ref.shape[0]
    G = NSA_KV_GROUPS
    R = NSA_HPG * TQ
    CW = 2 * TQ
    n_cw = R // CW
    chains = range(G * n_cw)
    t0 = pl.program_id(1) * TQ
    q = q_ref[0]
    kc = kc_ref[0]
    vc = vc_ref[0]
    pos = t0 + lax.broadcasted_iota(jnp.int32, (1, TQ), 1)
    blk_c = lax.broadcasted_iota(jnp.int32, (ncp, 1), 0)
    vis = (blk_c * CMP_STRIDE + (CMP_BLOCK - 1) <= pos) & (blk_c < n_cmp)
    bias = jnp.where(vis, 0.0, NEG_INF)
    bias = jnp.concatenate([bias] * (CW // TQ), axis=1)
    any_vis = (pos >= CMP_BLOCK - 1).astype(F32)
    any_vis = jnp.concatenate([any_vis] * (CW // TQ), axis=1)
    qall = jnp.concatenate([t for g in range(G) for t in _group_queries(q, g)], axis=0)
    s = [_dot_nt(kc, qall[j * CW:(j + 1) * CW]) + bias for j in chains]
    e = [jnp.exp2(s[j] - jnp.max(s[j], axis=0, keepdims=True)) for j in chains]
    p = [e[j] * (any_vis / jnp.sum(e[j], axis=0, keepdims=True)) for j in chains]
    o = [_dot_tn(vc, p[j].astype(BF16)) for j in chains]

    blk_s = lax.broadcasted_iota(jnp.int32, (nbp, 1), 0)
    cur = lax.shift_right_logical(pos, 6)
    causal = (blk_s * SLC_BLOCK <= pos) & (blk_s < n_slc)
    forced = (blk_s == 0) | (blk_s == cur) | (blk_s == cur - 1)
    bonus = np.float32(FORCE_BONUS) * forced.astype(F32)
    for g in range(G):
        psum = jnp.zeros((ncp, TQ), F32)
        for j in range(g * n_cw, (g + 1) * n_cw):
            for c in range(CW // TQ):
                psum = psum + p[j][:, c * TQ:(c + 1) * TQ]
        hi, lo = _split2(psum)
        imp = _dot(ov_ref[...], hi) + _dot(ov_ref[...], lo)
        score_ref[0, g * nbp:(g + 1) * nbp, :] = jnp.where(causal, imp + bonus, NEG_INF)

    half = NSA_HEAD_DIM
    for p_i in range(NSA_HPG):
        j, c = divmod(p_i, CW // TQ)
        cols = slice(c * TQ, (c + 1) * TQ)
        o_pair = jnp.concatenate([o[j][:half, cols], o[n_cw + j][half:, cols]], axis=0)
        ocmp_ref[0, :, p_i * LANES:(p_i + 1) * LANES] = o_pair.T


def _nsa_topk_kernel(score_ref, sel_ref, *, top_n):
    G = NSA_KV_GROUPS
    nbp = score_ref.shape[1] // G
    sc0 = score_ref[0].reshape(G, nbp, score_ref.shape[2])
    row = lax.broadcasted_iota(jnp.int32, sc0.shape, 1).astype(F32)

    def body(_, carry):
        sc, sel = carry
        m = jnp.max(sc, axis=1, keepdims=True)
        idx = jnp.min(jnp.where(sc == m, row, np.float32(nbp)), axis=1, keepdims=True)
        hit = row == idx
        return jnp.where(hit, np.float32(-3e38), sc), jnp.where(hit, 1.0, sel)

    _, sel = lax.fori_loop(0, top_n, body, (sc0, jnp.zeros_like(sc0)))
    sel_ref[0] = sel.reshape(G * nbp, score_ref.shape[2]).astype(sel_ref.dtype)


def _nsa_attn_kernel(q_ref, ks_ref, vs_ref, kw_ref, vw_ref, sel_ref, ocmp_ref, gates_ref, ge_ref,
                     o_ref, m_ref, acc_ref, *, win_rows):
    TQ = q_ref.shape[1]
    S = ks_ref.shape[1]
    TK = min(NSA_TK, S)
    nbp = sel_ref.shape[1] // NSA_KV_GROUPS
    R = NSA_HPG * TQ
    CW = 2 * TQ
    n_cw = R // CW
    t0 = pl.program_id(1) * TQ
    q = q_ref[0]
    pos = t0 + lax.broadcasted_iota(jnp.int32, (1, TQ), 1)
    n_kt = (t0 + TQ + TK - 1) // TK
    blk_l = lax.broadcasted_iota(jnp.int32, (TK, nbp), 1)
    key_s = lax.broadcasted_iota(jnp.int32, (TK, 1), 0)
    w_start = pl.multiple_of(jnp.maximum(t0 - WINDOW, 0), TQ)
    wkey = w_start + lax.broadcasted_iota(jnp.int32, (win_rows, 1), 0)
    wbias = jnp.where((wkey <= pos) & (wkey > pos - WINDOW), 0.0, NEG_INF)
    wbias = jnp.concatenate([wbias] * (CW // TQ), axis=1)

    G = NSA_KV_GROUPS
    chains = range(G * n_cw)
    qall = jnp.concatenate([t for g in range(G) for t in _group_queries(q, g)], axis=0)

    def scores(k_tile, bias):
        return [_dot_nt(k_tile, qall[j * CW:(j + 1) * CW]) + bias[j // n_cw] for j in chains]

    m_ref[...] = jnp.full(m_ref.shape, NEG_INF, F32)
    acc_ref[...] = jnp.zeros_like(acc_ref)

    def body(kt, carry):
        k0 = pl.multiple_of(kt * TK, TK)
        expand = (blk_l == lax.shift_right_logical(k0 + key_s, 6)).astype(BF16)
        causal = k0 + key_s <= pos
        bias = []
        for g in range(G):
            chosen = _dot(expand, sel_ref[0, g * nbp:(g + 1) * nbp, :])
            b = jnp.where((chosen > 0.5) & causal, 0.0, NEG_INF)
            bias.append(jnp.concatenate([b] * (CW // TQ), axis=1))
        v_raw = vs_ref[0, pl.ds(k0, TK), :]
        v_tile = [_ones_in_other_group(v_raw, g) for g in range(G)]
        s = scores(ks_ref[0, pl.ds(k0, TK), :], bias)
        m_old = [m_ref[:, j * CW:(j + 1) * CW] for j in chains]
        m_new = [jnp.maximum(m_old[j], jnp.max(s[j], axis=0, keepdims=True)) for j in chains]
        p = [jnp.exp2(s[j] - m_new[j]).astype(BF16) for j in chains]
        pv = [_dot_tn(v_tile[j // n_cw], p[j]) for j in chains]
        for j in chains:
            cols = slice(j * CW, (j + 1) * CW)
            acc_ref[:, cols] = acc_ref[:, cols] * jnp.exp2(m_old[j] - m_new[j]) + pv[j]
            m_ref[:, cols] = m_new[j]
        return carry

    lax.fori_loop(0, n_kt, body, 0)

    vw_raw = vw_ref[0, pl.ds(w_start, win_rows), :]
    vw = [_ones_in_other_group(vw_raw, g) for g in range(G)]
    s = scores(kw_ref[0, pl.ds(w_start, win_rows), :], [wbias] * G)
    p = [jnp.exp2(s[j] - jnp.max(s[j], axis=0, keepdims=True)).astype(BF16) for j in chains]
    acc_win = jnp.concatenate([_dot_tn(vw[j // n_cw], p[j]) for j in chains], axis=1)

    half = NSA_HEAD_DIM
    normed = {}
    for name, a in (("slc", acc_ref[...]), ("win", acc_win)):
        a0, a1 = a[:, :R], a[:, R:]
        normed[name] = jnp.concatenate([a0[:half] / a0[half:half + 1], a1[half:] / a1[0:1]], axis=0)
    gates = gates_ref[0]
    g_cmp = _dot_hilo(gates, ge_ref[0])
    g_slc = _dot_hilo(gates, ge_ref[1])
    g_win = _dot_hilo(gates, ge_ref[2])
    for p_i in range(NSA_HPG):
        cols = slice(p_i * LANES, (p_i + 1) * LANES)
        slc = normed["slc"][:, p_i * TQ:(p_i + 1) * TQ].T
        win = normed["win"][:, p_i * TQ:(p_i + 1) * TQ].T
        o_ref[0, :, cols] = (g_cmp[:, cols] * ocmp_ref[0, :, cols] + g_slc[:, cols] * slc
                             + g_win[:, cols] * win).astype(o_ref.dtype)


def _nsa_layer(h, g_pre, g_post, w_in, cmp_w1, cmp_b1, cmp_w2, cmp_pe, w_o):
    B, S, D = h.shape
    H, G, dh, hpg = NSA_HEADS, NSA_KV_GROUPS, NSA_HEAD_DIM, NSA_HPG
    perm = _nsa_head_perm()
    n_q = H * dh
    n_kv = 6 * G * dh
    wq = (w_in[:, :n_q] * np.float32(dh ** -0.5 * np.log2(np.e)))[:, perm].astype(BF16)
    wkv = w_in[:, n_q:n_q + n_kv].astype(BF16)
    wgt = _pad_cols(w_in[:, n_q + n_kv:], LANES).astype(BF16)
    tm = min(PROJ_TM, S)
    q, kv, gates = pl.pallas_call(
        _nsa_proj_kernel,
        grid=(B, S // tm),
        in_specs=[_tok_spec(tm, D), _const_spec((1, D)), _const_spec((D, n_q)),
                  _const_spec((D, n_kv)), _const_spec((D, LANES))],
        out_specs=[_tok_spec(tm, n_q), _tok_spec(tm, n_kv), _tok_spec(tm, LANES)],
        out_shape=[jax.ShapeDtypeStruct((B, S, n_q), BF16), jax.ShapeDtypeStruct((B, S, n_kv), BF16),
                   jax.ShapeDtypeStruct((B, S, LANES), F32)],
        compiler_params=_params(),
        name="nsa_proj",
    )(h, _row(g_pre), wq, wkv, wgt)

    n_cmp = (S - CMP_BLOCK) // CMP_STRIDE + 1
    ncp = S // CMP_STRIDE
    half = CMP_BLOCK // 2
    assert CMP_STRIDE == half
    rk = jnp.stack([kv[:, :, :LANES], kv[:, :, LANES:2 * LANES]]).reshape(2, B, ncp, half * LANES)
    w1 = cmp_w1.reshape(2, 2, half, dh, CMP_HIDDEN)
    w1x = jnp.zeros((2, G, 2, half, G, dh, CMP_HIDDEN), F32)
    for g in range(G):
        w1x = w1x.at[:, g, :, :, g].set(w1)
    w1x = w1x.reshape(2, G, 2, half * LANES, CMP_HIDDEN).astype(BF16)
    pe = jnp.broadcast_to(cmp_pe.reshape(2, 2, half, 1, dh), (2, 2, half, G, dh))
    pe = pe.reshape(2, 2, 1, half * LANES).astype(F32)
    w2x = jnp.zeros((2, G, CMP_HIDDEN, G, dh), F32)
    for g in range(G):
        w2x = w2x.at[:, g, :, g].set(cmp_w2)
    w2x = w2x.reshape(2, G, CMP_HIDDEN, LANES).astype(BF16)
    kvc = pl.pallas_call(
        functools.partial(_nsa_compress_kernel, n_cmp=n_cmp),
        grid=(2, B),
        in_specs=[pl.BlockSpec((1, 1, ncp, half * LANES), lambda c, b: (c, b, 0, 0)),
                  pl.BlockSpec((1, G, 2, half * LANES, CMP_HIDDEN), lambda c, b: (c, 0, 0, 0, 0)),
                  pl.BlockSpec((1, 2, 1, half * LANES), lambda c, b: (c, 0, 0, 0)),
                  pl.BlockSpec((1, 1, CMP_HIDDEN), lambda c, b: (c, 0, 0)),
                  pl.BlockSpec((1, G, CMP_HIDDEN, LANES), lambda c, b: (c, 0, 0, 0))],
        out_specs=pl.BlockSpec((1, 1, ncp, LANES), lambda c, b: (c, b, 0, 0)),
        out_shape=jax.ShapeDtypeStruct((2, B, ncp, LANES), BF16),
        compiler_params=_params(),
        name="nsa_compress",
    )(rk, w1x, pe, cmp_b1.reshape(2, 1, CMP_HIDDEN).astype(F32), w2x)

    n_slc = S // SLC_BLOCK
    top_n = min(SLC_TOPK, n_slc)
    nbp = max(LANES, n_slc)
    start = np.arange(ncp) * CMP_STRIDE
    end = start + CMP_BLOCK - 1
    s_start = np.arange(nbp) * SLC_BLOCK
    s_end = s_start + SLC_BLOCK - 1
    overlap = ((end[:, None] >= s_start[None, :]) & (start[:, None] <= s_end[None, :])
               & (np.arange(ncp)[:, None] < n_cmp) & (np.arange(nbp)[None, :] < n_slc))
    overlap_t = jnp.asarray(overlap.astype(np.float32).T, BF16)
    TQ = min(NSA_TQ, S)
    blk_tok_spec = lambda t: pl.BlockSpec((1, G * nbp, t), lambda b, i: (b, 0, i))
    o_cmp, score = pl.pallas_call(
        functools.partial(_nsa_cmp_kernel, n_cmp=n_cmp, n_slc=n_slc),
        grid=(B, S // TQ),
        in_specs=[_tok_spec(TQ, n_q),
                  pl.BlockSpec((None, 1, ncp, LANES), lambda b, i: (0, b, 0, 0)),
                  pl.BlockSpec((None, 1, ncp, LANES), lambda b, i: (1, b, 0, 0)),
                  _const_spec((nbp, ncp))],
        out_specs=[_tok_spec(TQ, n_q), blk_tok_spec(TQ)],
        out_shape=[jax.ShapeDtypeStruct((B, S, n_q), F32),
                   jax.ShapeDtypeStruct((B, G * nbp, S), F32)],
        compiler_params=_params(),
        name="nsa_compressed_attn",
    )(q, kvc, kvc, overlap_t)
    tr = min(NSA_TOPK_TOKENS, S)
    sel = pl.pallas_call(
        functools.partial(_nsa_topk_kernel, top_n=top_n),
        grid=(B, S // tr),
        in_specs=[blk_tok_spec(tr)],
        out_specs=blk_tok_spec(tr),
        out_shape=jax.ShapeDtypeStruct((B, G * nbp, S), BF16),
        compiler_params=_params(),
        name="nsa_topk",
    )(score)

    ge = np.zeros((3, LANES, n_q), np.float32)
    for g in range(G):
        for p in range(hpg):
            for j in range(3):
                ge[j, g * hpg * 3 + p * 3 + j, p * LANES + g * dh:p * LANES + (g + 1) * dh] = 1.0
    ge = jnp.asarray(ge, BF16)
    win_rows = min(WINDOW + TQ, S)
    kv_spec = lambda c: pl.BlockSpec((1, S, LANES), lambda b, i, c=c: (b, 0, c))
    o = pl.pallas_call(
        functools.partial(_nsa_attn_kernel, win_rows=win_rows),
        grid=(B, S // TQ),
        in_specs=[_tok_spec(TQ, n_q), kv_spec(2), kv_spec(3), kv_spec(4), kv_spec(5),
                  blk_tok_spec(TQ), _tok_spec(TQ, n_q), _tok_spec(TQ, LANES),
                  _const_spec((3, LANES, n_q))],
        out_specs=_tok_spec(TQ, n_q),
        out_shape=jax.ShapeDtypeStruct((B, S, n_q), BF16),
        scratch_shapes=[pltpu.VMEM((1, G * hpg * TQ), F32), pltpu.VMEM((LANES, G * hpg * TQ), F32)],
        compiler_params=_params(),
        name="nsa_selected_window_attn",
    )(q, kv, kv, kv, kv, sel, o_cmp, gates, ge)
    return _post(h, o, w_o[perm, :], g_post)


def kernel(x, norm_g, ffn_w_in, ffn_conv_w, ffn_conv_b, ffn_w_out, rwkv_mu, rwkv_w_rkv, rwkv_w0, rwkv_w1, rwkv_w2, rwkv_a0, rwkv_a1, rwkv_a2, rwkv_g1, rwkv_g2, rwkv_k_k, rwkv_k_a, rwkv_r_k, rwkv_ln_w, rwkv_ln_b, rwkv_w_o, rwkv_v0, rwkv_v1, rwkv_v2, nsa_w_in, nsa_cmp_w1, nsa_cmp_b1, nsa_cmp_w2, nsa_cmp_pe, nsa_w_o, gla_w_in, gla_a_up, gla_a_bias, gla_norm_w, gla_w_o):
    depth = norm_g.shape[0]
    h = x
    v_first = None
    for i in range(depth):
        kind, j = i % 3, i // 3
        if kind == 0:
            vres = None if j == 0 else (rwkv_v0[j - 1], rwkv_v1[j - 1], rwkv_v2[j - 1])
            h, v_first = _rwkv_layer(
                h, norm_g[i, 0], norm_g[i, 1], rwkv_mu[j], rwkv_w_rkv[j], rwkv_w0[j], rwkv_w1[j],
                rwkv_w2[j], rwkv_a0[j], rwkv_a1[j], rwkv_a2[j], rwkv_g1[j], rwkv_g2[j], rwkv_k_k[j],
                rwkv_k_a[j], rwkv_r_k[j], rwkv_ln_w[j], rwkv_ln_b[j], rwkv_w_o[j], v_first, vres)
        elif kind == 1:
            h = _nsa_layer(h, norm_g[i, 0], norm_g[i, 1], nsa_w_in[j], nsa_cmp_w1[j], nsa_cmp_b1[j],
                           nsa_cmp_w2[j], nsa_cmp_pe[j], nsa_w_o[j])
        else:
            h = _gla_layer(h, norm_g[i, 0], norm_g[i, 1], gla_w_in[j], gla_a_up[j], gla_a_bias[j],
                           gla_norm_w[j], gla_w_o[j])
        h = _ffn(h, norm_g[i, 2], ffn_w_in[i], ffn_conv_w[i], ffn_conv_b[i], ffn_w_out[i],
                 norm_g[i, 3])
    return h
```

```python
import functools

import numpy as np
import jax
import jax.numpy as jnp
from jax import lax
from jax.experimental import pallas as pl
from jax.experimental.pallas import tpu as pltpu

F32 = jnp.float32
BF16 = jnp.bfloat16

V7X_VMEM_BYTES = 64 * 1024 * 1024
VMEM_LIMIT_BYTES = V7X_VMEM_BYTES - 8 * 1024 * 1024
LANES = 128
SUBLANES = 8

D_MODEL = 1024
NORM_EPS = 1e-6
NEG_INF = -1e30

RWKV_HEAD_DIM = 64
RWKV_HEADS = D_MODEL // RWKV_HEAD_DIM
RWKV_GN_EPS = 64e-5
RWKV_CHUNK = 64
RWKV_CHUNKS_PER_STEP = 2
LORA_PAD = 128

NSA_HEADS = 16
NSA_KV_GROUPS = 2
NSA_HEAD_DIM = D_MODEL // NSA_HEADS
NSA_HPG = NSA_HEADS // NSA_KV_GROUPS
CMP_BLOCK = 32
CMP_STRIDE = 16
CMP_HIDDEN = 256
SLC_BLOCK = 64
SLC_TOPK = 16
WINDOW = 512
FORCE_BONUS = 1e4
NSA_TQ = 128
NSA_TK = 512
NSA_TOPK_TOKENS = 512

GLA_HEADS = 4
GLA_KEY_DIM = D_MODEL // 2
GLA_VAL_DIM = D_MODEL
GLA_GATE_LORA = 16
GLA_TAU = 16.0
GLA_CHUNK = 64
GLA_NORM_EPS = 1e-5

D_FF = 2816
FFN_CHUNK = 256
FFN_TM = 512
PROJ_TM = 256
POST_TM = 512


def _dot(a, b):
    return jnp.dot(a, b, preferred_element_type=F32)


def _dot_nt(a, b):
    return lax.dot_general(a, b, (((1,), (1,)), ((), ())), preferred_element_type=F32)


def _dot_tn(a, b):
    return lax.dot_general(a, b, (((0,), (0,)), ((), ())), preferred_element_type=F32)


def _split2(x):
    hi = x.astype(BF16)
    lo = (x - hi.astype(F32)).astype(BF16)
    return hi, lo


def _dot_hilo(x, m):
    hi, lo = _split2(x)
    return _dot(hi, m) + _dot(lo, m)


def _dot_tri3(tri, x):
    hi = x.astype(BF16)
    r1 = x - hi.astype(F32)
    mid = r1.astype(BF16)
    lo = (r1 - mid.astype(F32)).astype(BF16)
    return _dot(tri, hi) + _dot(tri, mid) + _dot(tri, lo)


def _head_sum(x, red_ref, exp_ref):
    return _dot_hilo(_dot_hilo(x, red_ref[...]), exp_ref[...])


def _rms(x, g, eps):
    return x * lax.rsqrt(jnp.mean(x * x, axis=-1, keepdims=True) + eps) * g


def _sigmoid(x):
    return 1.0 / (1.0 + jnp.exp(-x))


def _softplus(x):
    return jnp.maximum(x, 0.0) + jnp.log(1.0 + jnp.exp(-jnp.abs(x)))


def _gelu_tanh(x):
    c = np.float32(np.sqrt(2.0 / np.pi))
    return 0.5 * x * (1.0 + jnp.tanh(c * (x + 0.044715 * (x * x * x))))


def _shift_rows(x, n, prev8):
    rolled = pltpu.roll(x, n, 0)
    row = lax.broadcasted_iota(jnp.int32, x.shape, 0)
    for j in range(n):
        rolled = jnp.where(row == j, prev8[SUBLANES - n + j:SUBLANES - n + j + 1, :], rolled)
    return rolled


def _const_spec(shape):
    nd = len(shape)
    return pl.BlockSpec(shape, lambda *_: (0,) * nd, pipeline_mode=pl.Buffered(1))


def _tok_spec(tm, width):
    return pl.BlockSpec((1, tm, width), lambda b, i: (b, i, 0))


def _params():
    return pltpu.CompilerParams(dimension_semantics=("arbitrary", "arbitrary"),
                                vmem_limit_bytes=VMEM_LIMIT_BYTES)


def _row(v):
    return v.reshape(1, -1).astype(F32)


def _ffn_kernel(h_ref, gpre_ref, wg_ref, wu_ref, cw_ref, cb_ref, wo_ref, gpost_ref, out_ref,
                xn_ref, acc_ref, halo_ref, *, nc):
    @pl.when(pl.program_id(1) == 0)
    def _():
        halo_ref[...] = jnp.zeros_like(halo_ref)

    h = h_ref[0]
    xn_ref[...] = _rms(h, gpre_ref[...], NORM_EPS).astype(BF16)
    acc_ref[...] = jnp.zeros_like(acc_ref)
    tm = h.shape[0]

    def gate_up(c):
        xn = xn_ref[...]
        return _dot(xn, wg_ref[c]), _dot(xn, wu_ref[c])

    nxt = gate_up(0)
    for c in range(nc):
        gate, up = nxt
        if c + 1 < nc:
            nxt = gate_up(c + 1)
        prev = halo_ref[c]
        g1 = _shift_rows(gate, 1, prev)
        g2 = _shift_rows(gate, 2, prev)
        cw = cw_ref[c]
        z = cw[2:3] * gate + cw[1:2] * g1 + cw[0:1] * g2 + cb_ref[c]
        halo_ref[c] = gate[tm - SUBLANES:tm, :]
        act = (z * _sigmoid(z) * up).astype(BF16)
        acc_ref[...] += _dot(act, wo_ref[c])
    out_ref[0] = h + _rms(acc_ref[...], gpost_ref[...], NORM_EPS)


def _ffn(h, g_pre, w_in, conv_w, conv_b, w_out, g_post):
    B, S, D = h.shape
    F = w_out.shape[0]
    fc = FFN_CHUNK
    nc = F // fc
    tm = min(FFN_TM, S)
    wg = w_in[:, :F].reshape(D, nc, fc).transpose(1, 0, 2).astype(BF16)
    wu = w_in[:, F:].reshape(D, nc, fc).transpose(1, 0, 2).astype(BF16)
    cw = conv_w.reshape(3, nc, fc).transpose(1, 0, 2)
    cb = conv_b.reshape(nc, 1, fc)
    wo = w_out.reshape(nc, fc, D).astype(BF16)
    return pl.pallas_call(
        functools.partial(_ffn_kernel, nc=nc),
        grid=(B, S // tm),
        in_specs=[_tok_spec(tm, D), _const_spec((1, D)), _const_spec((nc, D, fc)),
                  _const_spec((nc, D, fc)), _const_spec((nc, 3, fc)), _const_spec((nc, 1, fc)),
                  _const_spec((nc, fc, D)), _const_spec((1, D))],
        out_specs=_tok_spec(tm, D),
        out_shape=jax.ShapeDtypeStruct((B, S, D), F32),
        scratch_shapes=[pltpu.VMEM((tm, D), BF16), pltpu.VMEM((tm, D), F32),
                        pltpu.VMEM((nc, SUBLANES, fc), F32)],
        compiler_params=_params(),
        name="conv_ffn",
    )(h, _row(g_pre), wg, wu, cw, cb, wo, _row(g_post))


def _post_kernel(h_ref, o_ref, wo_ref, g_ref, out_ref):
    y = _dot(o_ref[0].astype(BF16), wo_ref[...])
    out_ref[0] = h_ref[0] + _rms(y, g_ref[...], NORM_EPS)


def _post(h, o, w_o, g_post):
    B, S, D = h.shape
    K = o.shape[-1]
    tm = min(POST_TM, S)
    return pl.pallas_call(
        _post_kernel,
        grid=(B, S // tm),
        in_specs=[_tok_spec(tm, D), _tok_spec(tm, K), _const_spec((K, D)), _const_spec((1, D))],
        out_specs=_tok_spec(tm, D),
        out_shape=jax.ShapeDtypeStruct((B, S, D), F32),
        compiler_params=_params(),
        name="mixer_out",
    )(h, o, w_o.astype(BF16), _row(g_post))


def _rwkv_proj_kernel(*refs, has_vres):
    if has_vres:
        (h_ref, gn_ref, mu_ref, wr_ref, wk_ref, wv_ref, w1_ref, w2_ref, a1_ref, a2_ref, g1_ref,
         g2_ref, vec_ref, red_ref, exp_ref, vf_ref, v1_ref, v2_ref,
         r_ref, ld_ref, k_ref, v_ref, kkn_ref, b_ref, g_ref, bonus_ref, carry_ref) = refs
    else:
        (h_ref, gn_ref, mu_ref, wr_ref, wk_ref, wv_ref, w1_ref, w2_ref, a1_ref, a2_ref, g1_ref,
         g2_ref, vec_ref, red_ref, exp_ref,
         r_ref, ld_ref, k_ref, v_ref, kkn_ref, b_ref, g_ref, bonus_ref, carry_ref) = refs

    @pl.when(pl.program_id(1) == 0)
    def _():
        carry_ref[...] = jnp.zeros_like(carry_ref)

    x = _rms(h_ref[0], gn_ref[...], NORM_EPS)
    tm = x.shape[0]
    xx = _shift_rows(x, 1, carry_ref[...]) - x
    carry_ref[...] = x[tm - SUBLANES:tm, :]
    mu = mu_ref[...]
    xr, xw, xk, xv, xa, xg = ((x + xx * mu[c:c + 1]).astype(BF16) for c in range(6))
    vec = vec_ref[...]
    w0, a0, k_k, k_a, r_k, v0 = (vec[c:c + 1] for c in range(6))

    r = _dot(xr, wr_ref[...])
    k = _dot(xk, wk_ref[...])
    v = _dot(xv, wv_ref[...])
    wl = -_softplus(-(w0 + _dot(jnp.tanh(_dot(xw, w1_ref[...])).astype(BF16), w2_ref[...]))) - 0.5
    ld_ref[0] = -jnp.exp(wl)
    a = _sigmoid(a0 + _dot(_dot(xa, a1_ref[...]).astype(BF16), a2_ref[...]))
    g_ref[0] = _dot(_sigmoid(_dot(xg, g1_ref[...])).astype(BF16), g2_ref[...])
    if has_vres:
        mix = _sigmoid(v0 + _dot(_dot(xv, v1_ref[...]).astype(BF16), v2_ref[...]))
        v = v + (vf_ref[0] - v) * mix
    kk = k * k_k
    norm = jnp.sqrt(_head_sum(kk * kk, red_ref, exp_ref))
    kkn = kk / jnp.maximum(norm, 1e-12)
    k = k * (1.0 + (a - 1.0) * k_a)
    r_ref[0] = r
    k_ref[0] = k
    v_ref[0] = v
    kkn_ref[0] = kkn
    b_ref[0] = kkn * a
    bonus_ref[0] = _head_sum(r * k * r_k, red_ref, exp_ref) * v


def _pad_cols(w, n):
    return jnp.pad(w, ((0, 0), (0, n - w.shape[1])))


def _pad_rows(w, n):
    return jnp.pad(w, ((0, n - w.shape[0]), (0, 0)))


def _head_reduce_expand(d, hd, value):
    red = (np.arange(d)[:, None] // hd == np.arange(LANES)[None, :]).astype(np.float32)
    return jnp.asarray(red, BF16), jnp.asarray(red.T * value, BF16)


def _rwkv_proj(h, g_norm, mu, w_rkv, w0, w1, w2, a0, a1, a2, g1, g2, k_k, k_a, r_k, v_first, vres):
    B, S, D = h.shape
    tm = min(PROJ_TM, S)
    has_vres = vres is not None
    v0 = vres[0] if has_vres else jnp.zeros((D,), F32)
    vec = jnp.stack([w0, a0, k_k, k_a, r_k.reshape(-1), v0]).astype(F32)
    red, exp = _head_reduce_expand(D, RWKV_HEAD_DIM, 1.0)
    lora_in = lambda w: _pad_cols(w, LORA_PAD).astype(BF16)
    lora_out = lambda w: _pad_rows(w, LORA_PAD).astype(BF16)
    args = [h, _row(g_norm), mu, w_rkv[0].astype(BF16), w_rkv[1].astype(BF16), w_rkv[2].astype(BF16),
            lora_in(w1), lora_out(w2), lora_in(a1), lora_out(a2), lora_in(g1), lora_out(g2), vec, red, exp]
    specs = [_tok_spec(tm, D), _const_spec((1, D)), _const_spec((6, D)), _const_spec((D, D)),
             _const_spec((D, D)), _const_spec((D, D)), _const_spec((D, LORA_PAD)),
             _const_spec((LORA_PAD, D)), _const_spec((D, LORA_PAD)), _const_spec((LORA_PAD, D)),
             _const_spec((D, LORA_PAD)), _const_spec((LORA_PAD, D)), _const_spec((6, D)),
             _const_spec((D, LANES)), _const_spec((LANES, D))]
    if has_vres:
        args += [v_first, lora_in(vres[1]), lora_out(vres[2])]
        specs += [_tok_spec(tm, D), _const_spec((D, LORA_PAD)), _const_spec((LORA_PAD, D))]
    outs = pl.pallas_call(
        functools.partial(_rwkv_proj_kernel, has_vres=has_vres),
        grid=(B, S // tm),
        in_specs=specs,
        out_specs=[_tok_spec(tm, D)] * 8,
        out_shape=[jax.ShapeDtypeStruct((B, S, D), F32)] * 8,
        scratch_shapes=[pltpu.VMEM((SUBLANES, D), F32)],
        compiler_params=_params(),
        name="rwkv_proj",
    )(*args)
    return outs


def _rwkv_scan_kernel(r_ref, ld_ref, k_ref, v_ref, kkn_ref, b_ref, y_ref, st_ref):
    @pl.when(pl.program_id(1) == 0)
    def _():
        st_ref[...] = jnp.zeros_like(st_ref)

    C = RWKV_CHUNK
    N = RWKV_HEAD_DIM
    assert 2 * N == LANES and C == N
    rowi = lax.broadcasted_iota(jnp.int32, (C, C), 0)
    coli = lax.broadcasted_iota(jnp.int32, (C, C), 1)
    tri = (rowi >= coli).astype(BF16)
    prow = lax.broadcasted_iota(jnp.int32, (C, LANES), 0)
    pcol = lax.broadcasted_iota(jnp.int32, (C, LANES), 1) & (N - 1)
    low_incl = prow >= pcol
    low_strict = prow > pcol
    eye = (prow == pcol).astype(F32)
    srow = lax.broadcasted_iota(jnp.int32, (LANES, LANES), 0)
    scol = lax.broadcasted_iota(jnp.int32, (LANES, LANES), 1)
    same_head = (srow < N) == (scol < N)

    def bdiag(x):
        lane = lax.broadcasted_iota(jnp.int32, x.shape, 1)
        zero = jnp.zeros((), x.dtype)
        return jnp.concatenate([jnp.where(lane < N, x, zero), jnp.where(lane >= N, x, zero)], axis=0)

    n_chunks = r_ref.shape[1] // C
    n_pairs = RWKV_HEADS // 2
    sls = [slice(p * LANES, (p + 1) * LANES) for p in range(n_pairs)]
    a_t, r_t, b_t, k_t, b_h, k_h, vb, p_all = ([] for _ in range(8))
    for ci in range(n_chunks):
        rows = pl.ds(ci * C, C)
        ld = ld_ref[0, rows, :]
        cum = _dot_tri3(tri, ld)
        cum_last = cum[C - 1:C, :]
        p_inv = jnp.exp(-cum)
        p_rem = jnp.exp(cum_last - cum)
        p_all.append(jnp.exp(cum_last))
        b = b_ref[0, rows, :]
        k = k_ref[0, rows, :]
        a_t.append((-kkn_ref[0, rows, :] * jnp.exp(cum - ld)).astype(BF16))
        r_t.append((r_ref[0, rows, :] * jnp.exp(cum)).astype(BF16))
        b_t.append((b * p_inv).astype(BF16))
        k_t.append((k * p_inv).astype(BF16))
        b_h.append((b * p_rem).astype(BF16))
        k_h.append((k * p_rem).astype(BF16))
        vb.append(v_ref[0, rows, :].astype(BF16))

    units = [(ci, p) for ci in range(n_chunks) for p in range(n_pairs)]
    idx = range(len(units))
    tile = lambda arr, u: arr[units[u][0]][:, sls[units[u][1]]]
    ar = [jnp.concatenate([tile(a_t, u), tile(r_t, u)], axis=0) for u in idx]
    s_b = [_dot_nt(ar[u], bdiag(tile(b_t, u))) for u in idx]
    s_k = [_dot_nt(ar[u], bdiag(tile(k_t, u))) for u in idx]
    a_ab = [jnp.where(low_strict, s_b[u][:C], 0.0) for u in idx]
    a_rb = [jnp.where(low_incl, s_b[u][C:], 0.0).astype(BF16) for u in idx]
    a_ak = [jnp.where(low_strict, s_k[u][:C], 0.0).astype(BF16) for u in idx]
    a_rk = [jnp.where(low_incl, s_k[u][C:], 0.0).astype(BF16) for u in idx]
    av = [_dot(jnp.concatenate([a_ak[u], a_rk[u]], axis=0), bdiag(tile(vb, u))) for u in idx]
    akv = [av[u][:C].astype(BF16) for u in idx]
    pw = [a_ab[u].astype(BF16) for u in idx]
    tinv = [eye + a_ab[u] for u in idx]
    pw = [_dot(pw[u], bdiag(pw[u])).astype(BF16) for u in idx]
    span = 4
    while span < C:
        both = [_dot(jnp.concatenate([pw[u], tinv[u].astype(BF16)], axis=0), bdiag(pw[u]))
                for u in idx]
        tinv = [tinv[u] + both[u][C:] for u in idx]
        pw = [both[u][:C].astype(BF16) for u in idx]
        span *= 2
    tinv = [tinv[u] + _dot(tinv[u].astype(BF16), bdiag(pw[u])) for u in idx]
    tb = [tinv[u].astype(BF16) for u in idx]
    a_hat = [_dot(tb[u], bdiag(tile(a_t, u))).astype(BF16) for u in idx]
    u_loc = [_dot(tb[u], bdiag(akv[u])) for u in idx]

    pairs = range(n_pairs)
    st = [st_ref[p] for p in pairs]
    for ci in range(n_chunks):
        u0 = ci * n_pairs
        on_state = [_dot_nt(jnp.concatenate([a_hat[u0 + p], r_t[ci][:, sls[p]]], axis=0),
                            st[p].astype(BF16)) for p in pairs]
        ub = [(on_state[p][:C] + u_loc[u0 + p]).astype(BF16) for p in pairs]
        for p in pairs:
            y_ref[0, pl.ds(ci * C, C), sls[p]] = (on_state[p][C:] + _dot(a_rb[u0 + p], bdiag(ub[p]))
                                                  + av[u0 + p][C:])
        upd = [_dot_tn(jnp.concatenate([ub[p], vb[ci][:, sls[p]]], axis=0),
                       jnp.concatenate([b_h[ci][:, sls[p]], k_h[ci][:, sls[p]]], axis=0))
               for p in pairs]
        st = [st[p] * p_all[ci][:, sls[p]] + jnp.where(same_head, upd[p], 0.0) for p in pairs]
    for p in pairs:
        st_ref[p] = st[p]


def _rwkv_scan(r, ld, k, v, kkn, b):
    B, S, D = r.shape
    C = RWKV_CHUNK * RWKV_CHUNKS_PER_STEP
    return pl.pallas_call(
        _rwkv_scan_kernel,
        grid=(B, S // C),
        in_specs=[_tok_spec(C, D)] * 6,
        out_specs=_tok_spec(C, D),
        out_shape=jax.ShapeDtypeStruct((B, S, D), F32),
        scratch_shapes=[pltpu.VMEM((RWKV_HEADS // 2, LANES, LANES), F32)],
        compiler_params=_params(),
        name="rwkv_scan",
    )(r, ld, k, v, kkn, b)


def _rwkv_post_kernel(h_ref, y_ref, bonus_ref, g_ref, vec_ref, red_ref, avg_ref, wo_ref, gpost_ref,
                      out_ref):
    y = y_ref[0]
    d = y - _head_sum(y, red_ref, avg_ref)
    var = _head_sum(d * d, red_ref, avg_ref)
    vec = vec_ref[...]
    yn = d * lax.rsqrt(var + RWKV_GN_EPS) * vec[0:1] + vec[1:2]
    o = ((yn + bonus_ref[0]) * g_ref[0]).astype(BF16)
    out_ref[0] = h_ref[0] + _rms(_dot(o, wo_ref[...]), gpost_ref[...], NORM_EPS)


def _rwkv_post(h, y, bonus, g, ln_w, ln_b, w_o, g_post):
    B, S, D = h.shape
    tm = min(PROJ_TM, S)
    vec = jnp.stack([ln_w, ln_b]).astype(F32)
    red, avg = _head_reduce_expand(D, RWKV_HEAD_DIM, 1.0 / RWKV_HEAD_DIM)
    return pl.pallas_call(
        _rwkv_post_kernel,
        grid=(B, S // tm),
        in_specs=[_tok_spec(tm, D)] * 4 + [_const_spec((2, D)), _const_spec((D, LANES)),
                                           _const_spec((LANES, D)), _const_spec((D, D)),
                                           _const_spec((1, D))],
        out_specs=_tok_spec(tm, D),
        out_shape=jax.ShapeDtypeStruct((B, S, D), F32),
        compiler_params=_params(),
        name="rwkv_out",
    )(h, y, bonus, g, vec, red, avg, w_o.astype(BF16), _row(g_post))


def _rwkv_layer(h, g_pre, g_post, mu, w_rkv, w0, w1, w2, a0, a1, a2, g1, g2, k_k, k_a, r_k,
                ln_w, ln_b, w_o, v_first, vres):
    r, ld, k, v, kkn, b, g, bonus = _rwkv_proj(h, g_pre, mu, w_rkv, w0, w1, w2, a0, a1, a2, g1, g2,
                                               k_k, k_a, r_k, v_first, vres)
    if vres is None:
        v_first = v
    y = _rwkv_scan(r, ld, k, v, kkn, b)
    return _rwkv_post(h, y, bonus, g, ln_w, ln_b, w_o, g_post), v_first


def _gla_proj_kernel(h_ref, gn_ref, w_ref, wa_ref, aup_ref, abias_ref,
                     q_ref, k_ref, v_ref, r_ref, la_ref):
    xn = _rms(h_ref[0], gn_ref[...], NORM_EPS).astype(BF16)
    qkvr = _dot(xn, w_ref[...])
    dk = GLA_KEY_DIM
    dv = GLA_VAL_DIM
    q_ref[0] = qkvr[:, :dk] * np.float32((dk // GLA_HEADS) ** -0.5)
    k_ref[0] = qkvr[:, dk:2 * dk]
    v_ref[0] = qkvr[:, 2 * dk:2 * dk + dv]
    r_ref[0] = qkvr[:, 2 * dk + dv:]
    z = _dot(_dot(xn, wa_ref[...]).astype(BF16), aup_ref[...]) + abias_ref[...]
    la_ref[0] = -_softplus(-z) * np.float32(1.0 / GLA_TAU)


def _gla_chunk_kernel(q_ref, k_ref, v_ref, r_ref, la_ref, nw_ref, o_ref, st_ref, *, chunks):
    @pl.when(pl.program_id(1) == 0)
    def _():
        st_ref[...] = jnp.zeros_like(st_ref)

    C = GLA_CHUNK
    dk = GLA_KEY_DIM // GLA_HEADS
    dv = GLA_VAL_DIM // GLA_HEADS
    rowi = lax.broadcasted_iota(jnp.int32, (C, C), 0)
    coli = lax.broadcasted_iota(jnp.int32, (C, C), 1)
    low_incl = rowi >= coli
    tri = low_incl.astype(BF16)
    nw = nw_ref[...]

    for ci in range(chunks):
        rows = pl.ds(ci * C, C)
        bcum = _dot_tri3(tri, la_ref[0, rows, :])
        b_last = bcum[C - 1:C, :]
        k = k_ref[0, rows, :]
        qe = (q_ref[0, rows, :] * jnp.exp(bcum)).astype(BF16)
        ke = (k * jnp.exp(-bcum)).astype(BF16)
        kd = (k * jnp.exp(b_last - bcum)).astype(BF16)
        dec = jnp.exp(b_last)
        vb = v_ref[0, rows, :].astype(BF16)
        r = r_ref[0, rows, :]
        for h in range(GLA_HEADS):
            ks = slice(h * dk, (h + 1) * dk)
            vs = slice(h * dv, (h + 1) * dv)
            a = jnp.where(low_incl, _dot_nt(qe[:, ks], ke[:, ks]), 0.0)
            st = st_ref[h]
            o = _dot(a.astype(BF16), vb[:, vs]) + _dot_nt(qe[:, ks], st.astype(BF16))
            st_ref[h] = st * dec[:, ks] + _dot_tn(vb[:, vs], kd[:, ks])
            o = o * lax.rsqrt(jnp.mean(o * o, axis=-1, keepdims=True) + GLA_NORM_EPS) * nw
            rh = r[:, vs]
            o_ref[0, rows, vs] = (o * (rh * _sigmoid(rh))).astype(o_ref.dtype)


def _gla_layer(h, g_pre, g_post, w_in, a_up, a_bias, norm_w, w_o):
    B, S, D = h.shape
    dk, dv = GLA_KEY_DIM, GLA_VAL_DIM
    tm = min(PROJ_TM, S)
    n_main = 2 * dk + 2 * dv
    w_main = w_in[:, :n_main].astype(BF16)
    w_a = _pad_cols(w_in[:, n_main:], LORA_PAD).astype(BF16)
    aup = _pad_rows(a_up, LORA_PAD).astype(BF16)
    q, k, v, r, la = pl.pallas_call(
        _gla_proj_kernel,
        grid=(B, S // tm),
        in_specs=[_tok_spec(tm, D), _const_spec((1, D)), _const_spec((D, n_main)),
                  _const_spec((D, LORA_PAD)), _const_spec((LORA_PAD, dk)), _const_spec((1, dk))],
        out_specs=[_tok_spec(tm, dk), _tok_spec(tm, dk), _tok_spec(tm, dv), _tok_spec(tm, dv),
                   _tok_spec(tm, dk)],
        out_shape=[jax.ShapeDtypeStruct((B, S, w), F32) for w in (dk, dk, dv, dv, dk)],
        compiler_params=_params(),
        name="gla_proj",
    )(h, _row(g_pre), w_main, w_a, aup, _row(a_bias))

    chunks = 4 if S % (4 * GLA_CHUNK) == 0 else 1
    tc = chunks * GLA_CHUNK
    o = pl.pallas_call(
        functools.partial(_gla_chunk_kernel, chunks=chunks),
        grid=(B, S // tc),
        in_specs=[_tok_spec(tc, dk), _tok_spec(tc, dk), _tok_spec(tc, dv), _tok_spec(tc, dv),
                  _tok_spec(tc, dk), _const_spec((1, dv // GLA_HEADS))],
        out_specs=_tok_spec(tc, dv),
        out_shape=jax.ShapeDtypeStruct((B, S, dv), BF16),
        scratch_shapes=[pltpu.VMEM((GLA_HEADS, dv // GLA_HEADS, dk // GLA_HEADS), F32)],
        compiler_params=_params(),
        name="gla_chunk",
    )(q, k, v, r, la, _row(norm_w))
    return _post(h, o, w_o, g_post)


def _nsa_head_perm():
    perm = np.zeros((D_MODEL,), np.int32)
    for p in range(NSA_HPG):
        for g in range(NSA_KV_GROUPS):
            src = (g * NSA_HPG + p) * NSA_HEAD_DIM
            dst = p * LANES + g * NSA_HEAD_DIM
            perm[dst:dst + NSA_HEAD_DIM] = np.arange(src, src + NSA_HEAD_DIM)
    return perm


def _nsa_proj_kernel(h_ref, gn_ref, wq_ref, wkv_ref, wg_ref, q_ref, kv_ref, gates_ref):
    xn = _rms(h_ref[0], gn_ref[...], NORM_EPS).astype(BF16)
    q_ref[0] = _dot(xn, wq_ref[...]).astype(BF16)
    kv_ref[0] = _dot(xn, wkv_ref[...]).astype(BF16)
    gates_ref[0] = _sigmoid(_dot(xn, wg_ref[...]))


def _nsa_compress_kernel(r_ref, w1_ref, pe_ref, b1_ref, w2_ref, out_ref, *, n_cmp):
    r = r_ref[0, 0].astype(F32)
    rows = r.shape[0]
    pe = pe_ref[0]
    ra = (r + pe[0]).astype(BF16)
    rb = (pltpu.roll(r, rows - 1, 0) + pe[1]).astype(BF16)
    out = jnp.zeros((rows, LANES), F32)
    for g in range(NSA_KV_GROUPS):
        hid = _gelu_tanh(_dot(ra, w1_ref[0, g, 0]) + _dot(rb, w1_ref[0, g, 1]) + b1_ref[0])
        out = out + _dot(hid.astype(BF16), w2_ref[0, g])
    rowi = lax.broadcasted_iota(jnp.int32, out.shape, 0)
    out_ref[0, 0] = jnp.where(rowi < n_cmp, out, 0.0).astype(BF16)


def _group_queries(q, g):
    lane = lax.broadcasted_iota(jnp.int32, (q.shape[0], LANES), 1)
    keep = (lane < NSA_HEAD_DIM) if g == 0 else (lane >= NSA_HEAD_DIM)
    return [jnp.where(keep, q[:, p * LANES:(p + 1) * LANES], jnp.zeros((), q.dtype))
            for p in range(NSA_HPG)]


def _ones_in_other_group(v, g):
    lane = lax.broadcasted_iota(jnp.int32, v.shape, 1)
    other = (lane >= NSA_HEAD_DIM) if g == 0 else (lane < NSA_HEAD_DIM)
    return jnp.where(other, jnp.ones((), v.dtype), v)


def _normalize_groups(acc0, acc1):
    lane = lax.broadcasted_iota(jnp.int32, acc0.shape, 1)
    half = NSA_HEAD_DIM
    return jnp.where(lane < half, acc0 / pltpu.roll(acc0, half, 1), acc1 / pltpu.roll(acc1, half, 1))


def _nsa_cmp_kernel(q_ref, kc_ref, vc_ref, ov_ref, ocmp_ref, score_ref, *, n_cmp, n_slc):
    TQ = q_ref.shape[1]
    ncp = kc_ref.shape[1]
    nbp = ov_ref.shape[0]
    G = NSA_KV_GROUPS
    R = NSA_HPG * TQ
    CW = 2 * TQ
    n_cw = R // CW
    chains = range(G * n_cw)
    t0 = pl.program_id(1) * TQ
    q = q_ref[0]
    kc = kc_ref[0]
    vc = vc_ref[0]
    pos = t0 + lax.broadcasted_iota(jnp.int32, (1, TQ), 1)
    blk_c = lax.broadcasted_iota(jnp.int32, (ncp, 1), 0)
    vis = (blk_c * CMP_STRIDE + (CMP_BLOCK - 1) <= pos) & (blk_c < n_cmp)
    bias = jnp.where(vis, 0.0, NEG_INF)
    bias = jnp.concatenate([bias] * (CW // TQ), axis=1)
    any_vis = (pos >= CMP_BLOCK - 1).astype(F32)
    any_vis = jnp.concatenate([any_vis] * (CW // TQ), axis=1)
    qall = jnp.concatenate([t for g in range(G) for t in _group_queries(q, g)], axis=0)
    s = [_dot_nt(kc, qall[j * CW:(j + 1) * CW]) + bias for j in chains]
    e = [jnp.exp2(s[j] - jnp.max(s[j], axis=0, keepdims=True)) for j in chains]
    p = [e[j] * (any_vis / jnp.sum(e[j], axis=0, keepdims=True)) for j in chains]
    o = [_dot_tn(vc, p[j].astype(BF16)) for j in chains]

    blk_s = lax.broadcasted_iota(jnp.int32, (nbp, 1), 0)
    cur = lax.shift_right_logical(pos, 6)
    causal = (blk_s * SLC_BLOCK <= pos) & (blk_s < n_slc)
    forced = (blk_s == 0) | (blk_s == cur) | (blk_s == cur - 1)
    bonus = np.float32(FORCE_BONUS) * forced.astype(F32)
    for g in range(G):
        psum = jnp.zeros((ncp, TQ), F32)
        for j in range(g * n_cw, (g + 1) * n_cw):
            for c in range(CW // TQ):
                psum = psum + p[j][:, c * TQ:(c + 1) * TQ]
        hi, lo = _split2(psum)
        imp = _dot(ov_ref[...], hi) + _dot(ov_ref[...], lo)
        score_ref[0, g * nbp:(g + 1) * nbp, :] = jnp.where(causal, imp + bonus, NEG_INF)

    half = NSA_HEAD_DIM
    for p_i in range(NSA_HPG):
        j, c = divmod(p_i, CW // TQ)
        cols = slice(c * TQ, (c + 1) * TQ)
        o_pair = jnp.concatenate([o[j][:half, cols], o[n_cw + j][half:, cols]], axis=0)
        ocmp_ref[0, :, p_i * LANES:(p_i + 1) * LANES] = o_pair.T


def _nsa_topk_kernel(score_ref, sel_ref, *, top_n):
    G = NSA_KV_GROUPS
    nbp = score_ref.shape[1] // G
    sc0 = score_ref[0].reshape(G, nbp, score_ref.shape[2])
    row = lax.broadcasted_iota(jnp.int32, sc0.shape, 1).astype(F32)

    def body(_, carry):
        sc, sel = carry
        m = jnp.max(sc, axis=1, keepdims=True)
        idx = jnp.min(jnp.where(sc == m, row, np.float32(nbp)), axis=1, keepdims=True)
        hit = row == idx
        return jnp.where(hit, np.float32(-3e38), sc), jnp.where(hit, 1.0, sel)

    _, sel = lax.fori_loop(0, top_n, body, (sc0, jnp.zeros_like(sc0)))
    sel_ref[0] = sel.reshape(G * nbp, score_ref.shape[2]).astype(sel_ref.dtype)


def _nsa_attn_kernel(q_ref, ks_ref, vs_ref, kw_ref, vw_ref, sel_ref, ocmp_ref, gates_ref, ge_ref,
                     o_ref, m_ref, acc_ref, *, win_rows):
    TQ = q_ref.shape[1]
    S = ks_ref.shape[1]
    TK = min(NSA_TK, S)
    nbp = sel_ref.shape[1] // NSA_KV_GROUPS
    R = NSA_HPG * TQ
    CW = 2 * TQ
    n_cw = R // CW
    t0 = pl.program_id(1) * TQ
    q = q_ref[0]
    pos = t0 + lax.broadcasted_iota(jnp.int32, (1, TQ), 1)
    n_kt = (t0 + TQ + TK - 1) // TK
    blk_l = lax.broadcasted_iota(jnp.int32, (TK, nbp), 1)
    key_s = lax.broadcasted_iota(jnp.int32, (TK, 1), 0)
    w_start = pl.multiple_of(jnp.maximum(t0 - WINDOW, 0), TQ)
    wkey = w_start + lax.broadcasted_iota(jnp.int32, (win_rows, 1), 0)
    wbias = jnp.where((wkey <= pos) & (wkey > pos - WINDOW), 0.0, NEG_INF)
    wbias = jnp.concatenate([wbias] * (CW // TQ), axis=1)

    G = NSA_KV_GROUPS
    chains = range(G * n_cw)
    qall = jnp.concatenate([t for g in range(G) for t in _group_queries(q, g)], axis=0)

    def scores(k_tile, bias):
        return [_dot_nt(k_tile, qall[j * CW:(j + 1) * CW]) + bias[j // n_cw] for j in chains]

    m_ref[...] = jnp.full(m_ref.shape, NEG_INF, F32)
    acc_ref[...] = jnp.zeros_like(acc_ref)

    def body(kt, carry):
        k0 = pl.multiple_of(kt * TK, TK)
        expand = (blk_l == lax.shift_right_logical(k0 + key_s, 6)).astype(BF16)
        causal = k0 + key_s <= pos
        bias = []
        for g in range(G):
            chosen = _dot(expand, sel_ref[0, g * nbp:(g + 1) * nbp, :])
            b = jnp.where((chosen > 0.5) & causal, 0.0, NEG_INF)
            bias.append(jnp.concatenate([b] * (CW // TQ), axis=1))
        v_raw = vs_ref[0, pl.ds(k0, TK), :]
        v_tile = [_ones_in_other_group(v_raw, g) for g in range(G)]
        s = scores(ks_ref[0, pl.ds(k0, TK), :], bias)
        for g in range(G):
            grp = range(g * n_cw, (g + 1) * n_cw)
            m_old = {j: m_ref[:, j * CW:(j + 1) * CW] for j in grp}
            m_new = {j: jnp.maximum(m_old[j], jnp.max(s[j], axis=0, keepdims=True)) for j in grp}
            p = {j: jnp.exp2(s[j] - m_new[j]).astype(BF16) for j in grp}
            pv = {j: _dot_tn(v_tile[g], p[j]) for j in grp}
            for j in grp:
                cols = slice(j * CW, (j + 1) * CW)
                acc_ref[:, cols] = acc_ref[:, cols] * jnp.exp2(m_old[j] - m_new[j]) + pv[j]
                m_ref[:, cols] = m_new[j]
        return carry

    lax.fori_loop(0, n_kt, body, 0)

    vw_raw = vw_ref[0, pl.ds(w_start, win_rows), :]
    vw = [_ones_in_other_group(vw_raw, g) for g in range(G)]
    s = scores(kw_ref[0, pl.ds(w_start, win_rows), :], [wbias] * G)
    p = [jnp.exp2(s[j] - jnp.max(s[j], axis=0, keepdims=True)).astype(BF16) for j in chains]
    acc_win = jnp.concatenate([_dot_tn(vw[j // n_cw], p[j]) for j in chains], axis=1)

    half = NSA_HEAD_DIM
    normed = {}
    for name, a in (("slc", acc_ref[...]), ("win", acc_win)):
        a0, a1 = a[:, :R], a[:, R:]
        normed[name] = jnp.concatenate([a0[:half] / a0[half:half + 1], a1[half:] / a1[0:1]], axis=0)
    gates = gates_ref[0]
    g_cmp = _dot_hilo(gates, ge_ref[0])
    g_slc = _dot_hilo(gates, ge_ref[1])
    g_win = _dot_hilo(gates, ge_ref[2])
    for p_i in range(NSA_HPG):
        cols = slice(p_i * LANES, (p_i + 1) * LANES)
        slc = normed["slc"][:, p_i * TQ:(p_i + 1) * TQ].T
        win = normed["win"][:, p_i * TQ:(p_i + 1) * TQ].T
        o_ref[0, :, cols] = (g_cmp[:, cols] * ocmp_ref[0, :, cols] + g_slc[:, cols] * slc
                             + g_win[:, cols] * win).astype(o_ref.dtype)


def _nsa_layer(h, g_pre, g_post, w_in, cmp_w1, cmp_b1, cmp_w2, cmp_pe, w_o):
    B, S, D = h.shape
    H, G, dh, hpg = NSA_HEADS, NSA_KV_GROUPS, NSA_HEAD_DIM, NSA_HPG
    perm = _nsa_head_perm()
    n_q = H * dh
    n_kv = 6 * G * dh
    wq = (w_in[:, :n_q] * np.float32(dh ** -0.5 * np.log2(np.e)))[:, perm].astype(BF16)
    wkv = w_in[:, n_q:n_q + n_kv].astype(BF16)
    wgt = _pad_cols(w_in[:, n_q + n_kv:], LANES).astype(BF16)
    tm = min(PROJ_TM, S)
    q, kv, gates = pl.pallas_call(
        _nsa_proj_kernel,
        grid=(B, S // tm),
        in_specs=[_tok_spec(tm, D), _const_spec((1, D)), _const_spec((D, n_q)),
                  _const_spec((D, n_kv)), _const_spec((D, LANES))],
        out_specs=[_tok_spec(tm, n_q), _tok_spec(tm, n_kv), _tok_spec(tm, LANES)],
        out_shape=[jax.ShapeDtypeStruct((B, S, n_q), BF16), jax.ShapeDtypeStruct((B, S, n_kv), BF16),
                   jax.ShapeDtypeStruct((B, S, LANES), F32)],
        compiler_params=_params(),
        name="nsa_proj",
    )(h, _row(g_pre), wq, wkv, wgt)

    n_cmp = (S - CMP_BLOCK) // CMP_STRIDE + 1
    ncp = S // CMP_STRIDE
    half = CMP_BLOCK // 2
    assert CMP_STRIDE == half
    rk = jnp.stack([kv[:, :, :LANES], kv[:, :, LANES:2 * LANES]]).reshape(2, B, ncp, half * LANES)
    w1 = cmp_w1.reshape(2, 2, half, dh, CMP_HIDDEN)
    w1x = jnp.zeros((2, G, 2, half, G, dh, CMP_HIDDEN), F32)
    for g in range(G):
        w1x = w1x.at[:, g, :, :, g].set(w1)
    w1x = w1x.reshape(2, G, 2, half * LANES, CMP_HIDDEN).astype(BF16)
    pe = jnp.broadcast_to(cmp_pe.reshape(2, 2, half, 1, dh), (2, 2, half, G, dh))
    pe = pe.reshape(2, 2, 1, half * LANES).astype(F32)
    w2x = jnp.zeros((2, G, CMP_HIDDEN, G, dh), F32)
    for g in range(G):
        w2x = w2x.at[:, g, :, g].set(cmp_w2)
    w2x = w2x.reshape(2, G, CMP_HIDDEN, LANES).astype(BF16)
    kvc = pl.pallas_call(
        functools.partial(_nsa_compress_kernel, n_cmp=n_cmp),
        grid=(2, B),
        in_specs=[pl.BlockSpec((1, 1, ncp, half * LANES), lambda c, b: (c, b, 0, 0)),
                  pl.BlockSpec((1, G, 2, half * LANES, CMP_HIDDEN), lambda c, b: (c, 0, 0, 0, 0)),
                  pl.BlockSpec((1, 2, 1, half * LANES), lambda c, b: (c, 0, 0, 0)),
                  pl.BlockSpec((1, 1, CMP_HIDDEN), lambda c, b: (c, 0, 0)),
                  pl.BlockSpec((1, G, CMP_HIDDEN, LANES), lambda c, b: (c, 0, 0, 0))],
        out_specs=pl.BlockSpec((1, 1, ncp, LANES), lambda c, b: (c, b, 0, 0)),
        out_shape=jax.ShapeDtypeStruct((2, B, ncp, LANES), BF16),
        compiler_params=_params(),
        name="nsa_compress",
    )(rk, w1x, pe, cmp_b1.reshape(2, 1, CMP_HIDDEN).astype(F32), w2x)

    n_slc = S // SLC_BLOCK
    top_n = min(SLC_TOPK, n_slc)
    nbp = max(LANES, n_slc)
    start = np.arange(ncp) * CMP_STRIDE
    end = start + CMP_BLOCK - 1
    s_start = np.arange(nbp) * SLC_BLOCK
    s_end = s_start + SLC_BLOCK - 1
    overlap = ((end[:, None] >= s_start[None, :]) & (start[:, None] <= s_end[None, :])
               & (np.arange(ncp)[:, None] < n_cmp) & (np.arange(nbp)[None, :] < n_slc))
    overlap_t = jnp.asarray(overlap.astype(np.float32).T, BF16)
    TQ = min(NSA_TQ, S)
    blk_tok_spec = lambda t: pl.BlockSpec((1, G * nbp, t), lambda b, i: (b, 0, i))
    o_cmp, score = pl.pallas_call(
        functools.partial(_nsa_cmp_kernel, n_cmp=n_cmp, n_slc=n_slc),
        grid=(B, S // TQ),
        in_specs=[_tok_spec(TQ, n_q),
                  pl.BlockSpec((None, 1, ncp, LANES), lambda b, i: (0, b, 0, 0)),
                  pl.BlockSpec((None, 1, ncp, LANES), lambda b, i: (1, b, 0, 0)),
                  _const_spec((nbp, ncp))],
        out_specs=[_tok_spec(TQ, n_q), blk_tok_spec(TQ)],
        out_shape=[jax.ShapeDtypeStruct((B, S, n_q), F32),
                   jax.ShapeDtypeStruct((B, G * nbp, S), F32)],
        compiler_params=_params(),
        name="nsa_compressed_attn",
    )(q, kvc, kvc, overlap_t)
    tr = min(NSA_TOPK_TOKENS, S)
    sel = pl.pallas_call(
        functools.partial(_nsa_topk_kernel, top_n=top_n),
        grid=(B, S // tr),
        in_specs=[blk_tok_spec(tr)],
        out_specs=blk_tok_spec(tr),
        out_shape=jax.ShapeDtypeStruct((B, G * nbp, S), BF16),
        compiler_params=_params(),
        name="nsa_topk",
    )(score)

    ge = np.zeros((3, LANES, n_q), np.float32)
    for g in range(G):
        for p in range(hpg):
            for j in range(3):
                ge[j, g * hpg * 3 + p * 3 + j, p * LANES + g * dh:p * LANES + (g + 1) * dh] = 1.0
    ge = jnp.asarray(ge, BF16)
    win_rows = min(WINDOW + TQ, S)
    kv_spec = lambda c: pl.BlockSpec((1, S, LANES), lambda b, i, c=c: (b, 0, c))
    o = pl.pallas_call(
        functools.partial(_nsa_attn_kernel, win_rows=win_rows),
        grid=(B, S // TQ),
        in_specs=[_tok_spec(TQ, n_q), kv_spec(2), kv_spec(3), kv_spec(4), kv_spec(5),
                  blk_tok_spec(TQ), _tok_spec(TQ, n_q), _tok_spec(TQ, LANES),
                  _const_spec((3, LANES, n_q))],
        out_specs=_tok_spec(TQ, n_q),
        out_shape=jax.ShapeDtypeStruct((B, S, n_q), BF16),
        scratch_shapes=[pltpu.VMEM((1, G * hpg * TQ), F32), pltpu.VMEM((LANES, G * hpg * TQ), F32)],
        compiler_params=_params(),
        name="nsa_selected_window_attn",
    )(q, kv, kv, kv, kv, sel, o_cmp, gates, ge)
    return _post(h, o, w_o[perm, :], g_post)


def kernel(x, norm_g, ffn_w_in, ffn_conv_w, ffn_conv_b, ffn_w_out, rwkv_mu, rwkv_w_rkv, rwkv_w0, rwkv_w1, rwkv_w2, rwkv_a0, rwkv_a1, rwkv_a2, rwkv_g1, rwkv_g2, rwkv_k_k, rwkv_k_a, rwkv_r_k, rwkv_ln_w, rwkv_ln_b, rwkv_w_o, rwkv_v0, rwkv_v1, rwkv_v2, nsa_w_in, nsa_cmp_w1, nsa_cmp_b1, nsa_cmp_w2, nsa_cmp_pe, nsa_w_o, gla_w_in, gla_a_up, gla_a_bias, gla_norm_w, gla_w_o):
    depth = norm_g.shape[0]
    h = x
    v_first = None
    for i in range(depth):
        kind, j = i % 3, i // 3
        if kind == 0:
            vres = None if j == 0 else (rwkv_v0[j - 1], rwkv_v1[j - 1], rwkv_v2[j - 1])
            h, v_first = _rwkv_layer(
                h, norm_g[i, 0], norm_g[i, 1], rwkv_mu[j], rwkv_w_rkv[j], rwkv_w0[j], rwkv_w1[j],
                rwkv_w2[j], rwkv_a0[j], rwkv_a1[j], rwkv_a2[j], rwkv_g1[j], rwkv_g2[j], rwkv_k_k[j],
                rwkv_k_a[j], rwkv_r_k[j], rwkv_ln_w[j], rwkv_ln_b[j], rwkv_w_o[j], v_first, vres)
        elif kind == 1:
            h = _nsa_layer(h, norm_g[i, 0], norm_g[i, 1], nsa_w_in[j], nsa_cmp_w1[j], nsa_cmp_b1[j],
                           nsa_cmp_w2[j], nsa_cmp_pe[j], nsa_w_o[j])
        else:
            h = _gla_layer(h, norm_g[i, 0], norm_g[i, 1], gla_w_in[j], gla_a_up[j], gla_a_bias[j],
                           gla_norm_w[j], gla_w_o[j])
        h = _ffn(h, norm_g[i, 2], ffn_w_in[i], ffn_conv_w[i], ffn_conv_b[i], ffn_w_out[i],
                 norm_g[i, 3])
    return h
```

```python
import functools

import numpy as np
import jax
import jax.numpy as jnp
from jax import lax
from jax.experimental import pallas as pl
from jax.experimental.pallas import tpu as pltpu

F32 = jnp.float32
BF16 = jnp.bfloat16

V7X_VMEM_BYTES = 64 * 1024 * 1024
VMEM_LIMIT_BYTES = V7X_VMEM_BYTES - 8 * 1024 * 1024
LANES = 128
SUBLANES = 8

D_MODEL = 1024
NORM_EPS = 1e-6
NEG_INF = -1e30

RWKV_HEAD_DIM = 64
RWKV_HEADS = D_MODEL // RWKV_HEAD_DIM
RWKV_GN_EPS = 64e-5
RWKV_CHUNK = 64
RWKV_CHUNKS_PER_STEP = 2
LORA_PAD = 128

NSA_HEADS = 16
NSA_KV_GROUPS = 2
NSA_HEAD_DIM = D_MODEL // NSA_HEADS
NSA_HPG = NSA_HEADS // NSA_KV_GROUPS
CMP_BLOCK = 32
CMP_STRIDE = 16
CMP_HIDDEN = 256
CMP_ROW_PARTS = 4
SLC_BLOCK = 64
SLC_TOPK = 16
WINDOW = 512
FORCE_BONUS = 1e4
NSA_TQ = 128
NSA_TK = 512
NSA_TOPK_TOKENS = 512

GLA_HEADS = 4
GLA_KEY_DIM = D_MODEL // 2
GLA_VAL_DIM = D_MODEL
GLA_GATE_LORA = 16
GLA_TAU = 16.0
GLA_CHUNK = 64
GLA_NORM_EPS = 1e-5

D_FF = 2816
FFN_CHUNK = 256
FFN_TM = 512
PROJ_TM = 256
POST_TM = 512


def _dot(a, b):
    return jnp.dot(a, b, preferred_element_type=F32)


def _dot_nt(a, b):
    return lax.dot_general(a, b, (((1,), (1,)), ((), ())), preferred_element_type=F32)


def _dot_tn(a, b):
    return lax.dot_general(a, b, (((0,), (0,)), ((), ())), preferred_element_type=F32)


def _split2(x):
    hi = x.astype(BF16)
    lo = (x - hi.astype(F32)).astype(BF16)
    return hi, lo


def _dot_hilo(x, m):
    hi, lo = _split2(x)
    return _dot(hi, m) + _dot(lo, m)


def _dot_tri3(tri, x):
    hi = x.astype(BF16)
    r1 = x - hi.astype(F32)
    mid = r1.astype(BF16)
    lo = (r1 - mid.astype(F32)).astype(BF16)
    return _dot(tri, hi) + _dot(tri, mid) + _dot(tri, lo)


def _head_sum(x, red_ref, exp_ref):
    return _dot_hilo(_dot(x.astype(BF16), red_ref[...]), exp_ref[...])


def _rms(x, g, eps):
    return x * lax.rsqrt(jnp.mean(x * x, axis=-1, keepdims=True) + eps) * g


def _sigmoid(x):
    return 1.0 / (1.0 + jnp.exp(-x))


def _softplus(x):
    return jnp.maximum(x, 0.0) + jnp.log(1.0 + jnp.exp(-jnp.abs(x)))


def _gelu_tanh(x):
    c = np.float32(np.sqrt(2.0 / np.pi))
    return 0.5 * x * (1.0 + jnp.tanh(c * (x + 0.044715 * (x * x * x))))


def _shift_rows(x, n, prev8):
    rolled = pltpu.roll(x, n, 0)
    row = lax.broadcasted_iota(jnp.int32, x.shape, 0)
    for j in range(n):
        rolled = jnp.where(row == j, prev8[SUBLANES - n + j:SUBLANES - n + j + 1, :], rolled)
    return rolled


def _const_spec(shape):
    nd = len(shape)
    return pl.BlockSpec(shape, lambda *_: (0,) * nd, pipeline_mode=pl.Buffered(1))


def _tok_spec(tm, width):
    return pl.BlockSpec((1, tm, width), lambda b, i: (b, i, 0))


def _params():
    return pltpu.CompilerParams(dimension_semantics=("arbitrary", "arbitrary"),
                                vmem_limit_bytes=VMEM_LIMIT_BYTES)


def _row(v):
    return v.reshape(1, -1).astype(F32)


def _ffn_kernel(h_ref, gpre_ref, win_ref, cw_ref, cb_ref, wo_ref, gpost_ref, out_ref,
                xn_ref, acc_ref, halo_ref, *, fc):
    @pl.when(pl.program_id(1) == 0)
    def _():
        halo_ref[...] = jnp.zeros_like(halo_ref)

    h = h_ref[0]
    xn_ref[...] = _rms(h, gpre_ref[...], NORM_EPS).astype(BF16)
    acc_ref[...] = jnp.zeros_like(acc_ref)
    tm = h.shape[0]
    F = wo_ref.shape[0]
    nc = F // fc

    def gate_up(c):
        xn = xn_ref[...]
        return (_dot(xn, win_ref[:, c * fc:(c + 1) * fc]),
                _dot(xn, win_ref[:, F + c * fc:F + (c + 1) * fc]))

    nxt = gate_up(0)
    for c in range(nc):
        cols = slice(c * fc, (c + 1) * fc)
        gate, up = nxt
        if c + 1 < nc:
            nxt = gate_up(c + 1)
        prev = halo_ref[:, cols]
        g1 = _shift_rows(gate, 1, prev)
        g2 = _shift_rows(gate, 2, prev)
        cw = cw_ref[:, cols]
        z = cw[2:3] * gate + cw[1:2] * g1 + cw[0:1] * g2 + cb_ref[:, cols]
        halo_ref[:, cols] = gate[tm - SUBLANES:tm, :]
        act = (z * _sigmoid(z) * up).astype(BF16)
        acc_ref[...] += _dot(act, wo_ref[cols, :])
    out_ref[0] = h + _rms(acc_ref[...], gpost_ref[...], NORM_EPS)


def _ffn(h, g_pre, w_in, conv_w, conv_b, w_out, g_post):
    B, S, D = h.shape
    F = w_out.shape[0]
    assert F % FFN_CHUNK == 0
    tm = min(FFN_TM, S)
    return pl.pallas_call(
        functools.partial(_ffn_kernel, fc=FFN_CHUNK),
        grid=(B, S // tm),
        in_specs=[_tok_spec(tm, D), _const_spec((1, D)), _const_spec((D, 2 * F)),
                  _const_spec((3, F)), _const_spec((1, F)), _const_spec((F, D)),
                  _const_spec((1, D))],
        out_specs=_tok_spec(tm, D),
        out_shape=jax.ShapeDtypeStruct((B, S, D), F32),
        scratch_shapes=[pltpu.VMEM((tm, D), BF16), pltpu.VMEM((tm, D), F32),
                        pltpu.VMEM((SUBLANES, F), F32)],
        compiler_params=_params(),
        name="conv_ffn",
    )(h, _row(g_pre), w_in.astype(BF16), conv_w.astype(F32), _row(conv_b), w_out.astype(BF16),
      _row(g_post))


def _post_kernel(h_ref, o_ref, wo_ref, g_ref, out_ref):
    y = _dot(o_ref[0].astype(BF16), wo_ref[...])
    out_ref[0] = h_ref[0] + _rms(y, g_ref[...], NORM_EPS)


def _post(h, o, w_o, g_post):
    B, S, D = h.shape
    K = o.shape[-1]
    tm = min(POST_TM, S)
    return pl.pallas_call(
        _post_kernel,
        grid=(B, S // tm),
        in_specs=[_tok_spec(tm, D), _tok_spec(tm, K), _const_spec((K, D)), _const_spec((1, D))],
        out_specs=_tok_spec(tm, D),
        out_shape=jax.ShapeDtypeStruct((B, S, D), F32),
        compiler_params=_params(),
        name="mixer_out",
    )(h, o, w_o.astype(BF16), _row(g_post))


def _rwkv_proj_kernel(*refs, has_vres):
    if has_vres:
        (h_ref, gn_ref, mu_ref, wr_ref, wk_ref, wv_ref, w1_ref, w2_ref, a1_ref, a2_ref, g1_ref,
         g2_ref, vec_ref, red_ref, exp_ref, vf_ref, v1_ref, v2_ref,
         r_ref, ld_ref, k_ref, v_ref, kkn_ref, b_ref, g_ref, bonus_ref, carry_ref) = refs
    else:
        (h_ref, gn_ref, mu_ref, wr_ref, wk_ref, wv_ref, w1_ref, w2_ref, a1_ref, a2_ref, g1_ref,
         g2_ref, vec_ref, red_ref, exp_ref,
         r_ref, ld_ref, k_ref, v_ref, kkn_ref, b_ref, g_ref, bonus_ref, carry_ref) = refs

    @pl.when(pl.program_id(1) == 0)
    def _():
        carry_ref[...] = jnp.zeros_like(carry_ref)

    x = _rms(h_ref[0], gn_ref[...], NORM_EPS)
    tm = x.shape[0]
    xx = _shift_rows(x, 1, carry_ref[...]) - x
    carry_ref[...] = x[tm - SUBLANES:tm, :]
    mu = mu_ref[...]
    xr, xw, xk, xv, xa, xg = ((x + xx * mu[c:c + 1]).astype(BF16) for c in range(6))
    vec = vec_ref[...]
    w0, a0, k_k, k_a, r_k, v0 = (vec[c:c + 1] for c in range(6))

    r = _dot(xr, wr_ref[...])
    k = _dot(xk, wk_ref[...])
    v = _dot(xv, wv_ref[...])
    wl = -_softplus(-(w0 + _dot(jnp.tanh(_dot(xw, w1_ref[...])).astype(BF16), w2_ref[...]))) - 0.5
    ld_ref[0] = -jnp.exp(wl)
    a = _sigmoid(a0 + _dot(_dot(xa, a1_ref[...]).astype(BF16), a2_ref[...]))
    g_ref[0] = _dot(_sigmoid(_dot(xg, g1_ref[...])).astype(BF16), g2_ref[...])
    if has_vres:
        mix = _sigmoid(v0 + _dot(_dot(xv, v1_ref[...]).astype(BF16), v2_ref[...]))
        v = v + (vf_ref[0] - v) * mix
    kk = k * k_k
    norm = jnp.sqrt(_head_sum(kk * kk, red_ref, exp_ref))
    kkn = kk / jnp.maximum(norm, 1e-12)
    k = k * (1.0 + (a - 1.0) * k_a)
    r_ref[0] = r
    k_ref[0] = k
    v_ref[0] = v
    kkn_ref[0] = kkn
    b_ref[0] = kkn * a
    bonus_ref[0] = _head_sum(r * k * r_k, red_ref, exp_ref) * v


def _pad_cols(w, n):
    return jnp.pad(w, ((0, 0), (0, n - w.shape[1])))


def _pad_rows(w, n):
    return jnp.pad(w, ((0, n - w.shape[0]), (0, 0)))


def _head_reduce_expand(d, hd, value):
    red = (np.arange(d)[:, None] // hd == np.arange(LANES)[None, :]).astype(np.float32)
    return jnp.asarray(red, BF16), jnp.asarray(red.T * value, BF16)


def _rwkv_proj(h, g_norm, mu, w_rkv, w0, w1, w2, a0, a1, a2, g1, g2, k_k, k_a, r_k, v_first, vres):
    B, S, D = h.shape
    tm = min(PROJ_TM, S)
    has_vres = vres is not None
    v0 = vres[0] if has_vres else jnp.zeros((D,), F32)
    vec = jnp.stack([w0, a0, k_k, k_a, r_k.reshape(-1), v0]).astype(F32)
    red, exp = _head_reduce_expand(D, RWKV_HEAD_DIM, 1.0)
    lora_in = lambda w: _pad_cols(w, LORA_PAD).astype(BF16)
    lora_out = lambda w: _pad_rows(w, LORA_PAD).astype(BF16)
    args = [h, _row(g_norm), mu, w_rkv[0].astype(BF16), w_rkv[1].astype(BF16), w_rkv[2].astype(BF16),
            lora_in(w1), lora_out(w2), lora_in(a1), lora_out(a2), lora_in(g1), lora_out(g2), vec, red, exp]
    specs = [_tok_spec(tm, D), _const_spec((1, D)), _const_spec((6, D)), _const_spec((D, D)),
             _const_spec((D, D)), _const_spec((D, D)), _const_spec((D, LORA_PAD)),
             _const_spec((LORA_PAD, D)), _const_spec((D, LORA_PAD)), _const_spec((LORA_PAD, D)),
             _const_spec((D, LORA_PAD)), _const_spec((LORA_PAD, D)), _const_spec((6, D)),
             _const_spec((D, LANES)), _const_spec((LANES, D))]
    if has_vres:
        args += [v_first, lora_in(vres[1]), lora_out(vres[2])]
        specs += [_tok_spec(tm, D), _const_spec((D, LORA_PAD)), _const_spec((LORA_PAD, D))]
    outs = pl.pallas_call(
        functools.partial(_rwkv_proj_kernel, has_vres=has_vres),
        grid=(B, S // tm),
        in_specs=specs,
        out_specs=[_tok_spec(tm, D)] * 8,
        out_shape=[jax.ShapeDtypeStruct((B, S, D), F32)] * 8,
        scratch_shapes=[pltpu.VMEM((SUBLANES, D), F32)],
        compiler_params=_params(),
        name="rwkv_proj",
    )(*args)
    return outs


def _rwkv_scan_kernel(r_ref, ld_ref, k_ref, v_ref, kkn_ref, b_ref, y_ref, st_ref):
    @pl.when(pl.program_id(1) == 0)
    def _():
        st_ref[...] = jnp.zeros_like(st_ref)

    C = RWKV_CHUNK
    N = RWKV_HEAD_DIM
    assert 2 * N == LANES and C == N
    rowi = lax.broadcasted_iota(jnp.int32, (C, C), 0)
    coli = lax.broadcasted_iota(jnp.int32, (C, C), 1)
    tri = (rowi >= coli).astype(BF16)
    prow = lax.broadcasted_iota(jnp.int32, (C, LANES), 0)
    pcol = lax.broadcasted_iota(jnp.int32, (C, LANES), 1) & (N - 1)
    low_incl = prow >= pcol
    low_strict = prow > pcol
    eye = (prow == pcol).astype(F32)
    srow = lax.broadcasted_iota(jnp.int32, (LANES, LANES), 0)
    scol = lax.broadcasted_iota(jnp.int32, (LANES, LANES), 1)
    same_head = (srow < N) == (scol < N)

    def bdiag(x):
        lane = lax.broadcasted_iota(jnp.int32, x.shape, 1)
        zero = jnp.zeros((), x.dtype)
        return jnp.concatenate([jnp.where(lane < N, x, zero), jnp.where(lane >= N, x, zero)], axis=0)

    n_chunks = r_ref.shape[1] // C
    n_pairs = RWKV_HEADS // 2
    sls = [slice(p * LANES, (p + 1) * LANES) for p in range(n_pairs)]
    a_t, r_t, b_t, k_t, b_h, k_h, vb, p_all = ([] for _ in range(8))
    for ci in range(n_chunks):
        rows = pl.ds(ci * C, C)
        ld = ld_ref[0, rows, :]
        cum = _dot_tri3(tri, ld)
        cum_last = cum[C - 1:C, :]
        p_inv = jnp.exp(-cum)
        p_rem = jnp.exp(cum_last - cum)
        p_all.append(jnp.exp(cum_last))
        b = b_ref[0, rows, :]
        k = k_ref[0, rows, :]
        a_t.append((-kkn_ref[0, rows, :] * jnp.exp(cum - ld)).astype(BF16))
        r_t.append((r_ref[0, rows, :] * jnp.exp(cum)).astype(BF16))
        b_t.append((b * p_inv).astype(BF16))
        k_t.append((k * p_inv).astype(BF16))
        b_h.append((b * p_rem).astype(BF16))
        k_h.append((k * p_rem).astype(BF16))
        vb.append(v_ref[0, rows, :].astype(BF16))

    units = [(ci, p) for ci in range(n_chunks) for p in range(n_pairs)]
    idx = range(len(units))
    tile = lambda arr, u: arr[units[u][0]][:, sls[units[u][1]]]
    ar = [jnp.concatenate([tile(a_t, u), tile(r_t, u)], axis=0) for u in idx]
    s_b = [_dot_nt(ar[u], bdiag(tile(b_t, u))) for u in idx]
    s_k = [_dot_nt(ar[u], bdiag(tile(k_t, u))) for u in idx]
    a_ab = [jnp.where(low_strict, s_b[u][:C], 0.0) for u in idx]
    a_rb = [jnp.where(low_incl, s_b[u][C:], 0.0).astype(BF16) for u in idx]
    a_ak = [jnp.where(low_strict, s_k[u][:C], 0.0).astype(BF16) for u in idx]
    a_rk = [jnp.where(low_incl, s_k[u][C:], 0.0).astype(BF16) for u in idx]
    av = [_dot(jnp.concatenate([a_ak[u], a_rk[u]], axis=0), bdiag(tile(vb, u))) for u in idx]
    akv = [av[u][:C].astype(BF16) for u in idx]
    pw = [a_ab[u].astype(BF16) for u in idx]
    tinv = [eye + a_ab[u] for u in idx]
    pw = [_dot(pw[u], bdiag(pw[u])).astype(BF16) for u in idx]
    span = 4
    while span < C:
        both = [_dot(jnp.concatenate([pw[u], tinv[u].astype(BF16)], axis=0), bdiag(pw[u]))
                for u in idx]
        tinv = [tinv[u] + both[u][C:] for u in idx]
        pw = [both[u][:C].astype(BF16) for u in idx]
        span *= 2
    tinv = [tinv[u] + _dot(tinv[u].astype(BF16), bdiag(pw[u])) for u in idx]
    tb = [tinv[u].astype(BF16) for u in idx]
    a_hat = [_dot(tb[u], bdiag(tile(a_t, u))).astype(BF16) for u in idx]
    u_loc = [_dot(tb[u], bdiag(akv[u])) for u in idx]

    pairs = range(n_pairs)
    st = [st_ref[p] for p in pairs]
    for ci in range(n_chunks):
        u0 = ci * n_pairs
        on_state = [_dot_nt(jnp.concatenate([a_hat[u0 + p], r_t[ci][:, sls[p]]], axis=0),
                            st[p].astype(BF16)) for p in pairs]
        ub = [(on_state[p][:C] + u_loc[u0 + p]).astype(BF16) for p in pairs]
        for p in pairs:
            y_ref[0, pl.ds(ci * C, C), sls[p]] = (on_state[p][C:] + _dot(a_rb[u0 + p], bdiag(ub[p]))
                                                  + av[u0 + p][C:])
        upd = [_dot_tn(jnp.concatenate([ub[p], vb[ci][:, sls[p]]], axis=0),
                       jnp.concatenate([b_h[ci][:, sls[p]], k_h[ci][:, sls[p]]], axis=0))
               for p in pairs]
        st = [st[p] * p_all[ci][:, sls[p]] + jnp.where(same_head, upd[p], 0.0) for p in pairs]
    for p in pairs:
        st_ref[p] = st[p]


def _rwkv_scan(r, ld, k, v, kkn, b):
    B, S, D = r.shape
    C = RWKV_CHUNK * RWKV_CHUNKS_PER_STEP
    return pl.pallas_call(
        _rwkv_scan_kernel,
        grid=(B, S // C),
        in_specs=[_tok_spec(C, D)] * 6,
        out_specs=_tok_spec(C, D),
        out_shape=jax.ShapeDtypeStruct((B, S, D), F32),
        scratch_shapes=[pltpu.VMEM((RWKV_HEADS // 2, LANES, LANES), F32)],
        compiler_params=_params(),
        name="rwkv_scan",
    )(r, ld, k, v, kkn, b)


def _rwkv_post_kernel(h_ref, y_ref, bonus_ref, g_ref, vec_ref, red_ref, avg_ref, wo_ref, gpost_ref,
                      out_ref):
    y = y_ref[0]
    d = y - _head_sum(y, red_ref, avg_ref)
    var = _head_sum(d * d, red_ref, avg_ref)
    vec = vec_ref[...]
    yn = d * lax.rsqrt(var + RWKV_GN_EPS) * vec[0:1] + vec[1:2]
    o = ((yn + bonus_ref[0]) * g_ref[0]).astype(BF16)
    out_ref[0] = h_ref[0] + _rms(_dot(o, wo_ref[...]), gpost_ref[...], NORM_EPS)


def _rwkv_post(h, y, bonus, g, ln_w, ln_b, w_o, g_post):
    B, S, D = h.shape
    tm = min(PROJ_TM, S)
    vec = jnp.stack([ln_w, ln_b]).astype(F32)
    red, avg = _head_reduce_expand(D, RWKV_HEAD_DIM, 1.0 / RWKV_HEAD_DIM)
    return pl.pallas_call(
        _rwkv_post_kernel,
        grid=(B, S // tm),
        in_specs=[_tok_spec(tm, D)] * 4 + [_const_spec((2, D)), _const_spec((D, LANES)),
                                           _const_spec((LANES, D)), _const_spec((D, D)),
                                           _const_spec((1, D))],
        out_specs=_tok_spec(tm, D),
        out_shape=jax.ShapeDtypeStruct((B, S, D), F32),
        compiler_params=_params(),
        name="rwkv_out",
    )(h, y, bonus, g, vec, red, avg, w_o.astype(BF16), _row(g_post))


def _rwkv_layer(h, g_pre, g_post, mu, w_rkv, w0, w1, w2, a0, a1, a2, g1, g2, k_k, k_a, r_k,
                ln_w, ln_b, w_o, v_first, vres):
    r, ld, k, v, kkn, b, g, bonus = _rwkv_proj(h, g_pre, mu, w_rkv, w0, w1, w2, a0, a1, a2, g1, g2,
                                               k_k, k_a, r_k, v_first, vres)
    if vres is None:
        v_first = v
    y = _rwkv_scan(r, ld, k, v, kkn, b)
    return _rwkv_post(h, y, bonus, g, ln_w, ln_b, w_o, g_post), v_first


def _gla_proj_kernel(h_ref, gn_ref, w_ref, wa_ref, aup_ref, abias_ref,
                     q_ref, k_ref, v_ref, r_ref, la_ref):
    xn = _rms(h_ref[0], gn_ref[...], NORM_EPS).astype(BF16)
    qkvr = _dot(xn, w_ref[...])
    dk = GLA_KEY_DIM
    dv = GLA_VAL_DIM
    q_ref[0] = qkvr[:, :dk] * np.float32((dk // GLA_HEADS) ** -0.5)
    k_ref[0] = qkvr[:, dk:2 * dk]
    v_ref[0] = qkvr[:, 2 * dk:2 * dk + dv]
    r_ref[0] = qkvr[:, 2 * dk + dv:]
    z = _dot(_dot(xn, wa_ref[...]).astype(BF16), aup_ref[...]) + abias_ref[...]
    la_ref[0] = -_softplus(-z) * np.float32(1.0 / GLA_TAU)


def _gla_chunk_kernel(q_ref, k_ref, v_ref, r_ref, la_ref, nw_ref, o_ref, st_ref, *, chunks):
    @pl.when(pl.program_id(1) == 0)
    def _():
        st_ref[...] = jnp.zeros_like(st_ref)

    C = GLA_CHUNK
    dk = GLA_KEY_DIM // GLA_HEADS
    dv = GLA_VAL_DIM // GLA_HEADS
    rowi = lax.broadcasted_iota(jnp.int32, (C, C), 0)
    coli = lax.broadcasted_iota(jnp.int32, (C, C), 1)
    low_incl = rowi >= coli
    tri = low_incl.astype(BF16)
    nw = nw_ref[...]

    for ci in range(chunks):
        rows = pl.ds(ci * C, C)
        bcum = _dot_tri3(tri, la_ref[0, rows, :])
        b_last = bcum[C - 1:C, :]
        k = k_ref[0, rows, :]
        qe = (q_ref[0, rows, :] * jnp.exp(bcum)).astype(BF16)
        ke = (k * jnp.exp(-bcum)).astype(BF16)
        kd = (k * jnp.exp(b_last - bcum)).astype(BF16)
        dec = jnp.exp(b_last)
        vb = v_ref[0, rows, :].astype(BF16)
        r = r_ref[0, rows, :]
        for h in range(GLA_HEADS):
            ks = slice(h * dk, (h + 1) * dk)
            vs = slice(h * dv, (h + 1) * dv)
            a = jnp.where(low_incl, _dot_nt(qe[:, ks], ke[:, ks]), 0.0)
            st = st_ref[h]
            o = _dot(a.astype(BF16), vb[:, vs]) + _dot_nt(qe[:, ks], st.astype(BF16))
            st_ref[h] = st * dec[:, ks] + _dot_tn(vb[:, vs], kd[:, ks])
            o = o * lax.rsqrt(jnp.mean(o * o, axis=-1, keepdims=True) + GLA_NORM_EPS) * nw
            rh = r[:, vs]
            o_ref[0, rows, vs] = (o * (rh * _sigmoid(rh))).astype(o_ref.dtype)


def _gla_layer(h, g_pre, g_post, w_in, a_up, a_bias, norm_w, w_o):
    B, S, D = h.shape
    dk, dv = GLA_KEY_DIM, GLA_VAL_DIM
    tm = min(PROJ_TM, S)
    n_main = 2 * dk + 2 * dv
    w_main = w_in[:, :n_main].astype(BF16)
    w_a = _pad_cols(w_in[:, n_main:], LORA_PAD).astype(BF16)
    aup = _pad_rows(a_up, LORA_PAD).astype(BF16)
    q, k, v, r, la = pl.pallas_call(
        _gla_proj_kernel,
        grid=(B, S // tm),
        in_specs=[_tok_spec(tm, D), _const_spec((1, D)), _const_spec((D, n_main)),
                  _const_spec((D, LORA_PAD)), _const_spec((LORA_PAD, dk)), _const_spec((1, dk))],
        out_specs=[_tok_spec(tm, dk), _tok_spec(tm, dk), _tok_spec(tm, dv), _tok_spec(tm, dv),
                   _tok_spec(tm, dk)],
        out_shape=[jax.ShapeDtypeStruct((B, S, w), F32) for w in (dk, dk, dv, dv, dk)],
        compiler_params=_params(),
        name="gla_proj",
    )(h, _row(g_pre), w_main, w_a, aup, _row(a_bias))

    chunks = 4 if S % (4 * GLA_CHUNK) == 0 else 1
    tc = chunks * GLA_CHUNK
    o = pl.pallas_call(
        functools.partial(_gla_chunk_kernel, chunks=chunks),
        grid=(B, S // tc),
        in_specs=[_tok_spec(tc, dk), _tok_spec(tc, dk), _tok_spec(tc, dv), _tok_spec(tc, dv),
                  _tok_spec(tc, dk), _const_spec((1, dv // GLA_HEADS))],
        out_specs=_tok_spec(tc, dv),
        out_shape=jax.ShapeDtypeStruct((B, S, dv), BF16),
        scratch_shapes=[pltpu.VMEM((GLA_HEADS, dv // GLA_HEADS, dk // GLA_HEADS), F32)],
        compiler_params=_params(),
        name="gla_chunk",
    )(q, k, v, r, la, _row(norm_w))
    return _post(h, o, w_o, g_post)


def _nsa_head_perm():
    perm = np.zeros((D_MODEL,), np.int32)
    for p in range(NSA_HPG):
        for g in range(NSA_KV_GROUPS):
            src = (g * NSA_HPG + p) * NSA_HEAD_DIM
            dst = p * LANES + g * NSA_HEAD_DIM
            perm[dst:dst + NSA_HEAD_DIM] = np.arange(src, src + NSA_HEAD_DIM)
    return perm


def _nsa_proj_kernel(h_ref, gn_ref, wq_ref, wkv_ref, wg_ref, q_ref, kv_ref, gates_ref):
    xn = _rms(h_ref[0], gn_ref[...], NORM_EPS).astype(BF16)
    q_ref[0] = _dot(xn, wq_ref[...]).astype(BF16)
    kv_ref[0] = _dot(xn, wkv_ref[...]).astype(BF16)
    gates_ref[0] = _sigmoid(_dot(xn, wg_ref[...]))


def _nsa_compress_kernel(r_ref, w1_ref, pe_ref, b1_ref, w2_ref, out_ref, *, n_cmp):
    r = r_ref[0, 0].astype(F32)
    rows = r.shape[0]
    pe = pe_ref[0]
    ra = (r + pe[0]).astype(BF16)
    rb = (pltpu.roll(r, rows - 1, 0) + pe[1]).astype(BF16)
    out = jnp.zeros((rows, LANES), F32)
    for g in range(NSA_KV_GROUPS):
        hid = _gelu_tanh(_dot(ra, w1_ref[0, g, 0]) + _dot(rb, w1_ref[0, g, 1]) + b1_ref[0])
        out = out + _dot(hid.astype(BF16), w2_ref[0, g])
    rowi = lax.broadcasted_iota(jnp.int32, out.shape, 0)
    out_ref[0, 0] = jnp.where(rowi < n_cmp, out, 0.0).astype(BF16)


def _group_queries(q, g):
    lane = lax.broadcasted_iota(jnp.int32, (q.shape[0], LANES), 1)
    keep = (lane < NSA_HEAD_DIM) if g == 0 else (lane >= NSA_HEAD_DIM)
    return [jnp.where(keep, q[:, p * LANES:(p + 1) * LANES], jnp.zeros((), q.dtype))
            for p in range(NSA_HPG)]


def _ones_in_other_group(v, g):
    lane = lax.broadcasted_iota(jnp.int32, v.shape, 1)
    other = (lane >= NSA_HEAD_DIM) if g == 0 else (lane < NSA_HEAD_DIM)
    return jnp.where(other, jnp.ones((), v.dtype), v)


def _normalize_groups(acc0, acc1):
    lane = lax.broadcasted_iota(jnp.int32, acc0.shape, 1)
    half = NSA_HEAD_DIM
    return jnp.where(lane < half, acc0 / pltpu.roll(acc0, half, 1), acc1 / pltpu.roll(acc1, half, 1))


def _nsa_cmp_kernel(q_ref, kc_ref, vc_ref, ov_ref, ocmp_ref, score_ref, *, n_cmp, n_slc):
    TQ = q_ref.shape[1]
    ncp = kc_ref.shape[1]
    nbp = ov_ref.shape[0]
    G = NSA_KV_GROUPS
    R = NSA_HPG * TQ
    CW = 2 * TQ
    n_cw = R // CW
    chains = range(G * n_cw)
    i = pl.program_id(1)
    n_q = pl.num_programs(1)
    t0 = i * TQ

    def run(nr):
        q = q_ref[0]
        kc = kc_ref[0, :nr, :]
        vc = vc_ref[0, :nr, :]
        ov = ov_ref[:, :nr]
        pos = t0 + lax.broadcasted_iota(jnp.int32, (1, TQ), 1)
        blk_c = lax.broadcasted_iota(jnp.int32, (nr, 1), 0)
        vis = (blk_c * CMP_STRIDE + (CMP_BLOCK - 1) <= pos) & (blk_c < n_cmp)
        bias = jnp.where(vis, 0.0, NEG_INF)
        bias = jnp.concatenate([bias] * (CW // TQ), axis=1)
        any_vis = (pos >= CMP_BLOCK - 1).astype(F32)
        any_vis = jnp.concatenate([any_vis] * (CW // TQ), axis=1)
        qall = jnp.concatenate([t for g in range(G) for t in _group_queries(q, g)], axis=0)
        s = [_dot_nt(kc, qall[j * CW:(j + 1) * CW]) + bias for j in chains]
        e = [jnp.exp2(s[j] - jnp.max(s[j], axis=0, keepdims=True)) for j in chains]
        p = [e[j] * (any_vis / jnp.sum(e[j], axis=0, keepdims=True)) for j in chains]
        o = [_dot_tn(vc, p[j].astype(BF16)) for j in chains]

        blk_s = lax.broadcasted_iota(jnp.int32, (nbp, 1), 0)
        cur = lax.shift_right_logical(pos, 6)
        causal = (blk_s * SLC_BLOCK <= pos) & (blk_s < n_slc)
        forced = (blk_s == 0) | (blk_s == cur) | (blk_s == cur - 1)
        bonus = np.float32(FORCE_BONUS) * forced.astype(F32)
        for g in range(G):
            psum = jnp.zeros((nr, TQ), F32)
            for j in range(g * n_cw, (g + 1) * n_cw):
                for c in range(CW // TQ):
                    psum = psum + p[j][:, c * TQ:(c + 1) * TQ]
            hi, lo = _split2(psum)
            imp = _dot(ov, hi) + _dot(ov, lo)
            score_ref[0, g * nbp:(g + 1) * nbp, :] = jnp.where(causal, imp + bonus, NEG_INF)

        half = NSA_HEAD_DIM
        for p_i in range(NSA_HPG):
            j, c = divmod(p_i, CW // TQ)
            cols = slice(c * TQ, (c + 1) * TQ)
            o_pair = jnp.concatenate([o[j][:half, cols], o[n_cw + j][half:, cols]], axis=0)
            ocmp_ref[0, :, p_i * LANES:(p_i + 1) * LANES] = o_pair.T

    parts = CMP_ROW_PARTS if ncp % (CMP_ROW_PARTS * LANES) == 0 else 1
    for k in range(parts):
        in_part = (i * parts >= k * n_q) & (i * parts < (k + 1) * n_q)
        pl.when(in_part)(functools.partial(run, (k + 1) * ncp // parts))


def _nsa_topk_kernel(score_ref, sel_ref, *, top_n):
    G = NSA_KV_GROUPS
    nbp = score_ref.shape[1] // G
    sc0 = score_ref[0].reshape(G, nbp, score_ref.shape[2])
    row = lax.broadcasted_iota(jnp.int32, sc0.shape, 1).astype(F32)

    def body(_, carry):
        sc, sel = carry
        m = jnp.max(sc, axis=1, keepdims=True)
        idx = jnp.min(jnp.where(sc == m, row, np.float32(nbp)), axis=1, keepdims=True)
        hit = row == idx
        return jnp.where(hit, np.float32(-3e38), sc), jnp.where(hit, 1.0, sel)

    _, sel = lax.fori_loop(0, top_n, body, (sc0, jnp.zeros_like(sc0)))
    sel_ref[0] = sel.reshape(G * nbp, score_ref.shape[2]).astype(sel_ref.dtype)


def _nsa_attn_kernel(q_ref, ks_ref, vs_ref, kw_ref, vw_ref, sel_ref, ocmp_ref, gates_ref, ge_ref,
                     o_ref, m_ref, acc_ref, *, win_rows):
    TQ = q_ref.shape[1]
    S = ks_ref.shape[1]
    TK = min(NSA_TK, S)
    nbp = sel_ref.shape[1] // NSA_KV_GROUPS
    R = NSA_HPG * TQ
    CW = 2 * TQ
    n_cw = R // CW
    t0 = pl.program_id(1) * TQ
    q = q_ref[0]
    pos = t0 + lax.broadcasted_iota(jnp.int32, (1, TQ), 1)
    n_kt = (t0 + TQ + TK - 1) // TK
    blk_l = lax.broadcasted_iota(jnp.int32, (TK, nbp), 1)
    key_s = lax.broadcasted_iota(jnp.int32, (TK, 1), 0)
    w_start = pl.multiple_of(jnp.maximum(t0 - WINDOW, 0), TQ)
    wkey = w_start + lax.broadcasted_iota(jnp.int32, (win_rows, 1), 0)
    wbias = jnp.where((wkey <= pos) & (wkey > pos - WINDOW), 0.0, NEG_INF)
    wbias = jnp.concatenate([wbias] * (CW // TQ), axis=1)

    G = NSA_KV_GROUPS
    chains = range(G * n_cw)
    qall = jnp.concatenate([t for g in range(G) for t in _group_queries(q, g)], axis=0)

    def scores(k_tile, bias):
        return [_dot_nt(k_tile, qall[j * CW:(j + 1) * CW]) + bias[j // n_cw] for j in chains]

    m_ref[...] = jnp.full(m_ref.shape, NEG_INF, F32)
    acc_ref[...] = jnp.zeros_like(acc_ref)

    def body(kt, carry):
        k0 = pl.multiple_of(kt * TK, TK)
        expand = (blk_l == lax.shift_right_logical(k0 + key_s, 6)).astype(BF16)
        causal = k0 + key_s <= pos
        bias = []
        for g in range(G):
            chosen = _dot(expand, sel_ref[0, g * nbp:(g + 1) * nbp, :])
            b = jnp.where((chosen > 0.5) & causal, 0.0, NEG_INF)
            bias.append(jnp.concatenate([b] * (CW // TQ), axis=1))
        v_raw = vs_ref[0, pl.ds(k0, TK), :]
        v_tile = [_ones_in_other_group(v_raw, g) for g in range(G)]
        s = scores(ks_ref[0, pl.ds(k0, TK), :], bias)
        for g in range(G):
            grp = range(g * n_cw, (g + 1) * n_cw)
            m_old = {j: m_ref[:, j * CW:(j + 1) * CW] for j in grp}
            m_new = {j: jnp.maximum(m_old[j], jnp.max(s[j], axis=0, keepdims=True)) for j in grp}
            p = {j: jnp.exp2(s[j] - m_new[j]).astype(BF16) for j in grp}
            pv = {j: _dot_tn(v_tile[g], p[j]) for j in grp}
            for j in grp:
                cols = slice(j * CW, (j + 1) * CW)
                acc_ref[:, cols] = acc_ref[:, cols] * jnp.exp2(m_old[j] - m_new[j]) + pv[j]
                m_ref[:, cols] = m_new[j]
        return carry

    lax.fori_loop(0, n_kt, body, 0)

    vw_raw = vw_ref[0, pl.ds(w_start, win_rows), :]
    vw = [_ones_in_other_group(vw_raw, g) for g in range(G)]
    s = scores(kw_ref[0, pl.ds(w_start, win_rows), :], [wbias] * G)
    p = [jnp.exp2(s[j] - jnp.max(s[j], axis=0, keepdims=True)).astype(BF16) for j in chains]
    acc_win = jnp.concatenate([_dot_tn(vw[j // n_cw], p[j]) for j in chains], axis=1)

    half = NSA_HEAD_DIM
    normed = {}
    for name, a in (("slc", acc_ref[...]), ("win", acc_win)):
        a0, a1 = a[:, :R], a[:, R:]
        normed[name] = jnp.concatenate([a0[:half] / a0[half:half + 1], a1[half:] / a1[0:1]], axis=0)
    gates = gates_ref[0]
    g_cmp = _dot_hilo(gates, ge_ref[0])
    g_slc = _dot_hilo(gates, ge_ref[1])
    g_win = _dot_hilo(gates, ge_ref[2])
    for p_i in range(NSA_HPG):
        cols = slice(p_i * LANES, (p_i + 1) * LANES)
        slc = normed["slc"][:, p_i * TQ:(p_i + 1) * TQ].T
        win = normed["win"][:, p_i * TQ:(p_i + 1) * TQ].T
        o_ref[0, :, cols] = (g_cmp[:, cols] * ocmp_ref[0, :, cols] + g_slc[:, cols] * slc
                             + g_win[:, cols] * win).astype(o_ref.dtype)


def _nsa_layer(h, g_pre, g_post, w_in, cmp_w1, cmp_b1, cmp_w2, cmp_pe, w_o):
    B, S, D = h.shape
    H, G, dh, hpg = NSA_HEADS, NSA_KV_GROUPS, NSA_HEAD_DIM, NSA_HPG
    perm = _nsa_head_perm()
    n_q = H * dh
    n_kv = 6 * G * dh
    wq = (w_in[:, :n_q] * np.float32(dh ** -0.5 * np.log2(np.e)))[:, perm].astype(BF16)
    wkv = w_in[:, n_q:n_q + n_kv].astype(BF16)
    wgt = _pad_cols(w_in[:, n_q + n_kv:], LANES).astype(BF16)
    tm = min(PROJ_TM, S)
    q, kv, gates = pl.pallas_call(
        _nsa_proj_kernel,
        grid=(B, S // tm),
        in_specs=[_tok_spec(tm, D), _const_spec((1, D)), _const_spec((D, n_q)),
                  _const_spec((D, n_kv)), _const_spec((D, LANES))],
        out_specs=[_tok_spec(tm, n_q), _tok_spec(tm, n_kv), _tok_spec(tm, LANES)],
        out_shape=[jax.ShapeDtypeStruct((B, S, n_q), BF16), jax.ShapeDtypeStruct((B, S, n_kv), BF16),
                   jax.ShapeDtypeStruct((B, S, LANES), F32)],
        compiler_params=_params(),
        name="nsa_proj",
    )(h, _row(g_pre), wq, wkv, wgt)

    n_cmp = (S - CMP_BLOCK) // CMP_STRIDE + 1
    ncp = S // CMP_STRIDE
    half = CMP_BLOCK // 2
    assert CMP_STRIDE == half
    rk = jnp.stack([kv[:, :, :LANES], kv[:, :, LANES:2 * LANES]]).reshape(2, B, ncp, half * LANES)
    w1 = cmp_w1.reshape(2, 2, half, dh, CMP_HIDDEN)
    w1x = jnp.zeros((2, G, 2, half, G, dh, CMP_HIDDEN), F32)
    for g in range(G):
        w1x = w1x.at[:, g, :, :, g].set(w1)
    w1x = w1x.reshape(2, G, 2, half * LANES, CMP_HIDDEN).astype(BF16)
    pe = jnp.broadcast_to(cmp_pe.reshape(2, 2, half, 1, dh), (2, 2, half, G, dh))
    pe = pe.reshape(2, 2, 1, half * LANES).astype(F32)
    w2x = jnp.zeros((2, G, CMP_HIDDEN, G, dh), F32)
    for g in range(G):
        w2x = w2x.at[:, g, :, g].set(cmp_w2)
    w2x = w2x.reshape(2, G, CMP_HIDDEN, LANES).astype(BF16)
    kvc = pl.pallas_call(
        functools.partial(_nsa_compress_kernel, n_cmp=n_cmp),
        grid=(2, B),
        in_specs=[pl.BlockSpec((1, 1, ncp, half * LANES), lambda c, b: (c, b, 0, 0)),
                  pl.BlockSpec((1, G, 2, half * LANES, CMP_HIDDEN), lambda c, b: (c, 0, 0, 0, 0)),
                  pl.BlockSpec((1, 2, 1, half * LANES), lambda c, b: (c, 0, 0, 0)),
                  pl.BlockSpec((1, 1, CMP_HIDDEN), lambda c, b: (c, 0, 0)),
                  pl.BlockSpec((1, G, CMP_HIDDEN, LANES), lambda c, b: (c, 0, 0, 0))],
        out_specs=pl.BlockSpec((1, 1, ncp, LANES), lambda c, b: (c, b, 0, 0)),
        out_shape=jax.ShapeDtypeStruct((2, B, ncp, LANES), BF16),
        compiler_params=_params(),
        name="nsa_compress",
    )(rk, w1x, pe, cmp_b1.reshape(2, 1, CMP_HIDDEN).astype(F32), w2x)

    n_slc = S // SLC_BLOCK
    top_n = min(SLC_TOPK, n_slc)
    nbp = max(LANES, n_slc)
    start = np.arange(ncp) * CMP_STRIDE
    end = start + CMP_BLOCK - 1
    s_start = np.arange(nbp) * SLC_BLOCK
    s_end = s_start + SLC_BLOCK - 1
    overlap = ((end[:, None] >= s_start[None, :]) & (start[:, None] <= s_end[None, :])
               & (np.arange(ncp)[:, None] < n_cmp) & (np.arange(nbp)[None, :] < n_slc))
    overlap_t = jnp.asarray(overlap.astype(np.float32).T, BF16)
    TQ = min(NSA_TQ, S)
    blk_tok_spec = lambda t: pl.BlockSpec((1, G * nbp, t), lambda b, i: (b, 0, i))
    o_cmp, score = pl.pallas_call(
        functools.partial(_nsa_cmp_kernel, n_cmp=n_cmp, n_slc=n_slc),
        grid=(B, S // TQ),
        in_specs=[_tok_spec(TQ, n_q),
                  pl.BlockSpec((None, 1, ncp, LANES), lambda b, i: (0, b, 0, 0)),
                  pl.BlockSpec((None, 1, ncp, LANES), lambda b, i: (1, b, 0, 0)),
                  _const_spec((nbp, ncp))],
        out_specs=[_tok_spec(TQ, n_q), blk_tok_spec(TQ)],
        out_shape=[jax.ShapeDtypeStruct((B, S, n_q), F32),
                   jax.ShapeDtypeStruct((B, G * nbp, S), F32)],
        compiler_params=_params(),
        name="nsa_compressed_attn",
    )(q, kvc, kvc, overlap_t)
    tr = min(NSA_TOPK_TOKENS, S)
    sel = pl.pallas_call(
        functools.partial(_nsa_topk_kernel, top_n=top_n),
        grid=(B, S // tr),
        in_specs=[blk_tok_spec(tr)],
        out_specs=blk_tok_spec(tr),
        out_shape=jax.ShapeDtypeStruct((B, G * nbp, S), BF16),
        compiler_params=_params(),
        name="nsa_topk",
    )(score)

    ge = np.zeros((3, LANES, n_q), np.float32)
    for g in range(G):
        for p in range(hpg):
            for j in range(3):
                ge[j, g * hpg * 3 + p * 3 + j, p * LANES + g * dh:p * LANES + (g + 1) * dh] = 1.0
    ge = jnp.asarray(ge, BF16)
    win_rows = min(WINDOW + TQ, S)
    kv_spec = lambda c: pl.BlockSpec((1, S, LANES), lambda b, i, c=c: (b, 0, c))
    o = pl.pallas_call(
        functools.partial(_nsa_attn_kernel, win_rows=win_rows),
        grid=(B, S // TQ),
        in_specs=[_tok_spec(TQ, n_q), kv_spec(2), kv_spec(3), kv_spec(4), kv_spec(5),
                  blk_tok_spec(TQ), _tok_spec(TQ, n_q), _tok_spec(TQ, LANES),
                  _const_spec((3, LANES, n_q))],
        out_specs=_tok_spec(TQ, n_q),
        out_shape=jax.ShapeDtypeStruct((B, S, n_q), BF16),
        scratch_shapes=[pltpu.VMEM((1, G * hpg * TQ), F32), pltpu.VMEM((LANES, G * hpg * TQ), F32)],
        compiler_params=_params(),
        name="nsa_selected_window_attn",
    )(q, kv, kv, kv, kv, sel, o_cmp, gates, ge)
    return _post(h, o, w_o[perm, :], g_post)


def kernel(x, norm_g, ffn_w_in, ffn_conv_w, ffn_conv_b, ffn_w_out, rwkv_mu, rwkv_w_rkv, rwkv_w0, rwkv_w1, rwkv_w2, rwkv_a0, rwkv_a1, rwkv_a2, rwkv_g1, rwkv_g2, rwkv_k_k, rwkv_k_a, rwkv_r_k, rwkv_ln_w, rwkv_ln_b, rwkv_w_o, rwkv_v0, rwkv_v1, rwkv_v2, nsa_w_in, nsa_cmp_w1, nsa_cmp_b1, nsa_cmp_w2, nsa_cmp_pe, nsa_w_o, gla_w_in, gla_a_up, gla_a_bias, gla_norm_w, gla_w_o):
    depth = norm_g.shape[0]
    h = x
    v_first = None
    for i in range(depth):
        kind, j = i % 3, i // 3
        if kind == 0:
            vres = None if j == 0 else (rwkv_v0[j - 1], rwkv_v1[j - 1], rwkv_v2[j - 1])
            h, v_first = _rwkv_layer(
                h, norm_g[i, 0], norm_g[i, 1], rwkv_mu[j], rwkv_w_rkv[j], rwkv_w0[j], rwkv_w1[j],
                rwkv_w2[j], rwkv_a0[j], rwkv_a1[j], rwkv_a2[j], rwkv_g1[j], rwkv_g2[j], rwkv_k_k[j],
                rwkv_k_a[j], rwkv_r_k[j], rwkv_ln_w[j], rwkv_ln_b[j], rwkv_w_o[j], v_first, vres)
        elif kind == 1:
            h = _nsa_layer(h, norm_g[i, 0], norm_g[i, 1], nsa_w_in[j], nsa_cmp_w1[j], nsa_cmp_b1[j],
                           nsa_cmp_w2[j], nsa_cmp_pe[j], nsa_w_o[j])
        else:
            h = _gla_layer(h, norm_g[i, 0], norm_g[i, 1], gla_w_in[j], gla_a_up[j], gla_a_bias[j],
                           gla_norm_w[j], gla_w_o[j])
        h = _ffn(h, norm_g[i, 2], ffn_w_in[i], ffn_conv_w[i], ffn_conv_b[i], ffn_w_out[i],
                 norm_g[i, 3])
    return h
```

```python
import functools

import numpy as np
import jax
import jax.numpy as jnp
from jax import lax
from jax.experimental import pallas as pl
from jax.experimental.pallas import tpu as pltpu

F32 = jnp.float32
BF16 = jnp.bfloat16

V7X_VMEM_BYTES = 64 * 1024 * 1024
VMEM_LIMIT_BYTES = V7X_VMEM_BYTES - 8 * 1024 * 1024
LANES = 128
SUBLANES = 8

D_MODEL = 1024
NORM_EPS = 1e-6
NEG_INF = -1e30

RWKV_HEAD_DIM = 64
RWKV_HEADS = D_MODEL // RWKV_HEAD_DIM
RWKV_GN_EPS = 64e-5
RWKV_CHUNK = 64
RWKV_CHUNKS_PER_STEP = 2
LORA_PAD = 128

NSA_HEADS = 16
NSA_KV_GROUPS = 2
NSA_HEAD_DIM = D_MODEL // NSA_HEADS
NSA_HPG = NSA_HEADS // NSA_KV_GROUPS
CMP_BLOCK = 32
CMP_STRIDE = 16
CMP_HIDDEN = 256
CMP_ROW_PARTS = 4
SLC_BLOCK = 64
SLC_TOPK = 16
WINDOW = 512
FORCE_BONUS = 1e4
NSA_TQ = 128
NSA_TK = 512
NSA_TOPK_TOKENS = 512

GLA_HEADS = 4
GLA_KEY_DIM = D_MODEL // 2
GLA_VAL_DIM = D_MODEL
GLA_GATE_LORA = 16
GLA_TAU = 16.0
GLA_CHUNK = 64
GLA_NORM_EPS = 1e-5

D_FF = 2816
FFN_CHUNK = 256
FFN_TM = 512
PROJ_TM = 256
POST_TM = 512


def _dot(a, b):
    return jnp.dot(a, b, preferred_element_type=F32)


def _dot_nt(a, b):
    return lax.dot_general(a, b, (((1,), (1,)), ((), ())), preferred_element_type=F32)


def _dot_tn(a, b):
    return lax.dot_general(a, b, (((0,), (0,)), ((), ())), preferred_element_type=F32)


def _split2(x):
    hi = x.astype(BF16)
    lo = (x - hi.astype(F32)).astype(BF16)
    return hi, lo


def _dot_hilo(x, m):
    hi, lo = _split2(x)
    return _dot(hi, m) + _dot(lo, m)


def _dot_tri3(tri, x):
    hi = x.astype(BF16)
    r1 = x - hi.astype(F32)
    mid = r1.astype(BF16)
    lo = (r1 - mid.astype(F32)).astype(BF16)
    return _dot(tri, hi) + _dot(tri, mid) + _dot(tri, lo)


def _head_sum(x, red_ref, exp_ref):
    return _dot_hilo(_dot(x.astype(BF16), red_ref[...]), exp_ref[...])


def _rms(x, g, eps):
    return x * lax.rsqrt(jnp.mean(x * x, axis=-1, keepdims=True) + eps) * g


def _sigmoid(x):
    return 1.0 / (1.0 + jnp.exp(-x))


def _softplus(x):
    return jnp.maximum(x, 0.0) + jnp.log(1.0 + jnp.exp(-jnp.abs(x)))


def _gelu_tanh(x):
    c = np.float32(np.sqrt(2.0 / np.pi))
    return 0.5 * x * (1.0 + jnp.tanh(c * (x + 0.044715 * (x * x * x))))


def _shift_rows(x, n, prev8):
    rolled = pltpu.roll(x, n, 0)
    row = lax.broadcasted_iota(jnp.int32, x.shape, 0)
    for j in range(n):
        rolled = jnp.where(row == j, prev8[SUBLANES - n + j:SUBLANES - n + j + 1, :], rolled)
    return rolled


def _const_spec(shape):
    nd = len(shape)
    return pl.BlockSpec(shape, lambda *_: (0,) * nd, pipeline_mode=pl.Buffered(1))


def _tok_spec(tm, width):
    return pl.BlockSpec((1, tm, width), lambda b, i: (b, i, 0))


def _params():
    return pltpu.CompilerParams(dimension_semantics=("arbitrary", "arbitrary"),
                                vmem_limit_bytes=VMEM_LIMIT_BYTES)


def _row(v):
    return v.reshape(1, -1).astype(F32)


def _ffn_kernel(h_ref, gpre_ref, win_ref, cw_ref, cb_ref, wo_ref, gpost_ref, out_ref,
                xn_ref, acc_ref, halo_ref, *, fc):
    @pl.when(pl.program_id(1) == 0)
    def _():
        halo_ref[...] = jnp.zeros_like(halo_ref)

    h = h_ref[0]
    xn_ref[...] = _rms(h, gpre_ref[...], NORM_EPS).astype(BF16)
    acc_ref[...] = jnp.zeros_like(acc_ref)
    tm = h.shape[0]
    F = wo_ref.shape[0]
    nc = F // fc

    def gate_up(c):
        xn = xn_ref[...]
        return (_dot(xn, win_ref[:, c * fc:(c + 1) * fc]),
                _dot(xn, win_ref[:, F + c * fc:F + (c + 1) * fc]))

    nxt = gate_up(0)
    for c in range(nc):
        cols = slice(c * fc, (c + 1) * fc)
        gate, up = nxt
        if c + 1 < nc:
            nxt = gate_up(c + 1)
        prev = halo_ref[:, cols]
        g1 = _shift_rows(gate, 1, prev)
        g2 = _shift_rows(gate, 2, prev)
        cw = cw_ref[:, cols]
        z = cw[2:3] * gate + cw[1:2] * g1 + cw[0:1] * g2 + cb_ref[:, cols]
        halo_ref[:, cols] = gate[tm - SUBLANES:tm, :]
        act = (z * _sigmoid(z) * up).astype(BF16)
        acc_ref[...] += _dot(act, wo_ref[cols, :])
    out_ref[0] = h + _rms(acc_ref[...], gpost_ref[...], NORM_EPS)


def _ffn(h, g_pre, w_in, conv_w, conv_b, w_out, g_post):
    B, S, D = h.shape
    F = w_out.shape[0]
    assert F % FFN_CHUNK == 0
    tm = min(FFN_TM, S)
    return pl.pallas_call(
        functools.partial(_ffn_kernel, fc=FFN_CHUNK),
        grid=(B, S // tm),
        in_specs=[_tok_spec(tm, D), _const_spec((1, D)), _const_spec((D, 2 * F)),
                  _const_spec((3, F)), _const_spec((1, F)), _const_spec((F, D)),
                  _const_spec((1, D))],
        out_specs=_tok_spec(tm, D),
        out_shape=jax.ShapeDtypeStruct((B, S, D), F32),
        scratch_shapes=[pltpu.VMEM((tm, D), BF16), pltpu.VMEM((tm, D), F32),
                        pltpu.VMEM((SUBLANES, F), F32)],
        compiler_params=_params(),
        name="conv_ffn",
    )(h, _row(g_pre), w_in.astype(BF16), conv_w.astype(F32), _row(conv_b), w_out.astype(BF16),
      _row(g_post))


def _post_kernel(h_ref, o_ref, wo_ref, g_ref, out_ref):
    y = _dot(o_ref[0].astype(BF16), wo_ref[...])
    out_ref[0] = h_ref[0] + _rms(y, g_ref[...], NORM_EPS)


def _post(h, o, w_o, g_post):
    B, S, D = h.shape
    K = o.shape[-1]
    tm = min(POST_TM, S)
    return pl.pallas_call(
        _post_kernel,
        grid=(B, S // tm),
        in_specs=[_tok_spec(tm, D), _tok_spec(tm, K), _const_spec((K, D)), _const_spec((1, D))],
        out_specs=_tok_spec(tm, D),
        out_shape=jax.ShapeDtypeStruct((B, S, D), F32),
        compiler_params=_params(),
        name="mixer_out",
    )(h, o, w_o.astype(BF16), _row(g_post))


def _rwkv_proj_kernel(*refs, has_vres):
    if has_vres:
        (h_ref, gn_ref, mu_ref, wr_ref, wk_ref, wv_ref, w1_ref, w2_ref, a1_ref, a2_ref, g1_ref,
         g2_ref, vec_ref, red_ref, exp_ref, vf_ref, v1_ref, v2_ref,
         r_ref, ld_ref, k_ref, v_ref, kkn_ref, b_ref, g_ref, bonus_ref, carry_ref) = refs
    else:
        (h_ref, gn_ref, mu_ref, wr_ref, wk_ref, wv_ref, w1_ref, w2_ref, a1_ref, a2_ref, g1_ref,
         g2_ref, vec_ref, red_ref, exp_ref,
         r_ref, ld_ref, k_ref, v_ref, kkn_ref, b_ref, g_ref, bonus_ref, carry_ref) = refs

    @pl.when(pl.program_id(1) == 0)
    def _():
        carry_ref[...] = jnp.zeros_like(carry_ref)

    x = _rms(h_ref[0], gn_ref[...], NORM_EPS)
    tm = x.shape[0]
    xx = _shift_rows(x, 1, carry_ref[...]) - x
    carry_ref[...] = x[tm - SUBLANES:tm, :]
    mu = mu_ref[...]
    xr, xw, xk, xv, xa, xg = ((x + xx * mu[c:c + 1]).astype(BF16) for c in range(6))
    vec = vec_ref[...]
    w0, a0, k_k, k_a, r_k, v0 = (vec[c:c + 1] for c in range(6))

    r = _dot(xr, wr_ref[...])
    k = _dot(xk, wk_ref[...])
    v = _dot(xv, wv_ref[...])
    wl = -_softplus(-(w0 + _dot(jnp.tanh(_dot(xw, w1_ref[...])).astype(BF16), w2_ref[...]))) - 0.5
    ld_ref[0] = -jnp.exp(wl)
    a = _sigmoid(a0 + _dot(_dot(xa, a1_ref[...]).astype(BF16), a2_ref[...]))
    g_ref[0] = _dot(_sigmoid(_dot(xg, g1_ref[...])).astype(BF16), g2_ref[...])
    if has_vres:
        mix = _sigmoid(v0 + _dot(_dot(xv, v1_ref[...]).astype(BF16), v2_ref[...]))
        v = v + (vf_ref[0] - v) * mix
    kk = k * k_k
    norm = jnp.sqrt(_head_sum(kk * kk, red_ref, exp_ref))
    kkn = kk / jnp.maximum(norm, 1e-12)
    k = k * (1.0 + (a - 1.0) * k_a)
    r_ref[0] = r
    k_ref[0] = k
    v_ref[0] = v
    kkn_ref[0] = kkn
    b_ref[0] = kkn * a
    bonus_ref[0] = _head_sum(r * k * r_k, red_ref, exp_ref) * v


def _pad_cols(w, n):
    return jnp.pad(w, ((0, 0), (0, n - w.shape[1])))


def _pad_rows(w, n):
    return jnp.pad(w, ((0, n - w.shape[0]), (0, 0)))


def _head_reduce_expand(d, hd, value):
    red = (np.arange(d)[:, None] // hd == np.arange(LANES)[None, :]).astype(np.float32)
    return jnp.asarray(red, BF16), jnp.asarray(red.T * value, BF16)


def _rwkv_proj(h, g_norm, mu, w_rkv, w0, w1, w2, a0, a1, a2, g1, g2, k_k, k_a, r_k, v_first, vres):
    B, S, D = h.shape
    tm = min(PROJ_TM, S)
    has_vres = vres is not None
    v0 = vres[0] if has_vres else jnp.zeros((D,), F32)
    vec = jnp.stack([w0, a0, k_k, k_a, r_k.reshape(-1), v0]).astype(F32)
    red, exp = _head_reduce_expand(D, RWKV_HEAD_DIM, 1.0)
    lora_in = lambda w: _pad_cols(w, LORA_PAD).astype(BF16)
    lora_out = lambda w: _pad_rows(w, LORA_PAD).astype(BF16)
    args = [h, _row(g_norm), mu, w_rkv[0].astype(BF16), w_rkv[1].astype(BF16), w_rkv[2].astype(BF16),
            lora_in(w1), lora_out(w2), lora_in(a1), lora_out(a2), lora_in(g1), lora_out(g2), vec, red, exp]
    specs = [_tok_spec(tm, D), _const_spec((1, D)), _const_spec((6, D)), _const_spec((D, D)),
             _const_spec((D, D)), _const_spec((D, D)), _const_spec((D, LORA_PAD)),
             _const_spec((LORA_PAD, D)), _const_spec((D, LORA_PAD)), _const_spec((LORA_PAD, D)),
             _const_spec((D, LORA_PAD)), _const_spec((LORA_PAD, D)), _const_spec((6, D)),
             _const_spec((D, LANES)), _const_spec((LANES, D))]
    if has_vres:
        args += [v_first, lora_in(vres[1]), lora_out(vres[2])]
        specs += [_tok_spec(tm, D), _const_spec((D, LORA_PAD)), _const_spec((LORA_PAD, D))]
    outs = pl.pallas_call(
        functools.partial(_rwkv_proj_kernel, has_vres=has_vres),
        grid=(B, S // tm),
        in_specs=specs,
        out_specs=[_tok_spec(tm, D)] * 8,
        out_shape=[jax.ShapeDtypeStruct((B, S, D), F32)] * 8,
        scratch_shapes=[pltpu.VMEM((SUBLANES, D), F32)],
        compiler_params=_params(),
        name="rwkv_proj",
    )(*args)
    return outs


def _rwkv_scan_kernel(r_ref, ld_ref, k_ref, v_ref, kkn_ref, b_ref, y_ref, st_ref):
    @pl.when(pl.program_id(1) == 0)
    def _():
        st_ref[...] = jnp.zeros_like(st_ref)

    C = RWKV_CHUNK
    N = RWKV_HEAD_DIM
    assert 2 * N == LANES and C == N
    rowi = lax.broadcasted_iota(jnp.int32, (C, C), 0)
    coli = lax.broadcasted_iota(jnp.int32, (C, C), 1)
    tri = (rowi >= coli).astype(BF16)
    prow = lax.broadcasted_iota(jnp.int32, (C, LANES), 0)
    pcol = lax.broadcasted_iota(jnp.int32, (C, LANES), 1) & (N - 1)
    low_incl = prow >= pcol
    low_strict = prow > pcol
    eye = (prow == pcol).astype(F32)
    srow = lax.broadcasted_iota(jnp.int32, (LANES, LANES), 0)
    scol = lax.broadcasted_iota(jnp.int32, (LANES, LANES), 1)
    same_head = (srow < N) == (scol < N)

    def bdiag(x):
        lane = lax.broadcasted_iota(jnp.int32, x.shape, 1)
        zero = jnp.zeros((), x.dtype)
        return jnp.concatenate([jnp.where(lane < N, x, zero), jnp.where(lane >= N, x, zero)], axis=0)

    n_chunks = r_ref.shape[1] // C
    n_pairs = RWKV_HEADS // 2
    sls = [slice(p * LANES, (p + 1) * LANES) for p in range(n_pairs)]
    a_t, r_t, b_t, k_t, b_h, k_h, vb, p_all = ([] for _ in range(8))
    for ci in range(n_chunks):
        rows = pl.ds(ci * C, C)
        ld = ld_ref[0, rows, :]
        cum = _dot_tri3(tri, ld)
        cum_last = cum[C - 1:C, :]
        p_inv = jnp.exp(-cum)
        p_rem = jnp.exp(cum_last - cum)
        p_all.append(jnp.exp(cum_last))
        b = b_ref[0, rows, :]
        k = k_ref[0, rows, :]
        a_t.append((-kkn_ref[0, rows, :] * jnp.exp(cum - ld)).astype(BF16))
        r_t.append((r_ref[0, rows, :] * jnp.exp(cum)).astype(BF16))
        b_t.append((b * p_inv).astype(BF16))
        k_t.append((k * p_inv).astype(BF16))
        b_h.append((b * p_rem).astype(BF16))
        k_h.append((k * p_rem).astype(BF16))
        vb.append(v_ref[0, rows, :].astype(BF16))

    units = [(ci, p) for ci in range(n_chunks) for p in range(n_pairs)]
    idx = range(len(units))
    tile = lambda arr, u: arr[units[u][0]][:, sls[units[u][1]]]
    ar = [jnp.concatenate([tile(a_t, u), tile(r_t, u)], axis=0) for u in idx]
    s_b = [_dot_nt(ar[u], bdiag(tile(b_t, u))) for u in idx]
    s_k = [_dot_nt(ar[u], bdiag(tile(k_t, u))) for u in idx]
    a_ab = [jnp.where(low_strict, s_b[u][:C], 0.0) for u in idx]
    a_rb = [jnp.where(low_incl, s_b[u][C:], 0.0).astype(BF16) for u in idx]
    a_ak = [jnp.where(low_strict, s_k[u][:C], 0.0).astype(BF16) for u in idx]
    a_rk = [jnp.where(low_incl, s_k[u][C:], 0.0).astype(BF16) for u in idx]
    av = [_dot(jnp.concatenate([a_ak[u], a_rk[u]], axis=0), bdiag(tile(vb, u))) for u in idx]
    akv = [av[u][:C].astype(BF16) for u in idx]
    pw = [a_ab[u].astype(BF16) for u in idx]
    tinv = [eye + a_ab[u] for u in idx]
    pw = [_dot(pw[u], bdiag(pw[u])).astype(BF16) for u in idx]
    span = 4
    while span < C:
        both = [_dot(jnp.concatenate([pw[u], tinv[u].astype(BF16)], axis=0), bdiag(pw[u]))
                for u in idx]
        tinv = [tinv[u] + both[u][C:] for u in idx]
        pw = [both[u][:C].astype(BF16) for u in idx]
        span *= 2
    tinv = [tinv[u] + _dot(tinv[u].astype(BF16), bdiag(pw[u])) for u in idx]
    tb = [tinv[u].astype(BF16) for u in idx]
    a_hat = [_dot(tb[u], bdiag(tile(a_t, u))).astype(BF16) for u in idx]
    u_loc = [_dot(tb[u], bdiag(akv[u])) for u in idx]

    pairs = range(n_pairs)
    st = [st_ref[p] for p in pairs]
    for ci in range(n_chunks):
        u0 = ci * n_pairs
        on_state = [_dot_nt(jnp.concatenate([a_hat[u0 + p], r_t[ci][:, sls[p]]], axis=0),
                            st[p].astype(BF16)) for p in pairs]
        ub = [(on_state[p][:C] + u_loc[u0 + p]).astype(BF16) for p in pairs]
        for p in pairs:
            y_ref[0, pl.ds(ci * C, C), sls[p]] = (on_state[p][C:] + _dot(a_rb[u0 + p], bdiag(ub[p]))
                                                  + av[u0 + p][C:])
        upd = [_dot_tn(jnp.concatenate([ub[p], vb[ci][:, sls[p]]], axis=0),
                       jnp.concatenate([b_h[ci][:, sls[p]], k_h[ci][:, sls[p]]], axis=0))
               for p in pairs]
        st = [st[p] * p_all[ci][:, sls[p]] + jnp.where(same_head, upd[p], 0.0) for p in pairs]
    for p in pairs:
        st_ref[p] = st[p]


def _rwkv_scan(r, ld, k, v, kkn, b):
    B, S, D = r.shape
    C = RWKV_CHUNK * RWKV_CHUNKS_PER_STEP
    return pl.pallas_call(
        _rwkv_scan_kernel,
        grid=(B, S // C),
        in_specs=[_tok_spec(C, D)] * 6,
        out_specs=_tok_spec(C, D),
        out_shape=jax.ShapeDtypeStruct((B, S, D), F32),
        scratch_shapes=[pltpu.VMEM((RWKV_HEADS // 2, LANES, LANES), F32)],
        compiler_params=_params(),
        name="rwkv_scan",
    )(r, ld, k, v, kkn, b)


def _rwkv_post_kernel(h_ref, y_ref, bonus_ref, g_ref, vec_ref, red_ref, avg_ref, wo_ref, gpost_ref,
                      out_ref):
    y = y_ref[0]
    d = y - _head_sum(y, red_ref, avg_ref)
    var = _head_sum(d * d, red_ref, avg_ref)
    vec = vec_ref[...]
    yn = d * lax.rsqrt(var + RWKV_GN_EPS) * vec[0:1] + vec[1:2]
    o = ((yn + bonus_ref[0]) * g_ref[0]).astype(BF16)
    out_ref[0] = h_ref[0] + _rms(_dot(o, wo_ref[...]), gpost_ref[...], NORM_EPS)


def _rwkv_post(h, y, bonus, g, ln_w, ln_b, w_o, g_post):
    B, S, D = h.shape
    tm = min(PROJ_TM, S)
    vec = jnp.stack([ln_w, ln_b]).astype(F32)
    red, avg = _head_reduce_expand(D, RWKV_HEAD_DIM, 1.0 / RWKV_HEAD_DIM)
    return pl.pallas_call(
        _rwkv_post_kernel,
        grid=(B, S // tm),
        in_specs=[_tok_spec(tm, D)] * 4 + [_const_spec((2, D)), _const_spec((D, LANES)),
                                           _const_spec((LANES, D)), _const_spec((D, D)),
                                           _const_spec((1, D))],
        out_specs=_tok_spec(tm, D),
        out_shape=jax.ShapeDtypeStruct((B, S, D), F32),
        compiler_params=_params(),
        name="rwkv_out",
    )(h, y, bonus, g, vec, red, avg, w_o.astype(BF16), _row(g_post))


def _rwkv_layer(h, g_pre, g_post, mu, w_rkv, w0, w1, w2, a0, a1, a2, g1, g2, k_k, k_a, r_k,
                ln_w, ln_b, w_o, v_first, vres):
    r, ld, k, v, kkn, b, g, bonus = _rwkv_proj(h, g_pre, mu, w_rkv, w0, w1, w2, a0, a1, a2, g1, g2,
                                               k_k, k_a, r_k, v_first, vres)
    if vres is None:
        v_first = v
    y = _rwkv_scan(r, ld, k, v, kkn, b)
    return _rwkv_post(h, y, bonus, g, ln_w, ln_b, w_o, g_post), v_first


def _gla_proj_kernel(h_ref, gn_ref, w_ref, wa_ref, aup_ref, abias_ref,
                     q_ref, k_ref, v_ref, r_ref, la_ref):
    xn = _rms(h_ref[0], gn_ref[...], NORM_EPS).astype(BF16)
    qkvr = _dot(xn, w_ref[...])
    dk = GLA_KEY_DIM
    dv = GLA_VAL_DIM
    q_ref[0] = qkvr[:, :dk] * np.float32((dk // GLA_HEADS) ** -0.5)
    k_ref[0] = qkvr[:, dk:2 * dk]
    v_ref[0] = qkvr[:, 2 * dk:2 * dk + dv]
    r_ref[0] = qkvr[:, 2 * dk + dv:]
    z = _dot(_dot(xn, wa_ref[...]).astype(BF16), aup_ref[...]) + abias_ref[...]
    la_ref[0] = -_softplus(-z) * np.float32(1.0 / GLA_TAU)


def _gla_chunk_kernel(q_ref, k_ref, v_ref, r_ref, la_ref, nw_ref, o_ref, st_ref, *, chunks):
    @pl.when(pl.program_id(1) == 0)
    def _():
        st_ref[...] = jnp.zeros_like(st_ref)

    C = GLA_CHUNK
    dk = GLA_KEY_DIM // GLA_HEADS
    dv = GLA_VAL_DIM // GLA_HEADS
    rowi = lax.broadcasted_iota(jnp.int32, (C, C), 0)
    coli = lax.broadcasted_iota(jnp.int32, (C, C), 1)
    low_incl = rowi >= coli
    tri = low_incl.astype(BF16)
    nw = nw_ref[...]

    for ci in range(chunks):
        rows = pl.ds(ci * C, C)
        bcum = _dot_tri3(tri, la_ref[0, rows, :])
        b_last = bcum[C - 1:C, :]
        k = k_ref[0, rows, :]
        qe = (q_ref[0, rows, :] * jnp.exp(bcum)).astype(BF16)
        ke = (k * jnp.exp(-bcum)).astype(BF16)
        kd = (k * jnp.exp(b_last - bcum)).astype(BF16)
        dec = jnp.exp(b_last)
        vb = v_ref[0, rows, :].astype(BF16)
        r = r_ref[0, rows, :]
        for h in range(GLA_HEADS):
            ks = slice(h * dk, (h + 1) * dk)
            vs = slice(h * dv, (h + 1) * dv)
            a = jnp.where(low_incl, _dot_nt(qe[:, ks], ke[:, ks]), 0.0)
            st = st_ref[h]
            o = _dot(a.astype(BF16), vb[:, vs]) + _dot_nt(qe[:, ks], st.astype(BF16))
            st_ref[h] = st * dec[:, ks] + _dot_tn(vb[:, vs], kd[:, ks])
            o = o * lax.rsqrt(jnp.mean(o * o, axis=-1, keepdims=True) + GLA_NORM_EPS) * nw
            rh = r[:, vs]
            o_ref[0, rows, vs] = (o * (rh * _sigmoid(rh))).astype(o_ref.dtype)


def _gla_layer(h, g_pre, g_post, w_in, a_up, a_bias, norm_w, w_o):
    B, S, D = h.shape
    dk, dv = GLA_KEY_DIM, GLA_VAL_DIM
    tm = min(PROJ_TM, S)
    n_main = 2 * dk + 2 * dv
    w_main = w_in[:, :n_main].astype(BF16)
    w_a = _pad_cols(w_in[:, n_main:], LORA_PAD).astype(BF16)
    aup = _pad_rows(a_up, LORA_PAD).astype(BF16)
    q, k, v, r, la = pl.pallas_call(
        _gla_proj_kernel,
        grid=(B, S // tm),
        in_specs=[_tok_spec(tm, D), _const_spec((1, D)), _const_spec((D, n_main)),
                  _const_spec((D, LORA_PAD)), _const_spec((LORA_PAD, dk)), _const_spec((1, dk))],
        out_specs=[_tok_spec(tm, dk), _tok_spec(tm, dk), _tok_spec(tm, dv), _tok_spec(tm, dv),
                   _tok_spec(tm, dk)],
        out_shape=[jax.ShapeDtypeStruct((B, S, w), F32) for w in (dk, dk, dv, dv, dk)],
        compiler_params=_params(),
        name="gla_proj",
    )(h, _row(g_pre), w_main, w_a, aup, _row(a_bias))

    chunks = 4 if S % (4 * GLA_CHUNK) == 0 else 1
    tc = chunks * GLA_CHUNK
    o = pl.pallas_call(
        functools.partial(_gla_chunk_kernel, chunks=chunks),
        grid=(B, S // tc),
        in_specs=[_tok_spec(tc, dk), _tok_spec(tc, dk), _tok_spec(tc, dv), _tok_spec(tc, dv),
                  _tok_spec(tc, dk), _const_spec((1, dv // GLA_HEADS))],
        out_specs=_tok_spec(tc, dv),
        out_shape=jax.ShapeDtypeStruct((B, S, dv), BF16),
        scratch_shapes=[pltpu.VMEM((GLA_HEADS, dv // GLA_HEADS, dk // GLA_HEADS), F32)],
        compiler_params=_params(),
        name="gla_chunk",
    )(q, k, v, r, la, _row(norm_w))
    return _post(h, o, w_o, g_post)


def _nsa_head_perm():
    perm = np.zeros((D_MODEL,), np.int32)
    for p in range(NSA_HPG):
        for g in range(NSA_KV_GROUPS):
            src = (g * NSA_HPG + p) * NSA_HEAD_DIM
            dst = p * LANES + g * NSA_HEAD_DIM
            perm[dst:dst + NSA_HEAD_DIM] = np.arange(src, src + NSA_HEAD_DIM)
    return perm


def _nsa_proj_kernel(h_ref, gn_ref, wqt_ref, wkv_ref, wg_ref, qt_ref, kv_ref, gates_ref):
    xn = _rms(h_ref[0], gn_ref[...], NORM_EPS).astype(BF16)
    qt_ref[0] = _dot_nt(wqt_ref[...], xn).astype(BF16)
    kv_ref[0] = _dot(xn, wkv_ref[...]).astype(BF16)
    gates_ref[0] = _sigmoid(_dot(xn, wg_ref[...]))


def _nsa_compress_kernel(r_ref, w1_ref, pe_ref, b1_ref, w2_ref, out_ref, *, n_cmp):
    r = r_ref[0, 0].astype(F32)
    rows = r.shape[0]
    pe = pe_ref[0]
    ra = (r + pe[0]).astype(BF16)
    rb = (pltpu.roll(r, rows - 1, 0) + pe[1]).astype(BF16)
    out = jnp.zeros((rows, LANES), F32)
    for g in range(NSA_KV_GROUPS):
        hid = _gelu_tanh(_dot(ra, w1_ref[0, g, 0]) + _dot(rb, w1_ref[0, g, 1]) + b1_ref[0])
        out = out + _dot(hid.astype(BF16), w2_ref[0, g])
    rowi = lax.broadcasted_iota(jnp.int32, out.shape, 0)
    out_ref[0, 0] = jnp.where(rowi < n_cmp, out, 0.0).astype(BF16)


def _query_chains(qt, heads_per_chain):
    row = lax.broadcasted_iota(jnp.int32, (LANES, qt.shape[1]), 0)
    zero = jnp.zeros((), qt.dtype)
    out = []
    for g in range(NSA_KV_GROUPS):
        keep = (row < NSA_HEAD_DIM) if g == 0 else (row >= NSA_HEAD_DIM)
        tiles = [jnp.where(keep, qt[p * LANES:(p + 1) * LANES, :], zero) for p in range(NSA_HPG)]
        out += [jnp.concatenate(tiles[c:c + heads_per_chain], axis=1)
                for c in range(0, NSA_HPG, heads_per_chain)]
    return out


def _ones_in_other_group(v, g):
    lane = lax.broadcasted_iota(jnp.int32, v.shape, 1)
    other = (lane >= NSA_HEAD_DIM) if g == 0 else (lane < NSA_HEAD_DIM)
    return jnp.where(other, jnp.ones((), v.dtype), v)


def _normalize_groups(acc0, acc1):
    lane = lax.broadcasted_iota(jnp.int32, acc0.shape, 1)
    half = NSA_HEAD_DIM
    return jnp.where(lane < half, acc0 / pltpu.roll(acc0, half, 1), acc1 / pltpu.roll(acc1, half, 1))


def _nsa_cmp_kernel(qt_ref, kc_ref, vc_ref, ov_ref, ocmp_ref, score_ref, *, n_cmp, n_slc):
    TQ = qt_ref.shape[2]
    ncp = kc_ref.shape[1]
    nbp = ov_ref.shape[0]
    G = NSA_KV_GROUPS
    R = NSA_HPG * TQ
    CW = 2 * TQ
    n_cw = R // CW
    chains = range(G * n_cw)
    i = pl.program_id(1)
    n_q = pl.num_programs(1)
    t0 = i * TQ

    def run(nr):
        kc = kc_ref[0, :nr, :]
        vc = vc_ref[0, :nr, :]
        ov = ov_ref[:, :nr]
        pos = t0 + lax.broadcasted_iota(jnp.int32, (1, TQ), 1)
        blk_c = lax.broadcasted_iota(jnp.int32, (nr, 1), 0)
        vis = (blk_c * CMP_STRIDE + (CMP_BLOCK - 1) <= pos) & (blk_c < n_cmp)
        bias = jnp.where(vis, 0.0, NEG_INF)
        bias = jnp.concatenate([bias] * (CW // TQ), axis=1)
        any_vis = (pos >= CMP_BLOCK - 1).astype(F32)
        any_vis = jnp.concatenate([any_vis] * (CW // TQ), axis=1)
        qch = _query_chains(qt_ref[0], CW // TQ)
        s = [_dot(kc, qch[j]) + bias for j in chains]
        e = [jnp.exp2(s[j] - jnp.max(s[j], axis=0, keepdims=True)) for j in chains]
        p = [e[j] * (any_vis / jnp.sum(e[j], axis=0, keepdims=True)) for j in chains]
        o = [_dot_tn(vc, p[j].astype(BF16)) for j in chains]

        blk_s = lax.broadcasted_iota(jnp.int32, (nbp, 1), 0)
        cur = lax.shift_right_logical(pos, 6)
        causal = (blk_s * SLC_BLOCK <= pos) & (blk_s < n_slc)
        forced = (blk_s == 0) | (blk_s == cur) | (blk_s == cur - 1)
        bonus = np.float32(FORCE_BONUS) * forced.astype(F32)
        for g in range(G):
            psum = jnp.zeros((nr, TQ), F32)
            for j in range(g * n_cw, (g + 1) * n_cw):
                for c in range(CW // TQ):
                    psum = psum + p[j][:, c * TQ:(c + 1) * TQ]
            hi, lo = _split2(psum)
            imp = _dot(ov, hi) + _dot(ov, lo)
            score_ref[0, g * nbp:(g + 1) * nbp, :] = jnp.where(causal, imp + bonus, NEG_INF)

        half = NSA_HEAD_DIM
        for p_i in range(NSA_HPG):
            j, c = divmod(p_i, CW // TQ)
            cols = slice(c * TQ, (c + 1) * TQ)
            o_pair = jnp.concatenate([o[j][:half, cols], o[n_cw + j][half:, cols]], axis=0)
            ocmp_ref[0, :, p_i * LANES:(p_i + 1) * LANES] = o_pair.T

    parts = CMP_ROW_PARTS if ncp % (CMP_ROW_PARTS * LANES) == 0 else 1
    for k in range(parts):
        in_part = (i * parts >= k * n_q) & (i * parts < (k + 1) * n_q)
        pl.when(in_part)(functools.partial(run, (k + 1) * ncp // parts))


def _nsa_topk_kernel(score_ref, bias_ref, *, top_n):
    G = NSA_KV_GROUPS
    nbp = score_ref.shape[1] // G
    sc0 = score_ref[0].reshape(G, nbp, score_ref.shape[2])
    row = lax.broadcasted_iota(jnp.int32, sc0.shape, 1).astype(F32)

    def body(_, carry):
        sc, bias = carry
        m = jnp.max(sc, axis=1, keepdims=True)
        idx = jnp.min(jnp.where(sc == m, row, np.float32(nbp)), axis=1, keepdims=True)
        hit = row == idx
        return jnp.where(hit, np.float32(-3e38), sc), jnp.where(hit, 0.0, bias)

    _, bias = lax.fori_loop(0, top_n, body, (sc0, jnp.full(sc0.shape, NEG_INF, F32)))
    bias_ref[0] = bias.reshape(G * nbp, score_ref.shape[2])


def _nsa_attn_kernel(qt_ref, ks_ref, vs_ref, kw_ref, vw_ref, bias_ref, ocmp_ref, gates_ref, ge_ref,
                     o_ref, m_ref, acc_ref, *, win_rows):
    TQ = qt_ref.shape[2]
    S = ks_ref.shape[1]
    TK = min(NSA_TK, S)
    n_blk = TK // SLC_BLOCK
    nbp = bias_ref.shape[1] // NSA_KV_GROUPS
    R = NSA_HPG * TQ
    CW = 2 * TQ
    n_cw = R // CW
    t0 = pl.program_id(1) * TQ
    pos = t0 + lax.broadcasted_iota(jnp.int32, (1, TQ), 1)
    n_kt = (t0 + TQ + TK - 1) // TK
    key_s = lax.broadcasted_iota(jnp.int32, (TK, 1), 0)
    w_start = pl.multiple_of(jnp.maximum(t0 - WINDOW, 0), TQ)
    wkey = w_start + lax.broadcasted_iota(jnp.int32, (win_rows, 1), 0)
    wbias = jnp.where((wkey <= pos) & (wkey > pos - WINDOW), 0.0, NEG_INF)
    wbias = jnp.concatenate([wbias] * (CW // TQ), axis=1)

    G = NSA_KV_GROUPS
    chains = range(G * n_cw)
    qch = _query_chains(qt_ref[0], CW // TQ)

    def scores(k_tile, bias):
        return [_dot(k_tile, qch[j]) + bias[j // n_cw] for j in chains]

    m_ref[...] = jnp.full(m_ref.shape, NEG_INF, F32)
    acc_ref[...] = jnp.zeros_like(acc_ref)

    def key_tile(kt, diagonal):
        k0 = pl.multiple_of(kt * TK, TK)
        b0 = pl.multiple_of(kt * n_blk, n_blk)
        k_tile = ks_ref[0, pl.ds(k0, TK), :]
        s = []
        for g in range(G):
            blk_bias = bias_ref[0, pl.ds(g * nbp + b0, n_blk), :]
            blk_bias = jnp.concatenate([blk_bias] * (CW // TQ), axis=1).reshape(n_blk, 1, CW)
            if diagonal:
                cb = jnp.where(k0 + key_s <= pos, 0.0, NEG_INF)
                cb = jnp.concatenate([cb] * (CW // TQ), axis=1).reshape(n_blk, SLC_BLOCK, CW)
                blk_bias = blk_bias + cb
            for j in range(g * n_cw, (g + 1) * n_cw):
                sj = _dot(k_tile, qch[j]).reshape(n_blk, SLC_BLOCK, CW) + blk_bias
                s.append(sj.reshape(TK, CW))
        v_raw = vs_ref[0, pl.ds(k0, TK), :]
        v_tile = [_ones_in_other_group(v_raw, g) for g in range(G)]
        for g in range(G):
            grp = range(g * n_cw, (g + 1) * n_cw)
            m_old = {j: m_ref[:, j * CW:(j + 1) * CW] for j in grp}
            m_new = {j: jnp.maximum(m_old[j], jnp.max(s[j], axis=0, keepdims=True)) for j in grp}
            p = {j: jnp.exp2(s[j] - m_new[j]).astype(BF16) for j in grp}
            pv = {j: _dot_tn(v_tile[g], p[j]) for j in grp}
            for j in grp:
                cols = slice(j * CW, (j + 1) * CW)
                acc_ref[:, cols] = acc_ref[:, cols] * jnp.exp2(m_old[j] - m_new[j]) + pv[j]
                m_ref[:, cols] = m_new[j]

    def full_tile(kt, carry):
        key_tile(kt, diagonal=False)
        return carry

    lax.fori_loop(0, n_kt - 1, full_tile, 0)
    key_tile(n_kt - 1, diagonal=True)

    vw_raw = vw_ref[0, pl.ds(w_start, win_rows), :]
    vw = [_ones_in_other_group(vw_raw, g) for g in range(G)]
    s = scores(kw_ref[0, pl.ds(w_start, win_rows), :], [wbias] * G)
    p = [jnp.exp2(s[j] - jnp.max(s[j], axis=0, keepdims=True)).astype(BF16) for j in chains]
    acc_win = jnp.concatenate([_dot_tn(vw[j // n_cw], p[j]) for j in chains], axis=1)

    half = NSA_HEAD_DIM
    normed = {}
    for name, a in (("slc", acc_ref[...]), ("win", acc_win)):
        a0, a1 = a[:, :R], a[:, R:]
        normed[name] = jnp.concatenate([a0[:half] / a0[half:half + 1], a1[half:] / a1[0:1]], axis=0)
    gates = gates_ref[0]
    g_cmp = _dot_hilo(gates, ge_ref[0])
    g_slc = _dot_hilo(gates, ge_ref[1])
    g_win = _dot_hilo(gates, ge_ref[2])
    for p_i in range(NSA_HPG):
        cols = slice(p_i * LANES, (p_i + 1) * LANES)
        slc = normed["slc"][:, p_i * TQ:(p_i + 1) * TQ].T
        win = normed["win"][:, p_i * TQ:(p_i + 1) * TQ].T
        o_ref[0, :, cols] = (g_cmp[:, cols] * ocmp_ref[0, :, cols] + g_slc[:, cols] * slc
                             + g_win[:, cols] * win).astype(o_ref.dtype)


def _nsa_layer(h, g_pre, g_post, w_in, cmp_w1, cmp_b1, cmp_w2, cmp_pe, w_o):
    B, S, D = h.shape
    H, G, dh, hpg = NSA_HEADS, NSA_KV_GROUPS, NSA_HEAD_DIM, NSA_HPG
    perm = _nsa_head_perm()
    n_q = H * dh
    n_kv = 6 * G * dh
    wqt = (w_in[:, :n_q] * np.float32(dh ** -0.5 * np.log2(np.e)))[:, perm].T.astype(BF16)
    wkv = w_in[:, n_q:n_q + n_kv].astype(BF16)
    wgt = _pad_cols(w_in[:, n_q + n_kv:], LANES).astype(BF16)
    tm = min(PROJ_TM, S)
    qt_spec = lambda t: pl.BlockSpec((1, n_q, t), lambda b, i: (b, 0, i))
    qt, kv, gates = pl.pallas_call(
        _nsa_proj_kernel,
        grid=(B, S // tm),
        in_specs=[_tok_spec(tm, D), _const_spec((1, D)), _const_spec((n_q, D)),
                  _const_spec((D, n_kv)), _const_spec((D, LANES))],
        out_specs=[qt_spec(tm), _tok_spec(tm, n_kv), _tok_spec(tm, LANES)],
        out_shape=[jax.ShapeDtypeStruct((B, n_q, S), BF16), jax.ShapeDtypeStruct((B, S, n_kv), BF16),
                   jax.ShapeDtypeStruct((B, S, LANES), F32)],
        compiler_params=_params(),
        name="nsa_proj",
    )(h, _row(g_pre), wqt, wkv, wgt)

    n_cmp = (S - CMP_BLOCK) // CMP_STRIDE + 1
    ncp = S // CMP_STRIDE
    half = CMP_BLOCK // 2
    assert CMP_STRIDE == half
    rk = jnp.stack([kv[:, :, :LANES], kv[:, :, LANES:2 * LANES]]).reshape(2, B, ncp, half * LANES)
    w1 = cmp_w1.reshape(2, 2, half, dh, CMP_HIDDEN)
    w1x = jnp.zeros((2, G, 2, half, G, dh, CMP_HIDDEN), F32)
    for g in range(G):
        w1x = w1x.at[:, g, :, :, g].set(w1)
    w1x = w1x.reshape(2, G, 2, half * LANES, CMP_HIDDEN).astype(BF16)
    pe = jnp.broadcast_to(cmp_pe.reshape(2, 2, half, 1, dh), (2, 2, half, G, dh))
    pe = pe.reshape(2, 2, 1, half * LANES).astype(F32)
    w2x = jnp.zeros((2, G, CMP_HIDDEN, G, dh), F32)
    for g in range(G):
        w2x = w2x.at[:, g, :, g].set(cmp_w2)
    w2x = w2x.reshape(2, G, CMP_HIDDEN, LANES).astype(BF16)
    kvc = pl.pallas_call(
        functools.partial(_nsa_compress_kernel, n_cmp=n_cmp),
        grid=(2, B),
        in_specs=[pl.BlockSpec((1, 1, ncp, half * LANES), lambda c, b: (c, b, 0, 0)),
                  pl.BlockSpec((1, G, 2, half * LANES, CMP_HIDDEN), lambda c, b: (c, 0, 0, 0, 0)),
                  pl.BlockSpec((1, 2, 1, half * LANES), lambda c, b: (c, 0, 0, 0)),
                  pl.BlockSpec((1, 1, CMP_HIDDEN), lambda c, b: (c, 0, 0)),
                  pl.BlockSpec((1, G, CMP_HIDDEN, LANES), lambda c, b: (c, 0, 0, 0))],
        out_specs=pl.BlockSpec((1, 1, ncp, LANES), lambda c, b: (c, b, 0, 0)),
        out_shape=jax.ShapeDtypeStruct((2, B, ncp, LANES), BF16),
        compiler_params=_params(),
        name="nsa_compress",
    )(rk, w1x, pe, cmp_b1.reshape(2, 1, CMP_HIDDEN).astype(F32), w2x)

    n_slc = S // SLC_BLOCK
    top_n = min(SLC_TOPK, n_slc)
    nbp = max(LANES, n_slc)
    start = np.arange(ncp) * CMP_STRIDE
    end = start + CMP_BLOCK - 1
    s_start = np.arange(nbp) * SLC_BLOCK
    s_end = s_start + SLC_BLOCK - 1
    overlap = ((end[:, None] >= s_start[None, :]) & (start[:, None] <= s_end[None, :])
               & (np.arange(ncp)[:, None] < n_cmp) & (np.arange(nbp)[None, :] < n_slc))
    overlap_t = jnp.asarray(overlap.astype(np.float32).T, BF16)
    TQ = min(NSA_TQ, S)
    blk_tok_spec = lambda t: pl.BlockSpec((1, G * nbp, t), lambda b, i: (b, 0, i))
    o_cmp, score = pl.pallas_call(
        functools.partial(_nsa_cmp_kernel, n_cmp=n_cmp, n_slc=n_slc),
        grid=(B, S // TQ),
        in_specs=[qt_spec(TQ),
                  pl.BlockSpec((None, 1, ncp, LANES), lambda b, i: (0, b, 0, 0)),
                  pl.BlockSpec((None, 1, ncp, LANES), lambda b, i: (1, b, 0, 0)),
                  _const_spec((nbp, ncp))],
        out_specs=[_tok_spec(TQ, n_q), blk_tok_spec(TQ)],
        out_shape=[jax.ShapeDtypeStruct((B, S, n_q), F32),
                   jax.ShapeDtypeStruct((B, G * nbp, S), F32)],
        compiler_params=_params(),
        name="nsa_compressed_attn",
    )(qt, kvc, kvc, overlap_t)
    tr = min(NSA_TOPK_TOKENS, S)
    sel_bias = pl.pallas_call(
        functools.partial(_nsa_topk_kernel, top_n=top_n),
        grid=(B, S // tr),
        in_specs=[blk_tok_spec(tr)],
        out_specs=blk_tok_spec(tr),
        out_shape=jax.ShapeDtypeStruct((B, G * nbp, S), F32),
        compiler_params=_params(),
        name="nsa_topk",
    )(score)

    ge = np.zeros((3, LANES, n_q), np.float32)
    for g in range(G):
        for p in range(hpg):
            for j in range(3):
                ge[j, g * hpg * 3 + p * 3 + j, p * LANES + g * dh:p * LANES + (g + 1) * dh] = 1.0
    ge = jnp.asarray(ge, BF16)
    win_rows = min(WINDOW + TQ, S)
    kv_spec = lambda c: pl.BlockSpec((1, S, LANES), lambda b, i, c=c: (b, 0, c))
    o = pl.pallas_call(
        functools.partial(_nsa_attn_kernel, win_rows=win_rows),
        grid=(B, S // TQ),
        in_specs=[qt_spec(TQ), kv_spec(2), kv_spec(3), kv_spec(4), kv_spec(5),
                  blk_tok_spec(TQ), _tok_spec(TQ, n_q), _tok_spec(TQ, LANES),
                  _const_spec((3, LANES, n_q))],
        out_specs=_tok_spec(TQ, n_q),
        out_shape=jax.ShapeDtypeStruct((B, S, n_q), BF16),
        scratch_shapes=[pltpu.VMEM((1, G * hpg * TQ), F32), pltpu.VMEM((LANES, G * hpg * TQ), F32)],
        compiler_params=_params(),
        name="nsa_selected_window_attn",
    )(qt, kv, kv, kv, kv, sel_bias, o_cmp, gates, ge)
    return _post(h, o, w_o[perm, :], g_post)


def kernel(x, norm_g, ffn_w_in, ffn_conv_w, ffn_conv_b, ffn_w_out, rwkv_mu, rwkv_w_rkv, rwkv_w0, rwkv_w1, rwkv_w2, rwkv_a0, rwkv_a1, rwkv_a2, rwkv_g1, rwkv_g2, rwkv_k_k, rwkv_k_a, rwkv_r_k, rwkv_ln_w, rwkv_ln_b, rwkv_w_o, rwkv_v0, rwkv_v1, rwkv_v2, nsa_w_in, nsa_cmp_w1, nsa_cmp_b1, nsa_cmp_w2, nsa_cmp_pe, nsa_w_o, gla_w_in, gla_a_up, gla_a_bias, gla_norm_w, gla_w_o):
    depth = norm_g.shape[0]
    h = x
    v_first = None
    for i in range(depth):
        kind, j = i % 3, i // 3
        if kind == 0:
            vres = None if j == 0 else (rwkv_v0[j - 1], rwkv_v1[j - 1], rwkv_v2[j - 1])
            h, v_first = _rwkv_layer(
                h, norm_g[i, 0], norm_g[i, 1], rwkv_mu[j], rwkv_w_rkv[j], rwkv_w0[j], rwkv_w1[j],
                rwkv_w2[j], rwkv_a0[j], rwkv_a1[j], rwkv_a2[j], rwkv_g1[j], rwkv_g2[j], rwkv_k_k[j],
                rwkv_k_a[j], rwkv_r_k[j], rwkv_ln_w[j], rwkv_ln_b[j], rwkv_w_o[j], v_first, vres)
        elif kind == 1:
            h = _nsa_layer(h, norm_g[i, 0], norm_g[i, 1], nsa_w_in[j], nsa_cmp_w1[j], nsa_cmp_b1[j],
                           nsa_cmp_w2[j], nsa_cmp_pe[j], nsa_w_o[j])
        else:
            h = _gla_layer(h, norm_g[i, 0], norm_g[i, 1], gla_w_in[j], gla_a_up[j], gla_a_bias[j],
                           gla_norm_w[j], gla_w_o[j])
        h = _ffn(h, norm_g[i, 2], ffn_w_in[i], ffn_conv_w[i], ffn_conv_b[i], ffn_w_out[i],
                 norm_g[i, 3])
    return h
```

```python
import functools

import numpy as np
import jax
import jax.numpy as jnp
from jax import lax
from jax.experimental import pallas as pl
from jax.experimental.pallas import tpu as pltpu

F32 = jnp.float32
BF16 = jnp.bfloat16

V7X_VMEM_BYTES = 64 * 1024 * 1024
VMEM_LIMIT_BYTES = V7X_VMEM_BYTES - 8 * 1024 * 1024
LANES = 128
SUBLANES = 8

D_MODEL = 1024
NORM_EPS = 1e-6
NEG_INF = -1e30

RWKV_HEAD_DIM = 64
RWKV_HEADS = D_MODEL // RWKV_HEAD_DIM
RWKV_GN_EPS = 64e-5
RWKV_CHUNK = 64
RWKV_CHUNKS_PER_STEP = 2
LORA_PAD = 128

NSA_HEADS = 16
NSA_KV_GROUPS = 2
NSA_HEAD_DIM = D_MODEL // NSA_HEADS
NSA_HPG = NSA_HEADS // NSA_KV_GROUPS
CMP_BLOCK = 32
CMP_STRIDE = 16
CMP_HIDDEN = 256
CMP_ROW_PARTS = 4
SLC_BLOCK = 64
SLC_TOPK = 16
WINDOW = 512
FORCE_BONUS = 1e4
NSA_TQ = 256
NSA_CHAIN_COLS = 256
NSA_TK = 512
NSA_TOPK_TOKENS = 512

GLA_HEADS = 4
GLA_KEY_DIM = D_MODEL // 2
GLA_VAL_DIM = D_MODEL
GLA_GATE_LORA = 16
GLA_TAU = 16.0
GLA_CHUNK = 64
GLA_NORM_EPS = 1e-5

D_FF = 2816
FFN_CHUNK = 256
FFN_TM = 512
PROJ_TM = 256
POST_TM = 512


def _dot(a, b):
    return jnp.dot(a, b, preferred_element_type=F32)


def _dot_nt(a, b):
    return lax.dot_general(a, b, (((1,), (1,)), ((), ())), preferred_element_type=F32)


def _dot_tn(a, b):
    return lax.dot_general(a, b, (((0,), (0,)), ((), ())), preferred_element_type=F32)


def _split2(x):
    hi = x.astype(BF16)
    lo = (x - hi.astype(F32)).astype(BF16)
    return hi, lo


def _dot_hilo(x, m):
    hi, lo = _split2(x)
    return _dot(hi, m) + _dot(lo, m)


def _dot_tri3(tri, x):
    hi = x.astype(BF16)
    r1 = x - hi.astype(F32)
    mid = r1.astype(BF16)
    lo = (r1 - mid.astype(F32)).astype(BF16)
    return _dot(tri, hi) + _dot(tri, mid) + _dot(tri, lo)


def _head_sum(x, red_ref, exp_ref):
    return _dot_hilo(_dot(x.astype(BF16), red_ref[...]), exp_ref[...])


def _rms(x, g, eps):
    return x * lax.rsqrt(jnp.mean(x * x, axis=-1, keepdims=True) + eps) * g


def _sigmoid(x):
    return 1.0 / (1.0 + jnp.exp(-x))


def _softplus(x):
    return jnp.maximum(x, 0.0) + jnp.log(1.0 + jnp.exp(-jnp.abs(x)))


def _gelu_tanh(x):
    c = np.float32(np.sqrt(2.0 / np.pi))
    return 0.5 * x * (1.0 + jnp.tanh(c * (x + 0.044715 * (x * x * x))))


def _shift_rows(x, n, prev8):
    rolled = pltpu.roll(x, n, 0)
    row = lax.broadcasted_iota(jnp.int32, x.shape, 0)
    for j in range(n):
        rolled = jnp.where(row == j, prev8[SUBLANES - n + j:SUBLANES - n + j + 1, :], rolled)
    return rolled


def _const_spec(shape):
    nd = len(shape)
    return pl.BlockSpec(shape, lambda *_: (0,) * nd, pipeline_mode=pl.Buffered(1))


def _tok_spec(tm, width):
    return pl.BlockSpec((1, tm, width), lambda b, i: (b, i, 0))


def _params():
    return pltpu.CompilerParams(dimension_semantics=("arbitrary", "arbitrary"),
                                vmem_limit_bytes=VMEM_LIMIT_BYTES)


def _row(v):
    return v.reshape(1, -1).astype(F32)


def _ffn_kernel(h_ref, gpre_ref, win_ref, cw_ref, cb_ref, wo_ref, gpost_ref, out_ref,
                xn_ref, acc_ref, halo_ref, *, fc):
    @pl.when(pl.program_id(1) == 0)
    def _():
        halo_ref[...] = jnp.zeros_like(halo_ref)

    h = h_ref[0]
    xn_ref[...] = _rms(h, gpre_ref[...], NORM_EPS).astype(BF16)
    acc_ref[...] = jnp.zeros_like(acc_ref)
    tm = h.shape[0]
    F = wo_ref.shape[0]
    nc = F // fc

    def gate_up(c):
        xn = xn_ref[...]
        return (_dot(xn, win_ref[:, c * fc:(c + 1) * fc]),
                _dot(xn, win_ref[:, F + c * fc:F + (c + 1) * fc]))

    nxt = gate_up(0)
    for c in range(nc):
        cols = slice(c * fc, (c + 1) * fc)
        gate, up = nxt
        if c + 1 < nc:
            nxt = gate_up(c + 1)
        prev = halo_ref[:, cols]
        g1 = _shift_rows(gate, 1, prev)
        g2 = _shift_rows(gate, 2, prev)
        cw = cw_ref[:, cols]
        z = cw[2:3] * gate + cw[1:2] * g1 + cw[0:1] * g2 + cb_ref[:, cols]
        halo_ref[:, cols] = gate[tm - SUBLANES:tm, :]
        act = (z * _sigmoid(z) * up).astype(BF16)
        acc_ref[...] += _dot(act, wo_ref[cols, :])
    out_ref[0] = h + _rms(acc_ref[...], gpost_ref[...], NORM_EPS)


def _ffn(h, g_pre, w_in, conv_w, conv_b, w_out, g_post):
    B, S, D = h.shape
    F = w_out.shape[0]
    assert F % FFN_CHUNK == 0
    tm = min(FFN_TM, S)
    return pl.pallas_call(
        functools.partial(_ffn_kernel, fc=FFN_CHUNK),
        grid=(B, S // tm),
        in_specs=[_tok_spec(tm, D), _const_spec((1, D)), _const_spec((D, 2 * F)),
                  _const_spec((3, F)), _const_spec((1, F)), _const_spec((F, D)),
                  _const_spec((1, D))],
        out_specs=_tok_spec(tm, D),
        out_shape=jax.ShapeDtypeStruct((B, S, D), F32),
        scratch_shapes=[pltpu.VMEM((tm, D), BF16), pltpu.VMEM((tm, D), F32),
                        pltpu.VMEM((SUBLANES, F), F32)],
        compiler_params=_params(),
        name="conv_ffn",
    )(h, _row(g_pre), w_in.astype(BF16), conv_w.astype(F32), _row(conv_b), w_out.astype(BF16),
      _row(g_post))


def _post_kernel(h_ref, o_ref, wo_ref, g_ref, out_ref):
    y = _dot(o_ref[0].astype(BF16), wo_ref[...])
    out_ref[0] = h_ref[0] + _rms(y, g_ref[...], NORM_EPS)


def _post(h, o, w_o, g_post):
    B, S, D = h.shape
    K = o.shape[-1]
    tm = min(POST_TM, S)
    return pl.pallas_call(
        _post_kernel,
        grid=(B, S // tm),
        in_specs=[_tok_spec(tm, D), _tok_spec(tm, K), _const_spec((K, D)), _const_spec((1, D))],
        out_specs=_tok_spec(tm, D),
        out_shape=jax.ShapeDtypeStruct((B, S, D), F32),
        compiler_params=_params(),
        name="mixer_out",
    )(h, o, w_o.astype(BF16), _row(g_post))


def _rwkv_proj_kernel(*refs, has_vres):
    if has_vres:
        (h_ref, gn_ref, mu_ref, wr_ref, wk_ref, wv_ref, w1_ref, w2_ref, a1_ref, a2_ref, g1_ref,
         g2_ref, vec_ref, red_ref, exp_ref, vf_ref, v1_ref, v2_ref,
         r_ref, ld_ref, k_ref, v_ref, kkn_ref, b_ref, g_ref, bonus_ref, carry_ref) = refs
    else:
        (h_ref, gn_ref, mu_ref, wr_ref, wk_ref, wv_ref, w1_ref, w2_ref, a1_ref, a2_ref, g1_ref,
         g2_ref, vec_ref, red_ref, exp_ref,
         r_ref, ld_ref, k_ref, v_ref, kkn_ref, b_ref, g_ref, bonus_ref, carry_ref) = refs

    @pl.when(pl.program_id(1) == 0)
    def _():
        carry_ref[...] = jnp.zeros_like(carry_ref)

    x = _rms(h_ref[0], gn_ref[...], NORM_EPS)
    tm = x.shape[0]
    xx = _shift_rows(x, 1, carry_ref[...]) - x
    carry_ref[...] = x[tm - SUBLANES:tm, :]
    mu = mu_ref[...]
    xr, xw, xk, xv, xa, xg = ((x + xx * mu[c:c + 1]).astype(BF16) for c in range(6))
    vec = vec_ref[...]
    w0, a0, k_k, k_a, r_k, v0 = (vec[c:c + 1] for c in range(6))

    r = _dot(xr, wr_ref[...])
    k = _dot(xk, wk_ref[...])
    v = _dot(xv, wv_ref[...])
    wl = -_softplus(-(w0 + _dot(jnp.tanh(_dot(xw, w1_ref[...])).astype(BF16), w2_ref[...]))) - 0.5
    ld_ref[0] = -jnp.exp(wl)
    a = _sigmoid(a0 + _dot(_dot(xa, a1_ref[...]).astype(BF16), a2_ref[...]))
    g_ref[0] = _dot(_sigmoid(_dot(xg, g1_ref[...])).astype(BF16), g2_ref[...])
    if has_vres:
        mix = _sigmoid(v0 + _dot(_dot(xv, v1_ref[...]).astype(BF16), v2_ref[...]))
        v = v + (vf_ref[0] - v) * mix
    kk = k * k_k
    norm = jnp.sqrt(_head_sum(kk * kk, red_ref, exp_ref))
    kkn = kk / jnp.maximum(norm, 1e-12)
    k = k * (1.0 + (a - 1.0) * k_a)
    r_ref[0] = r
    k_ref[0] = k
    v_ref[0] = v
    kkn_ref[0] = kkn
    b_ref[0] = kkn * a
    bonus_ref[0] = _head_sum(r * k * r_k, red_ref, exp_ref) * v


def _pad_cols(w, n):
    return jnp.pad(w, ((0, 0), (0, n - w.shape[1])))


def _pad_rows(w, n):
    return jnp.pad(w, ((0, n - w.shape[0]), (0, 0)))


def _head_reduce_expand(d, hd, value):
    red = (np.arange(d)[:, None] // hd == np.arange(LANES)[None, :]).astype(np.float32)
    return jnp.asarray(red, BF16), jnp.asarray(red.T * value, BF16)


def _rwkv_proj(h, g_norm, mu, w_rkv, w0, w1, w2, a0, a1, a2, g1, g2, k_k, k_a, r_k, v_first, vres):
    B, S, D = h.shape
    tm = min(PROJ_TM, S)
    has_vres = vres is not None
    v0 = vres[0] if has_vres else jnp.zeros((D,), F32)
    vec = jnp.stack([w0, a0, k_k, k_a, r_k.reshape(-1), v0]).astype(F32)
    red, exp = _head_reduce_expand(D, RWKV_HEAD_DIM, 1.0)
    lora_in = lambda w: _pad_cols(w, LORA_PAD).astype(BF16)
    lora_out = lambda w: _pad_rows(w, LORA_PAD).astype(BF16)
    args = [h, _row(g_norm), mu, w_rkv[0].astype(BF16), w_rkv[1].astype(BF16), w_rkv[2].astype(BF16),
            lora_in(w1), lora_out(w2), lora_in(a1), lora_out(a2), lora_in(g1), lora_out(g2), vec, red, exp]
    specs = [_tok_spec(tm, D), _const_spec((1, D)), _const_spec((6, D)), _const_spec((D, D)),
             _const_spec((D, D)), _const_spec((D, D)), _const_spec((D, LORA_PAD)),
             _const_spec((LORA_PAD, D)), _const_spec((D, LORA_PAD)), _const_spec((LORA_PAD, D)),
             _const_spec((D, LORA_PAD)), _const_spec((LORA_PAD, D)), _const_spec((6, D)),
             _const_spec((D, LANES)), _const_spec((LANES, D))]
    if has_vres:
        args += [v_first, lora_in(vres[1]), lora_out(vres[2])]
        specs += [_tok_spec(tm, D), _const_spec((D, LORA_PAD)), _const_spec((LORA_PAD, D))]
    outs = pl.pallas_call(
        functools.partial(_rwkv_proj_kernel, has_vres=has_vres),
        grid=(B, S // tm),
        in_specs=specs,
        out_specs=[_tok_spec(tm, D)] * 8,
        out_shape=[jax.ShapeDtypeStruct((B, S, D), F32)] * 8,
        scratch_shapes=[pltpu.VMEM((SUBLANES, D), F32)],
        compiler_params=_params(),
        name="rwkv_proj",
    )(*args)
    return outs


def _rwkv_scan_kernel(r_ref, ld_ref, k_ref, v_ref, kkn_ref, b_ref, y_ref, st_ref):
    @pl.when(pl.program_id(1) == 0)
    def _():
        st_ref[...] = jnp.zeros_like(st_ref)

    C = RWKV_CHUNK
    N = RWKV_HEAD_DIM
    assert 2 * N == LANES and C == N
    rowi = lax.broadcasted_iota(jnp.int32, (C, C), 0)
    coli = lax.broadcasted_iota(jnp.int32, (C, C), 1)
    tri = (rowi >= coli).astype(BF16)
    prow = lax.broadcasted_iota(jnp.int32, (C, LANES), 0)
    pcol = lax.broadcasted_iota(jnp.int32, (C, LANES), 1) & (N - 1)
    low_incl = prow >= pcol
    low_strict = prow > pcol
    eye = (prow == pcol).astype(F32)
    srow = lax.broadcasted_iota(jnp.int32, (LANES, LANES), 0)
    scol = lax.broadcasted_iota(jnp.int32, (LANES, LANES), 1)
    same_head = (srow < N) == (scol < N)

    def bdiag(x):
        lane = lax.broadcasted_iota(jnp.int32, x.shape, 1)
        zero = jnp.zeros((), x.dtype)
        return jnp.concatenate([jnp.where(lane < N, x, zero), jnp.where(lane >= N, x, zero)], axis=0)

    n_chunks = r_ref.shape[1] // C
    n_pairs = RWKV_HEADS // 2
    sls = [slice(p * LANES, (p + 1) * LANES) for p in range(n_pairs)]
    a_t, r_t, b_t, k_t, b_h, k_h, vb, p_all = ([] for _ in range(8))
    for ci in range(n_chunks):
        rows = pl.ds(ci * C, C)
        ld = ld_ref[0, rows, :]
        cum = _dot_tri3(tri, ld)
        cum_last = cum[C - 1:C, :]
        p_inv = jnp.exp(-cum)
        p_rem = jnp.exp(cum_last - cum)
        p_all.append(jnp.exp(cum_last))
        b = b_ref[0, rows, :]
        k = k_ref[0, rows, :]
        a_t.append((-kkn_ref[0, rows, :] * jnp.exp(cum - ld)).astype(BF16))
        r_t.append((r_ref[0, rows, :] * jnp.exp(cum)).astype(BF16))
        b_t.append((b * p_inv).astype(BF16))
        k_t.append((k * p_inv).astype(BF16))
        b_h.append((b * p_rem).astype(BF16))
        k_h.append((k * p_rem).astype(BF16))
        vb.append(v_ref[0, rows, :].astype(BF16))

    units = [(ci, p) for ci in range(n_chunks) for p in range(n_pairs)]
    idx = range(len(units))
    tile = lambda arr, u: arr[units[u][0]][:, sls[units[u][1]]]
    ar = [jnp.concatenate([tile(a_t, u), tile(r_t, u)], axis=0) for u in idx]
    s_b = [_dot_nt(ar[u], bdiag(tile(b_t, u))) for u in idx]
    s_k = [_dot_nt(ar[u], bdiag(tile(k_t, u))) for u in idx]
    a_ab = [jnp.where(low_strict, s_b[u][:C], 0.0) for u in idx]
    a_rb = [jnp.where(low_incl, s_b[u][C:], 0.0).astype(BF16) for u in idx]
    a_ak = [jnp.where(low_strict, s_k[u][:C], 0.0).astype(BF16) for u in idx]
    a_rk = [jnp.where(low_incl, s_k[u][C:], 0.0).astype(BF16) for u in idx]
    av = [_dot(jnp.concatenate([a_ak[u], a_rk[u]], axis=0), bdiag(tile(vb, u))) for u in idx]
    akv = [av[u][:C].astype(BF16) for u in idx]
    pw = [a_ab[u].astype(BF16) for u in idx]
    tinv = [eye + a_ab[u] for u in idx]
    pw = [_dot(pw[u], bdiag(pw[u])).astype(BF16) for u in idx]
    span = 4
    while span < C:
        both = [_dot(jnp.concatenate([pw[u], tinv[u].astype(BF16)], axis=0), bdiag(pw[u]))
                for u in idx]
        tinv = [tinv[u] + both[u][C:] for u in idx]
        pw = [both[u][:C].astype(BF16) for u in idx]
        span *= 2
    tinv = [tinv[u] + _dot(tinv[u].astype(BF16), bdiag(pw[u])) for u in idx]
    tb = [tinv[u].astype(BF16) for u in idx]
    a_hat = [_dot(tb[u], bdiag(tile(a_t, u))).astype(BF16) for u in idx]
    u_loc = [_dot(tb[u], bdiag(akv[u])) for u in idx]

    pairs = range(n_pairs)
    st = [st_ref[p] for p in pairs]
    for ci in range(n_chunks):
        u0 = ci * n_pairs
        on_state = [_dot_nt(jnp.concatenate([a_hat[u0 + p], r_t[ci][:, sls[p]]], axis=0),
                            st[p].astype(BF16)) for p in pairs]
        ub = [(on_state[p][:C] + u_loc[u0 + p]).astype(BF16) for p in pairs]
        for p in pairs:
            y_ref[0, pl.ds(ci * C, C), sls[p]] = (on_state[p][C:] + _dot(a_rb[u0 + p], bdiag(ub[p]))
                                                  + av[u0 + p][C:])
        upd = [_dot_tn(jnp.concatenate([ub[p], vb[ci][:, sls[p]]], axis=0),
                       jnp.concatenate([b_h[ci][:, sls[p]], k_h[ci][:, sls[p]]], axis=0))
               for p in pairs]
        st = [st[p] * p_all[ci][:, sls[p]] + jnp.where(same_head, upd[p], 0.0) for p in pairs]
    for p in pairs:
        st_ref[p] = st[p]


def _rwkv_scan(r, ld, k, v, kkn, b):
    B, S, D = r.shape
    C = RWKV_CHUNK * RWKV_CHUNKS_PER_STEP
    return pl.pallas_call(
        _rwkv_scan_kernel,
        grid=(B, S // C),
        in_specs=[_tok_spec(C, D)] * 6,
        out_specs=_tok_spec(C, D),
        out_shape=jax.ShapeDtypeStruct((B, S, D), F32),
        scratch_shapes=[pltpu.VMEM((RWKV_HEADS // 2, LANES, LANES), F32)],
        compiler_params=_params(),
        name="rwkv_scan",
    )(r, ld, k, v, kkn, b)


def _rwkv_post_kernel(h_ref, y_ref, bonus_ref, g_ref, vec_ref, red_ref, avg_ref, wo_ref, gpost_ref,
                      out_ref):
    y = y_ref[0]
    d = y - _head_sum(y, red_ref, avg_ref)
    var = _head_sum(d * d, red_ref, avg_ref)
    vec = vec_ref[...]
    yn = d * lax.rsqrt(var + RWKV_GN_EPS) * vec[0:1] + vec[1:2]
    o = ((yn + bonus_ref[0]) * g_ref[0]).astype(BF16)
    out_ref[0] = h_ref[0] + _rms(_dot(o, wo_ref[...]), gpost_ref[...], NORM_EPS)


def _rwkv_post(h, y, bonus, g, ln_w, ln_b, w_o, g_post):
    B, S, D = h.shape
    tm = min(PROJ_TM, S)
    vec = jnp.stack([ln_w, ln_b]).astype(F32)
    red, avg = _head_reduce_expand(D, RWKV_HEAD_DIM, 1.0 / RWKV_HEAD_DIM)
    return pl.pallas_call(
        _rwkv_post_kernel,
        grid=(B, S // tm),
        in_specs=[_tok_spec(tm, D)] * 4 + [_const_spec((2, D)), _const_spec((D, LANES)),
                                           _const_spec((LANES, D)), _const_spec((D, D)),
                                           _const_spec((1, D))],
        out_specs=_tok_spec(tm, D),
        out_shape=jax.ShapeDtypeStruct((B, S, D), F32),
        compiler_params=_params(),
        name="rwkv_out",
    )(h, y, bonus, g, vec, red, avg, w_o.astype(BF16), _row(g_post))


def _rwkv_layer(h, g_pre, g_post, mu, w_rkv, w0, w1, w2, a0, a1, a2, g1, g2, k_k, k_a, r_k,
                ln_w, ln_b, w_o, v_first, vres):
    r, ld, k, v, kkn, b, g, bonus = _rwkv_proj(h, g_pre, mu, w_rkv, w0, w1, w2, a0, a1, a2, g1, g2,
                                               k_k, k_a, r_k, v_first, vres)
    if vres is None:
        v_first = v
    y = _rwkv_scan(r, ld, k, v, kkn, b)
    return _rwkv_post(h, y, bonus, g, ln_w, ln_b, w_o, g_post), v_first


def _gla_proj_kernel(h_ref, gn_ref, w_ref, wa_ref, aup_ref, abias_ref,
                     q_ref, k_ref, v_ref, r_ref, la_ref):
    xn = _rms(h_ref[0], gn_ref[...], NORM_EPS).astype(BF16)
    qkvr = _dot(xn, w_ref[...])
    dk = GLA_KEY_DIM
    dv = GLA_VAL_DIM
    q_ref[0] = qkvr[:, :dk] * np.float32((dk // GLA_HEADS) ** -0.5)
    k_ref[0] = qkvr[:, dk:2 * dk]
    v_ref[0] = qkvr[:, 2 * dk:2 * dk + dv]
    r_ref[0] = qkvr[:, 2 * dk + dv:]
    z = _dot(_dot(xn, wa_ref[...]).astype(BF16), aup_ref[...]) + abias_ref[...]
    la_ref[0] = -_softplus(-z) * np.float32(1.0 / GLA_TAU)


def _gla_chunk_kernel(q_ref, k_ref, v_ref, r_ref, la_ref, nw_ref, o_ref, st_ref, *, chunks):
    @pl.when(pl.program_id(1) == 0)
    def _():
        st_ref[...] = jnp.zeros_like(st_ref)

    C = GLA_CHUNK
    dk = GLA_KEY_DIM // GLA_HEADS
    dv = GLA_VAL_DIM // GLA_HEADS
    rowi = lax.broadcasted_iota(jnp.int32, (C, C), 0)
    coli = lax.broadcasted_iota(jnp.int32, (C, C), 1)
    low_incl = rowi >= coli
    tri = low_incl.astype(BF16)
    nw = nw_ref[...]

    for ci in range(chunks):
        rows = pl.ds(ci * C, C)
        bcum = _dot_tri3(tri, la_ref[0, rows, :])
        b_last = bcum[C - 1:C, :]
        k = k_ref[0, rows, :]
        qe = (q_ref[0, rows, :] * jnp.exp(bcum)).astype(BF16)
        ke = (k * jnp.exp(-bcum)).astype(BF16)
        kd = (k * jnp.exp(b_last - bcum)).astype(BF16)
        dec = jnp.exp(b_last)
        vb = v_ref[0, rows, :].astype(BF16)
        r = r_ref[0, rows, :]
        for h in range(GLA_HEADS):
            ks = slice(h * dk, (h + 1) * dk)
            vs = slice(h * dv, (h + 1) * dv)
            a = jnp.where(low_incl, _dot_nt(qe[:, ks], ke[:, ks]), 0.0)
            st = st_ref[h]
            o = _dot(a.astype(BF16), vb[:, vs]) + _dot_nt(qe[:, ks], st.astype(BF16))
            st_ref[h] = st * dec[:, ks] + _dot_tn(vb[:, vs], kd[:, ks])
            o = o * lax.rsqrt(jnp.mean(o * o, axis=-1, keepdims=True) + GLA_NORM_EPS) * nw
            rh = r[:, vs]
            o_ref[0, rows, vs] = (o * (rh * _sigmoid(rh))).astype(o_ref.dtype)


def _gla_layer(h, g_pre, g_post, w_in, a_up, a_bias, norm_w, w_o):
    B, S, D = h.shape
    dk, dv = GLA_KEY_DIM, GLA_VAL_DIM
    tm = min(PROJ_TM, S)
    n_main = 2 * dk + 2 * dv
    w_main = w_in[:, :n_main].astype(BF16)
    w_a = _pad_cols(w_in[:, n_main:], LORA_PAD).astype(BF16)
    aup = _pad_rows(a_up, LORA_PAD).astype(BF16)
    q, k, v, r, la = pl.pallas_call(
        _gla_proj_kernel,
        grid=(B, S // tm),
        in_specs=[_tok_spec(tm, D), _const_spec((1, D)), _const_spec((D, n_main)),
                  _const_spec((D, LORA_PAD)), _const_spec((LORA_PAD, dk)), _const_spec((1, dk))],
        out_specs=[_tok_spec(tm, dk), _tok_spec(tm, dk), _tok_spec(tm, dv), _tok_spec(tm, dv),
                   _tok_spec(tm, dk)],
        out_shape=[jax.ShapeDtypeStruct((B, S, w), F32) for w in (dk, dk, dv, dv, dk)],
        compiler_params=_params(),
        name="gla_proj",
    )(h, _row(g_pre), w_main, w_a, aup, _row(a_bias))

    chunks = 4 if S % (4 * GLA_CHUNK) == 0 else 1
    tc = chunks * GLA_CHUNK
    o = pl.pallas_call(
        functools.partial(_gla_chunk_kernel, chunks=chunks),
        grid=(B, S // tc),
        in_specs=[_tok_spec(tc, dk), _tok_spec(tc, dk), _tok_spec(tc, dv), _tok_spec(tc, dv),
                  _tok_spec(tc, dk), _const_spec((1, dv // GLA_HEADS))],
        out_specs=_tok_spec(tc, dv),
        out_shape=jax.ShapeDtypeStruct((B, S, dv), BF16),
        scratch_shapes=[pltpu.VMEM((GLA_HEADS, dv // GLA_HEADS, dk // GLA_HEADS), F32)],
        compiler_params=_params(),
        name="gla_chunk",
    )(q, k, v, r, la, _row(norm_w))
    return _post(h, o, w_o, g_post)


def _nsa_head_perm():
    perm = np.zeros((D_MODEL,), np.int32)
    for p in range(NSA_HPG):
        for g in range(NSA_KV_GROUPS):
            src = (g * NSA_HPG + p) * NSA_HEAD_DIM
            dst = p * LANES + g * NSA_HEAD_DIM
            perm[dst:dst + NSA_HEAD_DIM] = np.arange(src, src + NSA_HEAD_DIM)
    return perm


def _nsa_proj_kernel(h_ref, gn_ref, wqt_ref, wkv_ref, wg_ref, qt_ref, kv_ref, gates_ref):
    xn = _rms(h_ref[0], gn_ref[...], NORM_EPS).astype(BF16)
    qt_ref[0] = _dot_nt(wqt_ref[...], xn).astype(BF16)
    kv_ref[0] = _dot(xn, wkv_ref[...]).astype(BF16)
    gates_ref[0] = _sigmoid(_dot(xn, wg_ref[...]))


def _nsa_compress_kernel(r_ref, w1_ref, pe_ref, b1_ref, w2_ref, out_ref, *, n_cmp):
    r = r_ref[0, 0].astype(F32)
    rows = r.shape[0]
    pe = pe_ref[0]
    ra = (r + pe[0]).astype(BF16)
    rb = (pltpu.roll(r, rows - 1, 0) + pe[1]).astype(BF16)
    out = jnp.zeros((rows, LANES), F32)
    for g in range(NSA_KV_GROUPS):
        hid = _gelu_tanh(_dot(ra, w1_ref[0, g, 0]) + _dot(rb, w1_ref[0, g, 1]) + b1_ref[0])
        out = out + _dot(hid.astype(BF16), w2_ref[0, g])
    rowi = lax.broadcasted_iota(jnp.int32, out.shape, 0)
    out_ref[0, 0] = jnp.where(rowi < n_cmp, out, 0.0).astype(BF16)


def _query_chains(qt, heads_per_chain):
    row = lax.broadcasted_iota(jnp.int32, (LANES, qt.shape[1]), 0)
    zero = jnp.zeros((), qt.dtype)
    out = []
    for g in range(NSA_KV_GROUPS):
        keep = (row < NSA_HEAD_DIM) if g == 0 else (row >= NSA_HEAD_DIM)
        tiles = [jnp.where(keep, qt[p * LANES:(p + 1) * LANES, :], zero) for p in range(NSA_HPG)]
        out += [jnp.concatenate(tiles[c:c + heads_per_chain], axis=1)
                for c in range(0, NSA_HPG, heads_per_chain)]
    return out


def _ones_in_other_group(v, g):
    lane = lax.broadcasted_iota(jnp.int32, v.shape, 1)
    other = (lane >= NSA_HEAD_DIM) if g == 0 else (lane < NSA_HEAD_DIM)
    return jnp.where(other, jnp.ones((), v.dtype), v)


def _normalize_groups(acc0, acc1):
    lane = lax.broadcasted_iota(jnp.int32, acc0.shape, 1)
    half = NSA_HEAD_DIM
    return jnp.where(lane < half, acc0 / pltpu.roll(acc0, half, 1), acc1 / pltpu.roll(acc1, half, 1))


def _nsa_cmp_kernel(qt_ref, kc_ref, vc_ref, ov_ref, ocmp_ref, score_ref, *, n_cmp, n_slc):
    TQ = qt_ref.shape[2]
    ncp = kc_ref.shape[1]
    nbp = ov_ref.shape[0]
    G = NSA_KV_GROUPS
    R = NSA_HPG * TQ
    CW = NSA_CHAIN_COLS
    n_cw = R // CW
    chains = range(G * n_cw)
    i = pl.program_id(1)
    n_q = pl.num_programs(1)
    t0 = i * TQ

    def run(nr):
        kc = kc_ref[0, :nr, :]
        vc = vc_ref[0, :nr, :]
        ov = ov_ref[:, :nr]
        pos = t0 + lax.broadcasted_iota(jnp.int32, (1, TQ), 1)
        blk_c = lax.broadcasted_iota(jnp.int32, (nr, 1), 0)
        vis = (blk_c * CMP_STRIDE + (CMP_BLOCK - 1) <= pos) & (blk_c < n_cmp)
        bias = jnp.where(vis, 0.0, NEG_INF)
        bias = jnp.concatenate([bias] * (CW // TQ), axis=1)
        any_vis = (pos >= CMP_BLOCK - 1).astype(F32)
        any_vis = jnp.concatenate([any_vis] * (CW // TQ), axis=1)
        qch = _query_chains(qt_ref[0], CW // TQ)
        s = [_dot(kc, qch[j]) + bias for j in chains]
        e = [jnp.exp2(s[j] - jnp.max(s[j], axis=0, keepdims=True)) for j in chains]
        p = [e[j] * (any_vis / jnp.sum(e[j], axis=0, keepdims=True)) for j in chains]
        o = [_dot_tn(vc, p[j].astype(BF16)) for j in chains]

        blk_s = lax.broadcasted_iota(jnp.int32, (nbp, 1), 0)
        cur = lax.shift_right_logical(pos, 6)
        causal = (blk_s * SLC_BLOCK <= pos) & (blk_s < n_slc)
        forced = (blk_s == 0) | (blk_s == cur) | (blk_s == cur - 1)
        bonus = np.float32(FORCE_BONUS) * forced.astype(F32)
        for g in range(G):
            psum = jnp.zeros((nr, TQ), F32)
            for j in range(g * n_cw, (g + 1) * n_cw):
                for c in range(CW // TQ):
                    psum = psum + p[j][:, c * TQ:(c + 1) * TQ]
            hi, lo = _split2(psum)
            imp = _dot(ov, hi) + _dot(ov, lo)
            score_ref[0, g * nbp:(g + 1) * nbp, :] = jnp.where(causal, imp + bonus, NEG_INF)

        half = NSA_HEAD_DIM
        for p_i in range(NSA_HPG):
            j, c = divmod(p_i, CW // TQ)
            cols = slice(c * TQ, (c + 1) * TQ)
            o_pair = jnp.concatenate([o[j][:half, cols], o[n_cw + j][half:, cols]], axis=0)
            ocmp_ref[0, :, p_i * LANES:(p_i + 1) * LANES] = o_pair.T

    parts = CMP_ROW_PARTS if ncp % (CMP_ROW_PARTS * LANES) == 0 else 1
    for k in range(parts):
        in_part = (i * parts >= k * n_q) & (i * parts < (k + 1) * n_q)
        pl.when(in_part)(functools.partial(run, (k + 1) * ncp // parts))


def _nsa_topk_kernel(score_ref, bias_ref, *, top_n):
    G = NSA_KV_GROUPS
    nbp = score_ref.shape[1] // G
    sc0 = score_ref[0].reshape(G, nbp, score_ref.shape[2])
    row = lax.broadcasted_iota(jnp.int32, sc0.shape, 1).astype(F32)

    def body(_, carry):
        sc, bias = carry
        m = jnp.max(sc, axis=1, keepdims=True)
        idx = jnp.min(jnp.where(sc == m, row, np.float32(nbp)), axis=1, keepdims=True)
        hit = row == idx
        return jnp.where(hit, np.float32(-3e38), sc), jnp.where(hit, 0.0, bias)

    _, bias = lax.fori_loop(0, top_n, body, (sc0, jnp.full(sc0.shape, NEG_INF, F32)))
    bias_ref[0] = bias.reshape(G * nbp, score_ref.shape[2])


def _nsa_attn_kernel(qt_ref, ks_ref, vs_ref, kw_ref, vw_ref, bias_ref, ocmp_ref, gates_ref, ge_ref,
                     o_ref, m_ref, acc_ref, *, win_rows):
    TQ = qt_ref.shape[2]
    S = ks_ref.shape[1]
    TK = min(NSA_TK, S)
    n_blk = TK // SLC_BLOCK
    nbp = bias_ref.shape[1] // NSA_KV_GROUPS
    R = NSA_HPG * TQ
    CW = NSA_CHAIN_COLS
    n_cw = R // CW
    t0 = pl.program_id(1) * TQ
    pos = t0 + lax.broadcasted_iota(jnp.int32, (1, TQ), 1)
    n_kt = (t0 + TQ + TK - 1) // TK
    key_s = lax.broadcasted_iota(jnp.int32, (TK, 1), 0)
    w_start = pl.multiple_of(jnp.maximum(t0 - WINDOW, 0), TQ)
    wkey = w_start + lax.broadcasted_iota(jnp.int32, (win_rows, 1), 0)
    wbias = jnp.where((wkey <= pos) & (wkey > pos - WINDOW), 0.0, NEG_INF)
    wbias = jnp.concatenate([wbias] * (CW // TQ), axis=1)

    G = NSA_KV_GROUPS
    chains = range(G * n_cw)
    qch = _query_chains(qt_ref[0], CW // TQ)

    def scores(k_tile, bias):
        return [_dot(k_tile, qch[j]) + bias[j // n_cw] for j in chains]

    m_ref[...] = jnp.full(m_ref.shape, NEG_INF, F32)
    acc_ref[...] = jnp.zeros_like(acc_ref)

    def key_tile(kt, diagonal):
        k0 = pl.multiple_of(kt * TK, TK)
        b0 = pl.multiple_of(kt * n_blk, n_blk)
        k_tile = ks_ref[0, pl.ds(k0, TK), :]
        s = []
        for g in range(G):
            blk_bias = bias_ref[0, pl.ds(g * nbp + b0, n_blk), :]
            blk_bias = jnp.concatenate([blk_bias] * (CW // TQ), axis=1).reshape(n_blk, 1, CW)
            if diagonal:
                cb = jnp.where(k0 + key_s <= pos, 0.0, NEG_INF)
                cb = jnp.concatenate([cb] * (CW // TQ), axis=1).reshape(n_blk, SLC_BLOCK, CW)
                blk_bias = blk_bias + cb
            for j in range(g * n_cw, (g + 1) * n_cw):
                sj = _dot(k_tile, qch[j]).reshape(n_blk, SLC_BLOCK, CW) + blk_bias
                s.append(sj.reshape(TK, CW))
        v_raw = vs_ref[0, pl.ds(k0, TK), :]
        v_tile = [_ones_in_other_group(v_raw, g) for g in range(G)]
        for g in range(G):
            grp = range(g * n_cw, (g + 1) * n_cw)
            m_old = {j: m_ref[:, j * CW:(j + 1) * CW] for j in grp}
            m_new = {j: jnp.maximum(m_old[j], jnp.max(s[j], axis=0, keepdims=True)) for j in grp}
            p = {j: jnp.exp2(s[j] - m_new[j]).astype(BF16) for j in grp}
            pv = {j: _dot_tn(v_tile[g], p[j]) for j in grp}
            for j in grp:
                cols = slice(j * CW, (j + 1) * CW)
                acc_ref[:, cols] = acc_ref[:, cols] * jnp.exp2(m_old[j] - m_new[j]) + pv[j]
                m_ref[:, cols] = m_new[j]

    def full_tile(kt, carry):
        key_tile(kt, diagonal=False)
        return carry

    lax.fori_loop(0, n_kt - 1, full_tile, 0)
    key_tile(n_kt - 1, diagonal=True)

    vw_raw = vw_ref[0, pl.ds(w_start, win_rows), :]
    vw = [_ones_in_other_group(vw_raw, g) for g in range(G)]
    s = scores(kw_ref[0, pl.ds(w_start, win_rows), :], [wbias] * G)
    p = [jnp.exp2(s[j] - jnp.max(s[j], axis=0, keepdims=True)).astype(BF16) for j in chains]
    acc_win = jnp.concatenate([_dot_tn(vw[j // n_cw], p[j]) for j in chains], axis=1)

    half = NSA_HEAD_DIM
    normed = {}
    for name, a in (("slc", acc_ref[...]), ("win", acc_win)):
        a0, a1 = a[:, :R], a[:, R:]
        normed[name] = jnp.concatenate([a0[:half] / a0[half:half + 1], a1[half:] / a1[0:1]], axis=0)
    gates = gates_ref[0]
    g_cmp = _dot_hilo(gates, ge_ref[0])
    g_slc = _dot_hilo(gates, ge_ref[1])
    g_win = _dot_hilo(gates, ge_ref[2])
    for p_i in range(NSA_HPG):
        cols = slice(p_i * LANES, (p_i + 1) * LANES)
        slc = normed["slc"][:, p_i * TQ:(p_i + 1) * TQ].T
        win = normed["win"][:, p_i * TQ:(p_i + 1) * TQ].T
        o_ref[0, :, cols] = (g_cmp[:, cols] * ocmp_ref[0, :, cols] + g_slc[:, cols] * slc
                             + g_win[:, cols] * win).astype(o_ref.dtype)


def _nsa_layer(h, g_pre, g_post, w_in, cmp_w1, cmp_b1, cmp_w2, cmp_pe, w_o):
    B, S, D = h.shape
    H, G, dh, hpg = NSA_HEADS, NSA_KV_GROUPS, NSA_HEAD_DIM, NSA_HPG
    perm = _nsa_head_perm()
    n_q = H * dh
    n_kv = 6 * G * dh
    wqt = (w_in[:, :n_q] * np.float32(dh ** -0.5 * np.log2(np.e)))[:, perm].T.astype(BF16)
    wkv = w_in[:, n_q:n_q + n_kv].astype(BF16)
    wgt = _pad_cols(w_in[:, n_q + n_kv:], LANES).astype(BF16)
    tm = min(PROJ_TM, S)
    qt_spec = lambda t: pl.BlockSpec((1, n_q, t), lambda b, i: (b, 0, i))
    qt, kv, gates = pl.pallas_call(
        _nsa_proj_kernel,
        grid=(B, S // tm),
        in_specs=[_tok_spec(tm, D), _const_spec((1, D)), _const_spec((n_q, D)),
                  _const_spec((D, n_kv)), _const_spec((D, LANES))],
        out_specs=[qt_spec(tm), _tok_spec(tm, n_kv), _tok_spec(tm, LANES)],
        out_shape=[jax.ShapeDtypeStruct((B, n_q, S), BF16), jax.ShapeDtypeStruct((B, S, n_kv), BF16),
                   jax.ShapeDtypeStruct((B, S, LANES), F32)],
        compiler_params=_params(),
        name="nsa_proj",
    )(h, _row(g_pre), wqt, wkv, wgt)

    n_cmp = (S - CMP_BLOCK) // CMP_STRIDE + 1
    ncp = S // CMP_STRIDE
    half = CMP_BLOCK // 2
    assert CMP_STRIDE == half
    rk = jnp.stack([kv[:, :, :LANES], kv[:, :, LANES:2 * LANES]]).reshape(2, B, ncp, half * LANES)
    w1 = cmp_w1.reshape(2, 2, half, dh, CMP_HIDDEN)
    w1x = jnp.zeros((2, G, 2, half, G, dh, CMP_HIDDEN), F32)
    for g in range(G):
        w1x = w1x.at[:, g, :, :, g].set(w1)
    w1x = w1x.reshape(2, G, 2, half * LANES, CMP_HIDDEN).astype(BF16)
    pe = jnp.broadcast_to(cmp_pe.reshape(2, 2, half, 1, dh), (2, 2, half, G, dh))
    pe = pe.reshape(2, 2, 1, half * LANES).astype(F32)
    w2x = jnp.zeros((2, G, CMP_HIDDEN, G, dh), F32)
    for g in range(G):
        w2x = w2x.at[:, g, :, g].set(cmp_w2)
    w2x = w2x.reshape(2, G, CMP_HIDDEN, LANES).astype(BF16)
    kvc = pl.pallas_call(
        functools.partial(_nsa_compress_kernel, n_cmp=n_cmp),
        grid=(2, B),
        in_specs=[pl.BlockSpec((1, 1, ncp, half * LANES), lambda c, b: (c, b, 0, 0)),
                  pl.BlockSpec((1, G, 2, half * LANES, CMP_HIDDEN), lambda c, b: (c, 0, 0, 0, 0)),
                  pl.BlockSpec((1, 2, 1, half * LANES), lambda c, b: (c, 0, 0, 0)),
                  pl.BlockSpec((1, 1, CMP_HIDDEN), lambda c, b: (c, 0, 0)),
                  pl.BlockSpec((1, G, CMP_HIDDEN, LANES), lambda c, b: (c, 0, 0, 0))],
        out_specs=pl.BlockSpec((1, 1, ncp, LANES), lambda c, b: (c, b, 0, 0)),
        out_shape=jax.ShapeDtypeStruct((2, B, ncp, LANES), BF16),
        compiler_params=_params(),
        name="nsa_compress",
    )(rk, w1x, pe, cmp_b1.reshape(2, 1, CMP_HIDDEN).astype(F32), w2x)

    n_slc = S // SLC_BLOCK
    top_n = min(SLC_TOPK, n_slc)
    nbp = max(LANES, n_slc)
    start = np.arange(ncp) * CMP_STRIDE
    end = start + CMP_BLOCK - 1
    s_start = np.arange(nbp) * SLC_BLOCK
    s_end = s_start + SLC_BLOCK - 1
    overlap = ((end[:, None] >= s_start[None, :]) & (start[:, None] <= s_end[None, :])
               & (np.arange(ncp)[:, None] < n_cmp) & (np.arange(nbp)[None, :] < n_slc))
    overlap_t = jnp.asarray(overlap.astype(np.float32).T, BF16)
    TQ = min(NSA_TQ, S)
    blk_tok_spec = lambda t: pl.BlockSpec((1, G * nbp, t), lambda b, i: (b, 0, i))
    o_cmp, score = pl.pallas_call(
        functools.partial(_nsa_cmp_kernel, n_cmp=n_cmp, n_slc=n_slc),
        grid=(B, S // TQ),
        in_specs=[qt_spec(TQ),
                  pl.BlockSpec((None, 1, ncp, LANES), lambda b, i: (0, b, 0, 0)),
                  pl.BlockSpec((None, 1, ncp, LANES), lambda b, i: (1, b, 0, 0)),
                  _const_spec((nbp, ncp))],
        out_specs=[_tok_spec(TQ, n_q), blk_tok_spec(TQ)],
        out_shape=[jax.ShapeDtypeStruct((B, S, n_q), F32),
                   jax.ShapeDtypeStruct((B, G * nbp, S), F32)],
        compiler_params=_params(),
        name="nsa_compressed_attn",
    )(qt, kvc, kvc, overlap_t)
    tr = min(NSA_TOPK_TOKENS, S)
    sel_bias = pl.pallas_call(
        functools.partial(_nsa_topk_kernel, top_n=top_n),
        grid=(B, S // tr),
        in_specs=[blk_tok_spec(tr)],
        out_specs=blk_tok_spec(tr),
        out_shape=jax.ShapeDtypeStruct((B, G * nbp, S), F32),
        compiler_params=_params(),
        name="nsa_topk",
    )(score)

    ge = np.zeros((3, LANES, n_q), np.float32)
    for g in range(G):
        for p in range(hpg):
            for j in range(3):
                ge[j, g * hpg * 3 + p * 3 + j, p * LANES + g * dh:p * LANES + (g + 1) * dh] = 1.0
    ge = jnp.asarray(ge, BF16)
    win_rows = min(WINDOW + TQ, S)
    kv_spec = lambda c: pl.BlockSpec((1, S, LANES), lambda b, i, c=c: (b, 0, c))
    o = pl.pallas_call(
        functools.partial(_nsa_attn_kernel, win_rows=win_rows),
        grid=(B, S // TQ),
        in_specs=[qt_spec(TQ), kv_spec(2), kv_spec(3), kv_spec(4), kv_spec(5),
                  blk_tok_spec(TQ), _tok_spec(TQ, n_q), _tok_spec(TQ, LANES),
                  _const_spec((3, LANES, n_q))],
        out_specs=_tok_spec(TQ, n_q),
        out_shape=jax.ShapeDtypeStruct((B, S, n_q), BF16),
        scratch_shapes=[pltpu.VMEM((1, G * hpg * TQ), F32), pltpu.VMEM((LANES, G * hpg * TQ), F32)],
        compiler_params=_params(),
        name="nsa_selected_window_attn",
    )(qt, kv, kv, kv, kv, sel_bias, o_cmp, gates, ge)
    return _post(h, o, w_o[perm, :], g_post)


def kernel(x, norm_g, ffn_w_in, ffn_conv_w, ffn_conv_b, ffn_w_out, rwkv_mu, rwkv_w_rkv, rwkv_w0, rwkv_w1, rwkv_w2, rwkv_a0, rwkv_a1, rwkv_a2, rwkv_g1, rwkv_g2, rwkv_k_k, rwkv_k_a, rwkv_r_k, rwkv_ln_w, rwkv_ln_b, rwkv_w_o, rwkv_v0, rwkv_v1, rwkv_v2, nsa_w_in, nsa_cmp_w1, nsa_cmp_b1, nsa_cmp_w2, nsa_cmp_pe, nsa_w_o, gla_w_in, gla_a_up, gla_a_bias, gla_norm_w, gla_w_o):
    depth = norm_g.shape[0]
    h = x
    v_first = None
    for i in range(depth):
        kind, j = i % 3, i // 3
        if kind == 0:
            vres = None if j == 0 else (rwkv_v0[j - 1], rwkv_v1[j - 1], rwkv_v2[j - 1])
            h, v_first = _rwkv_layer(
                h, norm_g[i, 0], norm_g[i, 1], rwkv_mu[j], rwkv_w_rkv[j], rwkv_w0[j], rwkv_w1[j],
                rwkv_w2[j], rwkv_a0[j], rwkv_a1[j], rwkv_a2[j], rwkv_g1[j], rwkv_g2[j], rwkv_k_k[j],
                rwkv_k_a[j], rwkv_r_k[j], rwkv_ln_w[j], rwkv_ln_b[j], rwkv_w_o[j], v_first, vres)
        elif kind == 1:
            h = _nsa_layer(h, norm_g[i, 0], norm_g[i, 1], nsa_w_in[j], nsa_cmp_w1[j], nsa_cmp_b1[j],
                           nsa_cmp_w2[j], nsa_cmp_pe[j], nsa_w_o[j])
        else:
            h = _gla_layer(h, norm_g[i, 0], norm_g[i, 1], gla_w_in[j], gla_a_up[j], gla_a_bias[j],
                           gla_norm_w[j], gla_w_o[j])
        h = _ffn(h, norm_g[i, 2], ffn_w_in[i], ffn_conv_w[i], ffn_conv_b[i], ffn_w_out[i],
                 norm_g[i, 3])
    return h
```

```python
import functools

import numpy as np
import jax
import jax.numpy as jnp
from jax import lax
from jax.experimental import pallas as pl
from jax.experimental.pallas import tpu as pltpu

F32 = jnp.float32
BF16 = jnp.bfloat16

V7X_VMEM_BYTES = 64 * 1024 * 1024
VMEM_LIMIT_BYTES = V7X_VMEM_BYTES - 8 * 1024 * 1024
LANES = 128
SUBLANES = 8

D_MODEL = 1024
NORM_EPS = 1e-6
NEG_INF = -1e30

RWKV_HEAD_DIM = 64
RWKV_HEADS = D_MODEL // RWKV_HEAD_DIM
RWKV_GN_EPS = 64e-5
RWKV_CHUNK = 64
RWKV_CHUNKS_PER_STEP = 2
LORA_PAD = 128

NSA_HEADS = 16
NSA_KV_GROUPS = 2
NSA_HEAD_DIM = D_MODEL // NSA_HEADS
NSA_HPG = NSA_HEADS // NSA_KV_GROUPS
CMP_BLOCK = 32
CMP_STRIDE = 16
CMP_HIDDEN = 256
CMP_ROW_PARTS = 4
SLC_BLOCK = 64
SLC_TOPK = 16
WINDOW = 512
FORCE_BONUS = 1e4
NSA_TQ = 256
NSA_CHAIN_COLS = 256
NSA_TK = 512
NSA_TOPK_TOKENS = 512

GLA_HEADS = 4
GLA_KEY_DIM = D_MODEL // 2
GLA_VAL_DIM = D_MODEL
GLA_TAU = 16.0
GLA_CHUNK = 64
GLA_NORM_EPS = 1e-5

FFN_CHUNK = 256
FFN_TM = 512
PROJ_TM = 256
POST_TM = 512


def _dot(a, b):
    return jnp.dot(a, b, preferred_element_type=F32)


def _dot_nt(a, b):
    return lax.dot_general(a, b, (((1,), (1,)), ((), ())), preferred_element_type=F32)


def _dot_tn(a, b):
    return lax.dot_general(a, b, (((0,), (0,)), ((), ())), preferred_element_type=F32)


def _split2(x):
    hi = x.astype(BF16)
    lo = (x - hi.astype(F32)).astype(BF16)
    return hi, lo


def _dot_hilo(x, m):
    hi, lo = _split2(x)
    return _dot(hi, m) + _dot(lo, m)


def _dot_tri3(tri, x):
    hi = x.astype(BF16)
    r1 = x - hi.astype(F32)
    mid = r1.astype(BF16)
    lo = (r1 - mid.astype(F32)).astype(BF16)
    return _dot(tri, hi) + _dot(tri, mid) + _dot(tri, lo)


def _head_sum(x, red_ref, exp_ref):
    return _dot_hilo(_dot(x.astype(BF16), red_ref[...]), exp_ref[...])


def _rms(x, g, eps):
    return x * lax.rsqrt(jnp.mean(x * x, axis=-1, keepdims=True) + eps) * g


def _sigmoid(x):
    return 1.0 / (1.0 + jnp.exp(-x))


def _softplus(x):
    return jnp.maximum(x, 0.0) + jnp.log(1.0 + jnp.exp(-jnp.abs(x)))


def _gelu_tanh(x):
    c = np.float32(np.sqrt(2.0 / np.pi))
    return 0.5 * x * (1.0 + jnp.tanh(c * (x + 0.044715 * (x * x * x))))


def _shift_rows(x, n, prev8):
    rolled = pltpu.roll(x, n, 0)
    row = lax.broadcasted_iota(jnp.int32, x.shape, 0)
    for j in range(n):
        rolled = jnp.where(row == j, prev8[SUBLANES - n + j:SUBLANES - n + j + 1, :], rolled)
    return rolled


def _const_spec(shape):
    nd = len(shape)
    return pl.BlockSpec(shape, lambda *_: (0,) * nd, pipeline_mode=pl.Buffered(1))


def _tok_spec(tm, width):
    return pl.BlockSpec((1, tm, width), lambda b, i: (b, i, 0))


def _params():
    return pltpu.CompilerParams(dimension_semantics=("arbitrary", "arbitrary"),
                                vmem_limit_bytes=VMEM_LIMIT_BYTES)


def _row(v):
    return v.reshape(1, -1).astype(F32)


def _ffn_kernel(h_ref, gpre_ref, win_ref, cw_ref, cb_ref, wo_ref, gpost_ref, out_ref,
                xn_ref, acc_ref, halo_ref, *, fc):
    @pl.when(pl.program_id(1) == 0)
    def _():
        halo_ref[...] = jnp.zeros_like(halo_ref)

    h = h_ref[0]
    xn_ref[...] = _rms(h, gpre_ref[...], NORM_EPS).astype(BF16)
    acc_ref[...] = jnp.zeros_like(acc_ref)
    tm = h.shape[0]
    F = wo_ref.shape[0]
    nc = F // fc

    def gate_up(c):
        xn = xn_ref[...]
        return (_dot(xn, win_ref[:, c * fc:(c + 1) * fc]),
                _dot(xn, win_ref[:, F + c * fc:F + (c + 1) * fc]))

    nxt = gate_up(0)
    for c in range(nc):
        cols = slice(c * fc, (c + 1) * fc)
        gate, up = nxt
        if c + 1 < nc:
            nxt = gate_up(c + 1)
        prev = halo_ref[:, cols]
        g1 = _shift_rows(gate, 1, prev)
        g2 = _shift_rows(gate, 2, prev)
        cw = cw_ref[:, cols]
        z = cw[2:3] * gate + cw[1:2] * g1 + cw[0:1] * g2 + cb_ref[:, cols]
        halo_ref[:, cols] = gate[tm - SUBLANES:tm, :]
        act = (z * _sigmoid(z) * up).astype(BF16)
        acc_ref[...] += _dot(act, wo_ref[cols, :])
    out_ref[0] = h + _rms(acc_ref[...], gpost_ref[...], NORM_EPS)


def _ffn(h, g_pre, w_in, conv_w, conv_b, w_out, g_post):
    B, S, D = h.shape
    F = w_out.shape[0]
    assert F % FFN_CHUNK == 0
    tm = min(FFN_TM, S)
    return pl.pallas_call(
        functools.partial(_ffn_kernel, fc=FFN_CHUNK),
        grid=(B, S // tm),
        in_specs=[_tok_spec(tm, D), _const_spec((1, D)), _const_spec((D, 2 * F)),
                  _const_spec((3, F)), _const_spec((1, F)), _const_spec((F, D)),
                  _const_spec((1, D))],
        out_specs=_tok_spec(tm, D),
        out_shape=jax.ShapeDtypeStruct((B, S, D), F32),
        scratch_shapes=[pltpu.VMEM((tm, D), BF16), pltpu.VMEM((tm, D), F32),
                        pltpu.VMEM((SUBLANES, F), F32)],
        compiler_params=_params(),
        name="conv_ffn",
    )(h, _row(g_pre), w_in.astype(BF16), conv_w.astype(F32), _row(conv_b), w_out.astype(BF16),
      _row(g_post))


def _post_kernel(h_ref, o_ref, wo_ref, g_ref, out_ref):
    y = _dot(o_ref[0].astype(BF16), wo_ref[...])
    out_ref[0] = h_ref[0] + _rms(y, g_ref[...], NORM_EPS)


def _post(h, o, w_o, g_post):
    B, S, D = h.shape
    K = o.shape[-1]
    tm = min(POST_TM, S)
    return pl.pallas_call(
        _post_kernel,
        grid=(B, S // tm),
        in_specs=[_tok_spec(tm, D), _tok_spec(tm, K), _const_spec((K, D)), _const_spec((1, D))],
        out_specs=_tok_spec(tm, D),
        out_shape=jax.ShapeDtypeStruct((B, S, D), F32),
        compiler_params=_params(),
        name="mixer_out",
    )(h, o, w_o.astype(BF16), _row(g_post))


def _rwkv_proj_kernel(*refs, has_vres):
    if has_vres:
        (h_ref, gn_ref, mu_ref, wr_ref, wk_ref, wv_ref, w1_ref, w2_ref, a1_ref, a2_ref, g1_ref,
         g2_ref, vec_ref, red_ref, exp_ref, vf_ref, v1_ref, v2_ref,
         r_ref, ld_ref, k_ref, v_ref, kkn_ref, b_ref, g_ref, bonus_ref, carry_ref) = refs
    else:
        (h_ref, gn_ref, mu_ref, wr_ref, wk_ref, wv_ref, w1_ref, w2_ref, a1_ref, a2_ref, g1_ref,
         g2_ref, vec_ref, red_ref, exp_ref,
         r_ref, ld_ref, k_ref, v_ref, kkn_ref, b_ref, g_ref, bonus_ref, carry_ref) = refs

    @pl.when(pl.program_id(1) == 0)
    def _():
        carry_ref[...] = jnp.zeros_like(carry_ref)

    x = _rms(h_ref[0], gn_ref[...], NORM_EPS)
    tm = x.shape[0]
    xx = _shift_rows(x, 1, carry_ref[...]) - x
    carry_ref[...] = x[tm - SUBLANES:tm, :]
    mu = mu_ref[...]
    xr, xw, xk, xv, xa, xg = ((x + xx * mu[c:c + 1]).astype(BF16) for c in range(6))
    vec = vec_ref[...]
    w0, a0, k_k, k_a, r_k, v0 = (vec[c:c + 1] for c in range(6))

    r = _dot(xr, wr_ref[...])
    k = _dot(xk, wk_ref[...])
    v = _dot(xv, wv_ref[...])
    wl = -_softplus(-(w0 + _dot(jnp.tanh(_dot(xw, w1_ref[...])).astype(BF16), w2_ref[...]))) - 0.5
    ld_ref[0] = -jnp.exp(wl)
    a = _sigmoid(a0 + _dot(_dot(xa, a1_ref[...]).astype(BF16), a2_ref[...]))
    g_ref[0] = _dot(_sigmoid(_dot(xg, g1_ref[...])).astype(BF16), g2_ref[...])
    if has_vres:
        mix = _sigmoid(v0 + _dot(_dot(xv, v1_ref[...]).astype(BF16), v2_ref[...]))
        v = v + (vf_ref[0] - v) * mix
    kk = k * k_k
    norm = jnp.sqrt(_head_sum(kk * kk, red_ref, exp_ref))
    kkn = kk / jnp.maximum(norm, 1e-12)
    k = k * (1.0 + (a - 1.0) * k_a)
    r_ref[0] = r
    k_ref[0] = k
    v_ref[0] = v
    kkn_ref[0] = kkn
    b_ref[0] = kkn * a
    bonus_ref[0] = _head_sum(r * k * r_k, red_ref, exp_ref) * v


def _pad_cols(w, n):
    return jnp.pad(w, ((0, 0), (0, n - w.shape[1])))


def _pad_rows(w, n):
    return jnp.pad(w, ((0, n - w.shape[0]), (0, 0)))


def _head_reduce_expand(d, hd, value):
    red = (np.arange(d)[:, None] // hd == np.arange(LANES)[None, :]).astype(np.float32)
    return jnp.asarray(red, BF16), jnp.asarray(red.T * value, BF16)


def _rwkv_proj(h, g_norm, mu, w_rkv, w0, w1, w2, a0, a1, a2, g1, g2, k_k, k_a, r_k, v_first, vres):
    B, S, D = h.shape
    tm = min(PROJ_TM, S)
    has_vres = vres is not None
    v0 = vres[0] if has_vres else jnp.zeros((D,), F32)
    vec = jnp.stack([w0, a0, k_k, k_a, r_k.reshape(-1), v0]).astype(F32)
    red, exp = _head_reduce_expand(D, RWKV_HEAD_DIM, 1.0)
    lora_in = lambda w: _pad_cols(w, LORA_PAD).astype(BF16)
    lora_out = lambda w: _pad_rows(w, LORA_PAD).astype(BF16)
    args = [h, _row(g_norm), mu, w_rkv[0].astype(BF16), w_rkv[1].astype(BF16), w_rkv[2].astype(BF16),
            lora_in(w1), lora_out(w2), lora_in(a1), lora_out(a2), lora_in(g1), lora_out(g2), vec, red, exp]
    specs = [_tok_spec(tm, D), _const_spec((1, D)), _const_spec((6, D)), _const_spec((D, D)),
             _const_spec((D, D)), _const_spec((D, D)), _const_spec((D, LORA_PAD)),
             _const_spec((LORA_PAD, D)), _const_spec((D, LORA_PAD)), _const_spec((LORA_PAD, D)),
             _const_spec((D, LORA_PAD)), _const_spec((LORA_PAD, D)), _const_spec((6, D)),
             _const_spec((D, LANES)), _const_spec((LANES, D))]
    if has_vres:
        args += [v_first, lora_in(vres[1]), lora_out(vres[2])]
        specs += [_tok_spec(tm, D), _const_spec((D, LORA_PAD)), _const_spec((LORA_PAD, D))]
    outs = pl.pallas_call(
        functools.partial(_rwkv_proj_kernel, has_vres=has_vres),
        grid=(B, S // tm),
        in_specs=specs,
        out_specs=[_tok_spec(tm, D)] * 8,
        out_shape=[jax.ShapeDtypeStruct((B, S, D), F32)] * 8,
        scratch_shapes=[pltpu.VMEM((SUBLANES, D), F32)],
        compiler_params=_params(),
        name="rwkv_proj",
    )(*args)
    return outs


def _rwkv_scan_kernel(r_ref, ld_ref, k_ref, v_ref, kkn_ref, b_ref, y_ref, st_ref):
    @pl.when(pl.program_id(1) == 0)
    def _():
        st_ref[...] = jnp.zeros_like(st_ref)

    C = RWKV_CHUNK
    N = RWKV_HEAD_DIM
    assert 2 * N == LANES and C == N
    rowi = lax.broadcasted_iota(jnp.int32, (C, C), 0)
    coli = lax.broadcasted_iota(jnp.int32, (C, C), 1)
    tri = (rowi >= coli).astype(BF16)
    prow = lax.broadcasted_iota(jnp.int32, (C, LANES), 0)
    pcol = lax.broadcasted_iota(jnp.int32, (C, LANES), 1) & (N - 1)
    low_incl = prow >= pcol
    low_strict = prow > pcol
    eye = (prow == pcol).astype(F32)
    srow = lax.broadcasted_iota(jnp.int32, (LANES, LANES), 0)
    scol = lax.broadcasted_iota(jnp.int32, (LANES, LANES), 1)
    same_head = (srow < N) == (scol < N)

    def bdiag(x):
        lane = lax.broadcasted_iota(jnp.int32, x.shape, 1)
        zero = jnp.zeros((), x.dtype)
        return jnp.concatenate([jnp.where(lane < N, x, zero), jnp.where(lane >= N, x, zero)], axis=0)

    n_chunks = r_ref.shape[1] // C
    n_pairs = RWKV_HEADS // 2
    sls = [slice(p * LANES, (p + 1) * LANES) for p in range(n_pairs)]
    a_t, r_t, b_t, k_t, b_h, k_h, vb, p_all = ([] for _ in range(8))
    for ci in range(n_chunks):
        rows = pl.ds(ci * C, C)
        ld = ld_ref[0, rows, :]
        cum = _dot_tri3(tri, ld)
        cum_last = cum[C - 1:C, :]
        p_inv = jnp.exp(-cum)
        p_rem = jnp.exp(cum_last - cum)
        p_all.append(jnp.exp(cum_last))
        b = b_ref[0, rows, :]
        k = k_ref[0, rows, :]
        a_t.append((-kkn_ref[0, rows, :] * jnp.exp(cum - ld)).astype(BF16))
        r_t.append((r_ref[0, rows, :] * jnp.exp(cum)).astype(BF16))
        b_t.append((b * p_inv).astype(BF16))
        k_t.append((k * p_inv).astype(BF16))
        b_h.append((b * p_rem).astype(BF16))
        k_h.append((k * p_rem).astype(BF16))
        vb.append(v_ref[0, rows, :].astype(BF16))

    units = [(ci, p) for ci in range(n_chunks) for p in range(n_pairs)]
    idx = range(len(units))
    tile = lambda arr, u: arr[units[u][0]][:, sls[units[u][1]]]
    ar = [jnp.concatenate([tile(a_t, u), tile(r_t, u)], axis=0) for u in idx]
    s_b = [_dot_nt(ar[u], bdiag(tile(b_t, u))) for u in idx]
    s_k = [_dot_nt(ar[u], bdiag(tile(k_t, u))) for u in idx]
    a_ab = [jnp.where(low_strict, s_b[u][:C], 0.0) for u in idx]
    a_rb = [jnp.where(low_incl, s_b[u][C:], 0.0).astype(BF16) for u in idx]
    a_ak = [jnp.where(low_strict, s_k[u][:C], 0.0).astype(BF16) for u in idx]
    a_rk = [jnp.where(low_incl, s_k[u][C:], 0.0).astype(BF16) for u in idx]
    av = [_dot(jnp.concatenate([a_ak[u], a_rk[u]], axis=0), bdiag(tile(vb, u))) for u in idx]
    akv = [av[u][:C].astype(BF16) for u in idx]
    pw = [a_ab[u].astype(BF16) for u in idx]
    tinv = [eye + a_ab[u] for u in idx]
    pw = [_dot(pw[u], bdiag(pw[u])).astype(BF16) for u in idx]
    span = 4
    while span < C:
        both = [_dot(jnp.concatenate([pw[u], tinv[u].astype(BF16)], axis=0), bdiag(pw[u]))
                for u in idx]
        tinv = [tinv[u] + both[u][C:] for u in idx]
        pw = [both[u][:C].astype(BF16) for u in idx]
        span *= 2
    tinv = [tinv[u] + _dot(tinv[u].astype(BF16), bdiag(pw[u])) for u in idx]
    tb = [tinv[u].astype(BF16) for u in idx]
    a_hat = [_dot(tb[u], bdiag(tile(a_t, u))).astype(BF16) for u in idx]
    u_loc = [_dot(tb[u], bdiag(akv[u])) for u in idx]

    pairs = range(n_pairs)
    st = [st_ref[p] for p in pairs]
    for ci in range(n_chunks):
        u0 = ci * n_pairs
        on_state = [_dot_nt(jnp.concatenate([a_hat[u0 + p], r_t[ci][:, sls[p]]], axis=0),
                            st[p].astype(BF16)) for p in pairs]
        ub = [(on_state[p][:C] + u_loc[u0 + p]).astype(BF16) for p in pairs]
        for p in pairs:
            y_ref[0, pl.ds(ci * C, C), sls[p]] = (on_state[p][C:] + _dot(a_rb[u0 + p], bdiag(ub[p]))
                                                  + av[u0 + p][C:])
        upd = [_dot_tn(jnp.concatenate([ub[p], vb[ci][:, sls[p]]], axis=0),
                       jnp.concatenate([b_h[ci][:, sls[p]], k_h[ci][:, sls[p]]], axis=0))
               for p in pairs]
        st = [st[p] * p_all[ci][:, sls[p]] + jnp.where(same_head, upd[p], 0.0) for p in pairs]
    for p in pairs:
        st_ref[p] = st[p]


def _rwkv_scan(r, ld, k, v, kkn, b):
    B, S, D = r.shape
    C = RWKV_CHUNK * RWKV_CHUNKS_PER_STEP
    return pl.pallas_call(
        _rwkv_scan_kernel,
        grid=(B, S // C),
        in_specs=[_tok_spec(C, D)] * 6,
        out_specs=_tok_spec(C, D),
        out_shape=jax.ShapeDtypeStruct((B, S, D), F32),
        scratch_shapes=[pltpu.VMEM((RWKV_HEADS // 2, LANES, LANES), F32)],
        compiler_params=_params(),
        name="rwkv_scan",
    )(r, ld, k, v, kkn, b)


def _rwkv_post_kernel(h_ref, y_ref, bonus_ref, g_ref, vec_ref, red_ref, avg_ref, wo_ref, gpost_ref,
                      out_ref):
    y = y_ref[0]
    d = y - _head_sum(y, red_ref, avg_ref)
    var = _head_sum(d * d, red_ref, avg_ref)
    vec = vec_ref[...]
    yn = d * lax.rsqrt(var + RWKV_GN_EPS) * vec[0:1] + vec[1:2]
    o = ((yn + bonus_ref[0]) * g_ref[0]).astype(BF16)
    out_ref[0] = h_ref[0] + _rms(_dot(o, wo_ref[...]), gpost_ref[...], NORM_EPS)


def _rwkv_post(h, y, bonus, g, ln_w, ln_b, w_o, g_post):
    B, S, D = h.shape
    tm = min(PROJ_TM, S)
    vec = jnp.stack([ln_w, ln_b]).astype(F32)
    red, avg = _head_reduce_expand(D, RWKV_HEAD_DIM, 1.0 / RWKV_HEAD_DIM)
    return pl.pallas_call(
        _rwkv_post_kernel,
        grid=(B, S // tm),
        in_specs=[_tok_spec(tm, D)] * 4 + [_const_spec((2, D)), _const_spec((D, LANES)),
                                           _const_spec((LANES, D)), _const_spec((D, D)),
                                           _const_spec((1, D))],
        out_specs=_tok_spec(tm, D),
        out_shape=jax.ShapeDtypeStruct((B, S, D), F32),
        compiler_params=_params(),
        name="rwkv_out",
    )(h, y, bonus, g, vec, red, avg, w_o.astype(BF16), _row(g_post))


def _rwkv_layer(h, g_pre, g_post, mu, w_rkv, w0, w1, w2, a0, a1, a2, g1, g2, k_k, k_a, r_k,
                ln_w, ln_b, w_o, v_first, vres):
    r, ld, k, v, kkn, b, g, bonus = _rwkv_proj(h, g_pre, mu, w_rkv, w0, w1, w2, a0, a1, a2, g1, g2,
                                               k_k, k_a, r_k, v_first, vres)
    if vres is None:
        v_first = v
    y = _rwkv_scan(r, ld, k, v, kkn, b)
    return _rwkv_post(h, y, bonus, g, ln_w, ln_b, w_o, g_post), v_first


def _gla_proj_kernel(h_ref, gn_ref, w_ref, wa_ref, aup_ref, abias_ref,
                     q_ref, k_ref, v_ref, r_ref, la_ref):
    xn = _rms(h_ref[0], gn_ref[...], NORM_EPS).astype(BF16)
    qkvr = _dot(xn, w_ref[...])
    dk = GLA_KEY_DIM
    dv = GLA_VAL_DIM
    q_ref[0] = qkvr[:, :dk] * np.float32((dk // GLA_HEADS) ** -0.5)
    k_ref[0] = qkvr[:, dk:2 * dk]
    v_ref[0] = qkvr[:, 2 * dk:2 * dk + dv]
    r_ref[0] = qkvr[:, 2 * dk + dv:]
    z = _dot(_dot(xn, wa_ref[...]).astype(BF16), aup_ref[...]) + abias_ref[...]
    la_ref[0] = -_softplus(-z) * np.float32(1.0 / GLA_TAU)


def _gla_chunk_kernel(q_ref, k_ref, v_ref, r_ref, la_ref, nw_ref, o_ref, st_ref, *, chunks):
    @pl.when(pl.program_id(1) == 0)
    def _():
        st_ref[...] = jnp.zeros_like(st_ref)

    C = GLA_CHUNK
    dk = GLA_KEY_DIM // GLA_HEADS
    dv = GLA_VAL_DIM // GLA_HEADS
    rowi = lax.broadcasted_iota(jnp.int32, (C, C), 0)
    coli = lax.broadcasted_iota(jnp.int32, (C, C), 1)
    low_incl = rowi >= coli
    tri = low_incl.astype(BF16)
    nw = nw_ref[...]

    heads = range(GLA_HEADS)
    kss = [slice(h * dk, (h + 1) * dk) for h in heads]
    vss = [slice(h * dv, (h + 1) * dv) for h in heads]
    qe, kd, dec, vb, o_intra, upd = [], [], [], [], [], []
    for ci in range(chunks):
        rows = pl.ds(ci * C, C)
        bcum = _dot_tri3(tri, la_ref[0, rows, :])
        b_last = bcum[C - 1:C, :]
        k = k_ref[0, rows, :]
        qe.append((q_ref[0, rows, :] * jnp.exp(bcum)).astype(BF16))
        ke = (k * jnp.exp(-bcum)).astype(BF16)
        kd.append((k * jnp.exp(b_last - bcum)).astype(BF16))
        dec.append(jnp.exp(b_last))
        vb.append(v_ref[0, rows, :].astype(BF16))
        a = [jnp.where(low_incl, _dot_nt(qe[ci][:, kss[h]], ke[:, kss[h]]), 0.0).astype(BF16)
             for h in heads]
        o_intra.append([_dot(a[h], vb[ci][:, vss[h]]) for h in heads])
        upd.append([_dot_tn(vb[ci][:, vss[h]], kd[ci][:, kss[h]]) for h in heads])
    st = [st_ref[h] for h in heads]
    for ci in range(chunks):
        rows = pl.ds(ci * C, C)
        r = r_ref[0, rows, :]
        for h in heads:
            o = o_intra[ci][h] + _dot_nt(qe[ci][:, kss[h]], st[h].astype(BF16))
            st[h] = st[h] * dec[ci][:, kss[h]] + upd[ci][h]
            o = o * lax.rsqrt(jnp.mean(o * o, axis=-1, keepdims=True) + GLA_NORM_EPS) * nw
            rh = r[:, vss[h]]
            o_ref[0, rows, vss[h]] = (o * (rh * _sigmoid(rh))).astype(o_ref.dtype)
    for h in heads:
        st_ref[h] = st[h]


def _gla_layer(h, g_pre, g_post, w_in, a_up, a_bias, norm_w, w_o):
    B, S, D = h.shape
    dk, dv = GLA_KEY_DIM, GLA_VAL_DIM
    tm = min(PROJ_TM, S)
    n_main = 2 * dk + 2 * dv
    w_main = w_in[:, :n_main].astype(BF16)
    w_a = _pad_cols(w_in[:, n_main:], LORA_PAD).astype(BF16)
    aup = _pad_rows(a_up, LORA_PAD).astype(BF16)
    q, k, v, r, la = pl.pallas_call(
        _gla_proj_kernel,
        grid=(B, S // tm),
        in_specs=[_tok_spec(tm, D), _const_spec((1, D)), _const_spec((D, n_main)),
                  _const_spec((D, LORA_PAD)), _const_spec((LORA_PAD, dk)), _const_spec((1, dk))],
        out_specs=[_tok_spec(tm, dk), _tok_spec(tm, dk), _tok_spec(tm, dv), _tok_spec(tm, dv),
                   _tok_spec(tm, dk)],
        out_shape=[jax.ShapeDtypeStruct((B, S, w), F32) for w in (dk, dk, dv, dv, dk)],
        compiler_params=_params(),
        name="gla_proj",
    )(h, _row(g_pre), w_main, w_a, aup, _row(a_bias))

    chunks = 4 if S % (4 * GLA_CHUNK) == 0 else 1
    tc = chunks * GLA_CHUNK
    o = pl.pallas_call(
        functools.partial(_gla_chunk_kernel, chunks=chunks),
        grid=(B, S // tc),
        in_specs=[_tok_spec(tc, dk), _tok_spec(tc, dk), _tok_spec(tc, dv), _tok_spec(tc, dv),
                  _tok_spec(tc, dk), _const_spec((1, dv // GLA_HEADS))],
        out_specs=_tok_spec(tc, dv),
        out_shape=jax.ShapeDtypeStruct((B, S, dv), BF16),
        scratch_shapes=[pltpu.VMEM((GLA_HEADS, dv // GLA_HEADS, dk // GLA_HEADS), F32)],
        compiler_params=_params(),
        name="gla_chunk",
    )(q, k, v, r, la, _row(norm_w))
    return _post(h, o, w_o, g_post)


def _nsa_head_perm():
    perm = np.zeros((D_MODEL,), np.int32)
    for p in range(NSA_HPG):
        for g in range(NSA_KV_GROUPS):
            src = (g * NSA_HPG + p) * NSA_HEAD_DIM
            dst = p * LANES + g * NSA_HEAD_DIM
            perm[dst:dst + NSA_HEAD_DIM] = np.arange(src, src + NSA_HEAD_DIM)
    return perm


def _nsa_proj_kernel(h_ref, gn_ref, wqt_ref, wkv_ref, wg_ref, qt_ref, kv_ref, gates_ref):
    xn = _rms(h_ref[0], gn_ref[...], NORM_EPS).astype(BF16)
    qt_ref[0] = _dot_nt(wqt_ref[...], xn).astype(BF16)
    kv_ref[0] = _dot(xn, wkv_ref[...]).astype(BF16)
    gates_ref[0] = _sigmoid(_dot(xn, wg_ref[...]))


def _nsa_compress_kernel(r_ref, w1_ref, pe_ref, b1_ref, w2_ref, out_ref, *, n_cmp):
    r = r_ref[0, 0].astype(F32)
    rows = r.shape[0]
    pe = pe_ref[0]
    ra = (r + pe[0]).astype(BF16)
    rb = (pltpu.roll(r, rows - 1, 0) + pe[1]).astype(BF16)
    out = jnp.zeros((rows, LANES), F32)
    for g in range(NSA_KV_GROUPS):
        hid = _gelu_tanh(_dot(ra, w1_ref[0, g, 0]) + _dot(rb, w1_ref[0, g, 1]) + b1_ref[0])
        out = out + _dot(hid.astype(BF16), w2_ref[0, g])
    rowi = lax.broadcasted_iota(jnp.int32, out.shape, 0)
    out_ref[0, 0] = jnp.where(rowi < n_cmp, out, 0.0).astype(BF16)


def _query_chains(qt, heads_per_chain):
    row = lax.broadcasted_iota(jnp.int32, (LANES, qt.shape[1]), 0)
    zero = jnp.zeros((), qt.dtype)
    out = []
    for g in range(NSA_KV_GROUPS):
        keep = (row < NSA_HEAD_DIM) if g == 0 else (row >= NSA_HEAD_DIM)
        tiles = [jnp.where(keep, qt[p * LANES:(p + 1) * LANES, :], zero) for p in range(NSA_HPG)]
        out += [jnp.concatenate(tiles[c:c + heads_per_chain], axis=1)
                for c in range(0, NSA_HPG, heads_per_chain)]
    return out


def _ones_in_other_group(v, g):
    lane = lax.broadcasted_iota(jnp.int32, v.shape, 1)
    other = (lane >= NSA_HEAD_DIM) if g == 0 else (lane < NSA_HEAD_DIM)
    return jnp.where(other, jnp.ones((), v.dtype), v)


def _nsa_cmp_kernel(qt_ref, kc_ref, vc_ref, ov_ref, ocmp_ref, score_ref, *, n_cmp, n_slc):
    TQ = qt_ref.shape[2]
    ncp = kc_ref.shape[1]
    nbp = ov_ref.shape[0]
    G = NSA_KV_GROUPS
    R = NSA_HPG * TQ
    CW = NSA_CHAIN_COLS
    n_cw = R // CW
    chains = range(G * n_cw)
    i = pl.program_id(1)
    n_q = pl.num_programs(1)
    t0 = i * TQ

    def run(nr):
        kc = kc_ref[0, :nr, :]
        vc = vc_ref[0, :nr, :]
        ov = ov_ref[:, :nr]
        pos = t0 + lax.broadcasted_iota(jnp.int32, (1, TQ), 1)
        blk_c = lax.broadcasted_iota(jnp.int32, (nr, 1), 0)
        vis = (blk_c * CMP_STRIDE + (CMP_BLOCK - 1) <= pos) & (blk_c < n_cmp)
        bias = jnp.where(vis, 0.0, NEG_INF)
        bias = jnp.concatenate([bias] * (CW // TQ), axis=1)
        any_vis = (pos >= CMP_BLOCK - 1).astype(F32)
        any_vis = jnp.concatenate([any_vis] * (CW // TQ), axis=1)
        qch = _query_chains(qt_ref[0], CW // TQ)
        s = [_dot(kc, qch[j]) + bias for j in chains]
        e = [jnp.exp2(s[j] - jnp.max(s[j], axis=0, keepdims=True)) for j in chains]
        p = [e[j] * (any_vis / jnp.sum(e[j], axis=0, keepdims=True)) for j in chains]
        o = [_dot_tn(vc, p[j].astype(BF16)) for j in chains]

        blk_s = lax.broadcasted_iota(jnp.int32, (nbp, 1), 0)
        cur = lax.shift_right_logical(pos, 6)
        causal = (blk_s * SLC_BLOCK <= pos) & (blk_s < n_slc)
        forced = (blk_s == 0) | (blk_s == cur) | (blk_s == cur - 1)
        bonus = np.float32(FORCE_BONUS) * forced.astype(F32)
        for g in range(G):
            psum = jnp.zeros((nr, TQ), F32)
            for j in range(g * n_cw, (g + 1) * n_cw):
                for c in range(CW // TQ):
                    psum = psum + p[j][:, c * TQ:(c + 1) * TQ]
            hi, lo = _split2(psum)
            imp = _dot(ov, hi) + _dot(ov, lo)
            score_ref[0, g * nbp:(g + 1) * nbp, :] = jnp.where(causal, imp + bonus, NEG_INF)

        half = NSA_HEAD_DIM
        for p_i in range(NSA_HPG):
            j, c = divmod(p_i, CW // TQ)
            cols = slice(c * TQ, (c + 1) * TQ)
            o_pair = jnp.concatenate([o[j][:half, cols], o[n_cw + j][half:, cols]], axis=0)
            ocmp_ref[0, :, p_i * LANES:(p_i + 1) * LANES] = o_pair.T

    parts = CMP_ROW_PARTS if ncp % (CMP_ROW_PARTS * LANES) == 0 else 1
    for k in range(parts):
        in_part = (i * parts >= k * n_q) & (i * parts < (k + 1) * n_q)
        pl.when(in_part)(functools.partial(run, (k + 1) * ncp // parts))


def _nsa_topk_kernel(score_ref, bias_ref, *, top_n):
    G = NSA_KV_GROUPS
    nbp = score_ref.shape[1] // G
    sc0 = score_ref[0].reshape(G, nbp, score_ref.shape[2])
    row = lax.broadcasted_iota(jnp.int32, sc0.shape, 1).astype(F32)

    def body(_, carry):
        sc, bias = carry
        m = jnp.max(sc, axis=1, keepdims=True)
        idx = jnp.min(jnp.where(sc == m, row, np.float32(nbp)), axis=1, keepdims=True)
        hit = row == idx
        return jnp.where(hit, np.float32(-3e38), sc), jnp.where(hit, 0.0, bias)

    _, bias = lax.fori_loop(0, top_n, body, (sc0, jnp.full(sc0.shape, NEG_INF, F32)))
    bias_ref[0] = bias.reshape(G * nbp, score_ref.shape[2])


def _nsa_attn_kernel(qt_ref, ks_ref, vs_ref, kw_ref, vw_ref, bias_ref, ocmp_ref, gates_ref, ge_ref,
                     o_ref, m_ref, acc_ref, *, win_rows):
    TQ = qt_ref.shape[2]
    S = ks_ref.shape[1]
    TK = min(NSA_TK, S)
    n_blk = TK // SLC_BLOCK
    nbp = bias_ref.shape[1] // NSA_KV_GROUPS
    R = NSA_HPG * TQ
    CW = NSA_CHAIN_COLS
    n_cw = R // CW
    t0 = pl.program_id(1) * TQ
    pos = t0 + lax.broadcasted_iota(jnp.int32, (1, TQ), 1)
    n_kt = (t0 + TQ + TK - 1) // TK
    key_s = lax.broadcasted_iota(jnp.int32, (TK, 1), 0)
    w_start = pl.multiple_of(jnp.maximum(t0 - WINDOW, 0), TQ)
    wkey = w_start + lax.broadcasted_iota(jnp.int32, (win_rows, 1), 0)
    wbias = jnp.where((wkey <= pos) & (wkey > pos - WINDOW), 0.0, NEG_INF)
    wbias = jnp.concatenate([wbias] * (CW // TQ), axis=1)

    G = NSA_KV_GROUPS
    chains = range(G * n_cw)
    qch = _query_chains(qt_ref[0], CW // TQ)

    def scores(k_tile, bias):
        return [_dot(k_tile, qch[j]) + bias[j // n_cw] for j in chains]

    m_ref[...] = jnp.full(m_ref.shape, NEG_INF, F32)
    acc_ref[...] = jnp.zeros_like(acc_ref)

    def key_tile(kt, diagonal):
        k0 = pl.multiple_of(kt * TK, TK)
        b0 = pl.multiple_of(kt * n_blk, n_blk)
        k_tile = ks_ref[0, pl.ds(k0, TK), :]
        s = []
        for g in range(G):
            blk_bias = bias_ref[0, pl.ds(g * nbp + b0, n_blk), :]
            blk_bias = jnp.concatenate([blk_bias] * (CW // TQ), axis=1).reshape(n_blk, 1, CW)
            if diagonal:
                cb = jnp.where(k0 + key_s <= pos, 0.0, NEG_INF)
                cb = jnp.concatenate([cb] * (CW // TQ), axis=1).reshape(n_blk, SLC_BLOCK, CW)
                blk_bias = blk_bias + cb
            for j in range(g * n_cw, (g + 1) * n_cw):
                sj = _dot(k_tile, qch[j]).reshape(n_blk, SLC_BLOCK, CW) + blk_bias
                s.append(sj.reshape(TK, CW))
        v_raw = vs_ref[0, pl.ds(k0, TK), :]
        v_tile = [_ones_in_other_group(v_raw, g) for g in range(G)]
        for g in range(G):
            grp = range(g * n_cw, (g + 1) * n_cw)
            m_old = {j: m_ref[:, j * CW:(j + 1) * CW] for j in grp}
            m_new = {j: jnp.maximum(m_old[j], jnp.max(s[j], axis=0, keepdims=True)) for j in grp}
            p = {j: jnp.exp2(s[j] - m_new[j]).astype(BF16) for j in grp}
            pv = {j: _dot_tn(v_tile[g], p[j]) for j in grp}
            for j in grp:
                cols = slice(j * CW, (j + 1) * CW)
                acc_ref[:, cols] = acc_ref[:, cols] * jnp.exp2(m_old[j] - m_new[j]) + pv[j]
                m_ref[:, cols] = m_new[j]

    def full_tile(kt, carry):
        key_tile(kt, diagonal=False)
        return carry

    lax.fori_loop(0, n_kt - 1, full_tile, 0)
    key_tile(n_kt - 1, diagonal=True)

    vw_raw = vw_ref[0, pl.ds(w_start, win_rows), :]
    vw = [_ones_in_other_group(vw_raw, g) for g in range(G)]
    s = scores(kw_ref[0, pl.ds(w_start, win_rows), :], [wbias] * G)
    p = [jnp.exp2(s[j] - jnp.max(s[j], axis=0, keepdims=True)).astype(BF16) for j in chains]
    acc_win = jnp.concatenate([_dot_tn(vw[j // n_cw], p[j]) for j in chains], axis=1)

    half = NSA_HEAD_DIM
    normed = {}
    for name, a in (("slc", acc_ref[...]), ("win", acc_win)):
        a0, a1 = a[:, :R], a[:, R:]
        normed[name] = jnp.concatenate([a0[:half] / a0[half:half + 1], a1[half:] / a1[0:1]], axis=0)
    gates = gates_ref[0]
    g_cmp = _dot_hilo(gates, ge_ref[0])
    g_slc = _dot_hilo(gates, ge_ref[1])
    g_win = _dot_hilo(gates, ge_ref[2])
    for p_i in range(NSA_HPG):
        cols = slice(p_i * LANES, (p_i + 1) * LANES)
        slc = normed["slc"][:, p_i * TQ:(p_i + 1) * TQ].T
        win = normed["win"][:, p_i * TQ:(p_i + 1) * TQ].T
        o_ref[0, :, cols] = (g_cmp[:, cols] * ocmp_ref[0, :, cols] + g_slc[:, cols] * slc
                             + g_win[:, cols] * win).astype(o_ref.dtype)


def _nsa_layer(h, g_pre, g_post, w_in, cmp_w1, cmp_b1, cmp_w2, cmp_pe, w_o):
    B, S, D = h.shape
    H, G, dh, hpg = NSA_HEADS, NSA_KV_GROUPS, NSA_HEAD_DIM, NSA_HPG
    perm = _nsa_head_perm()
    n_q = H * dh
    n_kv = 6 * G * dh
    wqt = (w_in[:, :n_q] * np.float32(dh ** -0.5 * np.log2(np.e)))[:, perm].T.astype(BF16)
    wkv = w_in[:, n_q:n_q + n_kv].astype(BF16)
    wgt = _pad_cols(w_in[:, n_q + n_kv:], LANES).astype(BF16)
    tm = min(PROJ_TM, S)
    qt_spec = lambda t: pl.BlockSpec((1, n_q, t), lambda b, i: (b, 0, i))
    qt, kv, gates = pl.pallas_call(
        _nsa_proj_kernel,
        grid=(B, S // tm),
        in_specs=[_tok_spec(tm, D), _const_spec((1, D)), _const_spec((n_q, D)),
                  _const_spec((D, n_kv)), _const_spec((D, LANES))],
        out_specs=[qt_spec(tm), _tok_spec(tm, n_kv), _tok_spec(tm, LANES)],
        out_shape=[jax.ShapeDtypeStruct((B, n_q, S), BF16), jax.ShapeDtypeStruct((B, S, n_kv), BF16),
                   jax.ShapeDtypeStruct((B, S, LANES), F32)],
        compiler_params=_params(),
        name="nsa_proj",
    )(h, _row(g_pre), wqt, wkv, wgt)

    n_cmp = (S - CMP_BLOCK) // CMP_STRIDE + 1
    ncp = S // CMP_STRIDE
    half = CMP_BLOCK // 2
    assert CMP_STRIDE == half
    rk = jnp.stack([kv[:, :, :LANES], kv[:, :, LANES:2 * LANES]]).reshape(2, B, ncp, half * LANES)
    w1 = cmp_w1.reshape(2, 2, half, dh, CMP_HIDDEN)
    w1x = jnp.zeros((2, G, 2, half, G, dh, CMP_HIDDEN), F32)
    for g in range(G):
        w1x = w1x.at[:, g, :, :, g].set(w1)
    w1x = w1x.reshape(2, G, 2, half * LANES, CMP_HIDDEN).astype(BF16)
    pe = jnp.broadcast_to(cmp_pe.reshape(2, 2, half, 1, dh), (2, 2, half, G, dh))
    pe = pe.reshape(2, 2, 1, half * LANES).astype(F32)
    w2x = jnp.zeros((2, G, CMP_HIDDEN, G, dh), F32)
    for g in range(G):
        w2x = w2x.at[:, g, :, g].set(cmp_w2)
    w2x = w2x.reshape(2, G, CMP_HIDDEN, LANES).astype(BF16)
    kvc = pl.pallas_call(
        functools.partial(_nsa_compress_kernel, n_cmp=n_cmp),
        grid=(2, B),
        in_specs=[pl.BlockSpec((1, 1, ncp, half * LANES), lambda c, b: (c, b, 0, 0)),
                  pl.BlockSpec((1, G, 2, half * LANES, CMP_HIDDEN), lambda c, b: (c, 0, 0, 0, 0)),
                  pl.BlockSpec((1, 2, 1, half * LANES), lambda c, b: (c, 0, 0, 0)),
                  pl.BlockSpec((1, 1, CMP_HIDDEN), lambda c, b: (c, 0, 0)),
                  pl.BlockSpec((1, G, CMP_HIDDEN, LANES), lambda c, b: (c, 0, 0, 0))],
        out_specs=pl.BlockSpec((1, 1, ncp, LANES), lambda c, b: (c, b, 0, 0)),
        out_shape=jax.ShapeDtypeStruct((2, B, ncp, LANES), BF16),
        compiler_params=_params(),
        name="nsa_compress",
    )(rk, w1x, pe, cmp_b1.reshape(2, 1, CMP_HIDDEN).astype(F32), w2x)

    n_slc = S // SLC_BLOCK
    top_n = min(SLC_TOPK, n_slc)
    nbp = max(LANES, n_slc)
    start = np.arange(ncp) * CMP_STRIDE
    end = start + CMP_BLOCK - 1
    s_start = np.arange(nbp) * SLC_BLOCK
    s_end = s_start + SLC_BLOCK - 1
    overlap = ((end[:, None] >= s_start[None, :]) & (start[:, None] <= s_end[None, :])
               & (np.arange(ncp)[:, None] < n_cmp) & (np.arange(nbp)[None, :] < n_slc))
    overlap_t = jnp.asarray(overlap.astype(np.float32).T, BF16)
    TQ = min(NSA_TQ, S)
    blk_tok_spec = lambda t: pl.BlockSpec((1, G * nbp, t), lambda b, i: (b, 0, i))
    o_cmp, score = pl.pallas_call(
        functools.partial(_nsa_cmp_kernel, n_cmp=n_cmp, n_slc=n_slc),
        grid=(B, S // TQ),
        in_specs=[qt_spec(TQ),
                  pl.BlockSpec((None, 1, ncp, LANES), lambda b, i: (0, b, 0, 0)),
                  pl.BlockSpec((None, 1, ncp, LANES), lambda b, i: (1, b, 0, 0)),
                  _const_spec((nbp, ncp))],
        out_specs=[_tok_spec(TQ, n_q), blk_tok_spec(TQ)],
        out_shape=[jax.ShapeDtypeStruct((B, S, n_q), F32),
                   jax.ShapeDtypeStruct((B, G * nbp, S), F32)],
        compiler_params=_params(),
        name="nsa_compressed_attn",
    )(qt, kvc, kvc, overlap_t)
    tr = min(NSA_TOPK_TOKENS, S)
    sel_bias = pl.pallas_call(
        functools.partial(_nsa_topk_kernel, top_n=top_n),
        grid=(B, S // tr),
        in_specs=[blk_tok_spec(tr)],
        out_specs=blk_tok_spec(tr),
        out_shape=jax.ShapeDtypeStruct((B, G * nbp, S), F32),
        compiler_params=_params(),
        name="nsa_topk",
    )(score)

    ge = np.zeros((3, LANES, n_q), np.float32)
    for g in range(G):
        for p in range(hpg):
            for j in range(3):
                ge[j, g * hpg * 3 + p * 3 + j, p * LANES + g * dh:p * LANES + (g + 1) * dh] = 1.0
    ge = jnp.asarray(ge, BF16)
    win_rows = min(WINDOW + TQ, S)
    kv_spec = lambda c: pl.BlockSpec((1, S, LANES), lambda b, i, c=c: (b, 0, c))
    o = pl.pallas_call(
        functools.partial(_nsa_attn_kernel, win_rows=win_rows),
        grid=(B, S // TQ),
        in_specs=[qt_spec(TQ), kv_spec(2), kv_spec(3), kv_spec(4), kv_spec(5),
                  blk_tok_spec(TQ), _tok_spec(TQ, n_q), _tok_spec(TQ, LANES),
                  _const_spec((3, LANES, n_q))],
        out_specs=_tok_spec(TQ, n_q),
        out_shape=jax.ShapeDtypeStruct((B, S, n_q), BF16),
        scratch_shapes=[pltpu.VMEM((1, G * hpg * TQ), F32), pltpu.VMEM((LANES, G * hpg * TQ), F32)],
        compiler_params=_params(),
        name="nsa_selected_window_attn",
    )(qt, kv, kv, kv, kv, sel_bias, o_cmp, gates, ge)
    return _post(h, o, w_o[perm, :], g_post)


def kernel(x, norm_g, ffn_w_in, ffn_conv_w, ffn_conv_b, ffn_w_out, rwkv_mu, rwkv_w_rkv, rwkv_w0, rwkv_w1, rwkv_w2, rwkv_a0, rwkv_a1, rwkv_a2, rwkv_g1, rwkv_g2, rwkv_k_k, rwkv_k_a, rwkv_r_k, rwkv_ln_w, rwkv_ln_b, rwkv_w_o, rwkv_v0, rwkv_v1, rwkv_v2, nsa_w_in, nsa_cmp_w1, nsa_cmp_b1, nsa_cmp_w2, nsa_cmp_pe, nsa_w_o, gla_w_in, gla_a_up, gla_a_bias, gla_norm_w, gla_w_o):
    depth = norm_g.shape[0]
    h = x
    v_first = None
    for i in range(depth):
        kind, j = i % 3, i // 3
        if kind == 0:
            vres = None if j == 0 else (rwkv_v0[j - 1], rwkv_v1[j - 1], rwkv_v2[j - 1])
            h, v_first = _rwkv_layer(
                h, norm_g[i, 0], norm_g[i, 1], rwkv_mu[j], rwkv_w_rkv[j], rwkv_w0[j], rwkv_w1[j],
                rwkv_w2[j], rwkv_a0[j], rwkv_a1[j], rwkv_a2[j], rwkv_g1[j], rwkv_g2[j], rwkv_k_k[j],
                rwkv_k_a[j], rwkv_r_k[j], rwkv_ln_w[j], rwkv_ln_b[j], rwkv_w_o[j], v_first, vres)
        elif kind == 1:
            h = _nsa_layer(h, norm_g[i, 0], norm_g[i, 1], nsa_w_in[j], nsa_cmp_w1[j], nsa_cmp_b1[j],
                           nsa_cmp_w2[j], nsa_cmp_pe[j], nsa_w_o[j])
        else:
            h = _gla_layer(h, norm_g[i, 0], norm_g[i, 1], gla_w_in[j], gla_a_up[j], gla_a_bias[j],
                           gla_norm_w[j], gla_w_o[j])
        h = _ffn(h, norm_g[i, 2], ffn_w_in[i], ffn_conv_w[i], ffn_conv_b[i], ffn_w_out[i],
                 norm_g[i, 3])
    return h
```

```python
import functools

import numpy as np
import jax
import jax.numpy as jnp
from jax import lax
from jax.experimental import pallas as pl
from jax.experimental.pallas import tpu as pltpu

F32 = jnp.float32
BF16 = jnp.bfloat16

V7X_VMEM_BYTES = 64 * 1024 * 1024
VMEM_LIMIT_BYTES = V7X_VMEM_BYTES - 8 * 1024 * 1024
LANES = 128
SUBLANES = 8

D_MODEL = 1024
NORM_EPS = 1e-6
NEG_INF = -1e30

RWKV_HEAD_DIM = 64
RWKV_HEADS = D_MODEL // RWKV_HEAD_DIM
RWKV_GN_EPS = 64e-5
RWKV_CHUNK = 64
RWKV_CHUNKS_PER_STEP = 2
LORA_PAD = 128

NSA_HEADS = 16
NSA_KV_GROUPS = 2
NSA_HEAD_DIM = D_MODEL // NSA_HEADS
NSA_HPG = NSA_HEADS // NSA_KV_GROUPS
CMP_BLOCK = 32
CMP_STRIDE = 16
CMP_HIDDEN = 256
CMP_ROW_PARTS = 4
SLC_BLOCK = 64
SLC_TOPK = 16
WINDOW = 512
FORCE_BONUS = 1e4
NSA_TQ = 256
NSA_CHAIN_COLS = 256
NSA_TK = 512
NSA_TOPK_TOKENS = 512

GLA_HEADS = 4
GLA_KEY_DIM = D_MODEL // 2
GLA_VAL_DIM = D_MODEL
GLA_TAU = 16.0
GLA_CHUNK = 64
GLA_NORM_EPS = 1e-5

FFN_CHUNK = 256
FFN_TM = 512
PROJ_TM = 256
POST_TM = 512


def _dot(a, b):
    return jnp.dot(a, b, preferred_element_type=F32)


def _dot_nt(a, b):
    return lax.dot_general(a, b, (((1,), (1,)), ((), ())), preferred_element_type=F32)


def _dot_tn(a, b):
    return lax.dot_general(a, b, (((0,), (0,)), ((), ())), preferred_element_type=F32)


def _split2(x):
    hi = x.astype(BF16)
    lo = (x - hi.astype(F32)).astype(BF16)
    return hi, lo


def _dot_hilo(x, m):
    hi, lo = _split2(x)
    return _dot(hi, m) + _dot(lo, m)


def _dot_tri3(tri, x):
    hi = x.astype(BF16)
    r1 = x - hi.astype(F32)
    mid = r1.astype(BF16)
    lo = (r1 - mid.astype(F32)).astype(BF16)
    return _dot(tri, hi) + _dot(tri, mid) + _dot(tri, lo)


def _head_sum(x, red_ref, exp_ref):
    return _dot_hilo(_dot(x.astype(BF16), red_ref[...]), exp_ref[...])


def _rms(x, g, eps):
    return x * lax.rsqrt(jnp.mean(x * x, axis=-1, keepdims=True) + eps) * g


def _sigmoid(x):
    return 1.0 / (1.0 + jnp.exp(-x))


def _softplus(x):
    return jnp.maximum(x, 0.0) + jnp.log(1.0 + jnp.exp(-jnp.abs(x)))


def _gelu_tanh(x):
    c = np.float32(np.sqrt(2.0 / np.pi))
    return 0.5 * x * (1.0 + jnp.tanh(c * (x + 0.044715 * (x * x * x))))


def _shift_rows(x, n, prev8):
    rolled = pltpu.roll(x, n, 0)
    row = lax.broadcasted_iota(jnp.int32, x.shape, 0)
    for j in range(n):
        rolled = jnp.where(row == j, prev8[SUBLANES - n + j:SUBLANES - n + j + 1, :], rolled)
    return rolled


def _const_spec(shape):
    nd = len(shape)
    return pl.BlockSpec(shape, lambda *_: (0,) * nd, pipeline_mode=pl.Buffered(1))


def _tok_spec(tm, width):
    return pl.BlockSpec((1, tm, width), lambda b, i: (b, i, 0))


def _params():
    return pltpu.CompilerParams(dimension_semantics=("arbitrary", "arbitrary"),
                                vmem_limit_bytes=VMEM_LIMIT_BYTES)


def _row(v):
    return v.reshape(1, -1).astype(F32)


def _ffn_kernel(h_ref, gpre_ref, win_ref, cw_ref, cb_ref, wo_ref, gpost_ref, out_ref,
                xn_ref, acc_ref, halo_ref, *, fc):
    @pl.when(pl.program_id(1) == 0)
    def _():
        halo_ref[...] = jnp.zeros_like(halo_ref)

    h = h_ref[0]
    xn_ref[...] = _rms(h, gpre_ref[...], NORM_EPS).astype(BF16)
    acc_ref[...] = jnp.zeros_like(acc_ref)
    tm = h.shape[0]
    F = wo_ref.shape[0]
    nc = F // fc

    def gate_up(c):
        xn = xn_ref[...]
        return (_dot(xn, win_ref[:, c * fc:(c + 1) * fc]),
                _dot(xn, win_ref[:, F + c * fc:F + (c + 1) * fc]))

    nxt = gate_up(0)
    for c in range(nc):
        cols = slice(c * fc, (c + 1) * fc)
        gate, up = nxt
        if c + 1 < nc:
            nxt = gate_up(c + 1)
        prev = halo_ref[:, cols]
        g1 = _shift_rows(gate, 1, prev)
        g2 = _shift_rows(gate, 2, prev)
        cw = cw_ref[:, cols]
        z = cw[2:3] * gate + cw[1:2] * g1 + cw[0:1] * g2 + cb_ref[:, cols]
        halo_ref[:, cols] = gate[tm - SUBLANES:tm, :]
        act = (z * _sigmoid(z) * up).astype(BF16)
        acc_ref[...] += _dot(act, wo_ref[cols, :])
    out_ref[0] = h + _rms(acc_ref[...], gpost_ref[...], NORM_EPS)


def _ffn(h, g_pre, w_in, conv_w, conv_b, w_out, g_post):
    B, S, D = h.shape
    F = w_out.shape[0]
    assert F % FFN_CHUNK == 0
    tm = min(FFN_TM, S)
    return pl.pallas_call(
        functools.partial(_ffn_kernel, fc=FFN_CHUNK),
        grid=(B, S // tm),
        in_specs=[_tok_spec(tm, D), _const_spec((1, D)), _const_spec((D, 2 * F)),
                  _const_spec((3, F)), _const_spec((1, F)), _const_spec((F, D)),
                  _const_spec((1, D))],
        out_specs=_tok_spec(tm, D),
        out_shape=jax.ShapeDtypeStruct((B, S, D), F32),
        scratch_shapes=[pltpu.VMEM((tm, D), BF16), pltpu.VMEM((tm, D), F32),
                        pltpu.VMEM((SUBLANES, F), F32)],
        compiler_params=_params(),
        name="conv_ffn",
    )(h, _row(g_pre), w_in.astype(BF16), conv_w.astype(F32), _row(conv_b), w_out.astype(BF16),
      _row(g_post))


def _post_kernel(h_ref, o_ref, wo_ref, g_ref, out_ref):
    y = _dot(o_ref[0].astype(BF16), wo_ref[...])
    out_ref[0] = h_ref[0] + _rms(y, g_ref[...], NORM_EPS)


def _post(h, o, w_o, g_post):
    B, S, D = h.shape
    K = o.shape[-1]
    tm = min(POST_TM, S)
    return pl.pallas_call(
        _post_kernel,
        grid=(B, S // tm),
        in_specs=[_tok_spec(tm, D), _tok_spec(tm, K), _const_spec((K, D)), _const_spec((1, D))],
        out_specs=_tok_spec(tm, D),
        out_shape=jax.ShapeDtypeStruct((B, S, D), F32),
        compiler_params=_params(),
        name="mixer_out",
    )(h, o, w_o.astype(BF16), _row(g_post))


def _rwkv_proj_kernel(*refs, has_vres):
    if has_vres:
        (h_ref, gn_ref, mu_ref, wr_ref, wk_ref, wv_ref, w1_ref, w2_ref, a1_ref, a2_ref, g1_ref,
         g2_ref, vec_ref, red_ref, exp_ref, vf_ref, v1_ref, v2_ref,
         r_ref, ld_ref, k_ref, v_ref, kkn_ref, b_ref, g_ref, bonus_ref, carry_ref) = refs
    else:
        (h_ref, gn_ref, mu_ref, wr_ref, wk_ref, wv_ref, w1_ref, w2_ref, a1_ref, a2_ref, g1_ref,
         g2_ref, vec_ref, red_ref, exp_ref,
         r_ref, ld_ref, k_ref, v_ref, kkn_ref, b_ref, g_ref, bonus_ref, carry_ref) = refs

    @pl.when(pl.program_id(1) == 0)
    def _():
        carry_ref[...] = jnp.zeros_like(carry_ref)

    x = _rms(h_ref[0], gn_ref[...], NORM_EPS)
    tm = x.shape[0]
    xx = _shift_rows(x, 1, carry_ref[...]) - x
    carry_ref[...] = x[tm - SUBLANES:tm, :]
    mu = mu_ref[...]
    xr, xw, xk, xv, xa, xg = ((x + xx * mu[c:c + 1]).astype(BF16) for c in range(6))
    vec = vec_ref[...]
    w0, a0, k_k, k_a, r_k, v0 = (vec[c:c + 1] for c in range(6))

    r = _dot(xr, wr_ref[...])
    k = _dot(xk, wk_ref[...])
    v = _dot(xv, wv_ref[...])
    wl = -_softplus(-(w0 + _dot(jnp.tanh(_dot(xw, w1_ref[...])).astype(BF16), w2_ref[...]))) - 0.5
    ld_ref[0] = -jnp.exp(wl)
    a = _sigmoid(a0 + _dot(_dot(xa, a1_ref[...]).astype(BF16), a2_ref[...]))
    g_ref[0] = _dot(_sigmoid(_dot(xg, g1_ref[...])).astype(BF16), g2_ref[...])
    if has_vres:
        mix = _sigmoid(v0 + _dot(_dot(xv, v1_ref[...]).astype(BF16), v2_ref[...]))
        v = v + (vf_ref[0] - v) * mix
    kk = k * k_k
    norm = jnp.sqrt(_head_sum(kk * kk, red_ref, exp_ref))
    kkn = kk / jnp.maximum(norm, 1e-12)
    k = k * (1.0 + (a - 1.0) * k_a)
    r_ref[0] = r
    k_ref[0] = k
    v_ref[0] = v
    kkn_ref[0] = kkn
    b_ref[0] = kkn * a
    bonus_ref[0] = _head_sum(r * k * r_k, red_ref, exp_ref) * v


def _pad_cols(w, n):
    return jnp.pad(w, ((0, 0), (0, n - w.shape[1])))


def _pad_rows(w, n):
    return jnp.pad(w, ((0, n - w.shape[0]), (0, 0)))


def _head_reduce_expand(d, hd, value):
    red = (np.arange(d)[:, None] // hd == np.arange(LANES)[None, :]).astype(np.float32)
    return jnp.asarray(red, BF16), jnp.asarray(red.T * value, BF16)


def _rwkv_proj(h, g_norm, mu, w_rkv, w0, w1, w2, a0, a1, a2, g1, g2, k_k, k_a, r_k, v_first, vres):
    B, S, D = h.shape
    tm = min(PROJ_TM, S)
    has_vres = vres is not None
    v0 = vres[0] if has_vres else jnp.zeros((D,), F32)
    vec = jnp.stack([w0, a0, k_k, k_a, r_k.reshape(-1), v0]).astype(F32)
    red, exp = _head_reduce_expand(D, RWKV_HEAD_DIM, 1.0)
    lora_in = lambda w: _pad_cols(w, LORA_PAD).astype(BF16)
    lora_out = lambda w: _pad_rows(w, LORA_PAD).astype(BF16)
    args = [h, _row(g_norm), mu, w_rkv[0].astype(BF16), w_rkv[1].astype(BF16), w_rkv[2].astype(BF16),
            lora_in(w1), lora_out(w2), lora_in(a1), lora_out(a2), lora_in(g1), lora_out(g2), vec, red, exp]
    specs = [_tok_spec(tm, D), _const_spec((1, D)), _const_spec((6, D)), _const_spec((D, D)),
             _const_spec((D, D)), _const_spec((D, D)), _const_spec((D, LORA_PAD)),
             _const_spec((LORA_PAD, D)), _const_spec((D, LORA_PAD)), _const_spec((LORA_PAD, D)),
             _const_spec((D, LORA_PAD)), _const_spec((LORA_PAD, D)), _const_spec((6, D)),
             _const_spec((D, LANES)), _const_spec((LANES, D))]
    if has_vres:
        args += [v_first, lora_in(vres[1]), lora_out(vres[2])]
        specs += [_tok_spec(tm, D), _const_spec((D, LORA_PAD)), _const_spec((LORA_PAD, D))]
    outs = pl.pallas_call(
        functools.partial(_rwkv_proj_kernel, has_vres=has_vres),
        grid=(B, S // tm),
        in_specs=specs,
        out_specs=[_tok_spec(tm, D)] * 8,
        out_shape=[jax.ShapeDtypeStruct((B, S, D), F32)] * 8,
        scratch_shapes=[pltpu.VMEM((SUBLANES, D), F32)],
        compiler_params=_params(),
        name="rwkv_proj",
    )(*args)
    return outs


def _rwkv_scan_kernel(r_ref, ld_ref, k_ref, v_ref, kkn_ref, b_ref, y_ref, st_ref):
    @pl.when(pl.program_id(1) == 0)
    def _():
        st_ref[...] = jnp.zeros_like(st_ref)

    C = RWKV_CHUNK
    N = RWKV_HEAD_DIM
    assert 2 * N == LANES and C == N
    rowi = lax.broadcasted_iota(jnp.int32, (C, C), 0)
    coli = lax.broadcasted_iota(jnp.int32, (C, C), 1)
    tri = (rowi >= coli).astype(BF16)
    prow = lax.broadcasted_iota(jnp.int32, (C, LANES), 0)
    pcol = lax.broadcasted_iota(jnp.int32, (C, LANES), 1) & (N - 1)
    low_incl = prow >= pcol
    low_strict = prow > pcol
    eye = (prow == pcol).astype(F32)
    srow = lax.broadcasted_iota(jnp.int32, (LANES, LANES), 0)
    scol = lax.broadcasted_iota(jnp.int32, (LANES, LANES), 1)
    same_head = (srow < N) == (scol < N)

    def bdiag(x):
        lane = lax.broadcasted_iota(jnp.int32, x.shape, 1)
        zero = jnp.zeros((), x.dtype)
        return jnp.concatenate([jnp.where(lane < N, x, zero), jnp.where(lane >= N, x, zero)], axis=0)

    n_chunks = r_ref.shape[1] // C
    n_pairs = RWKV_HEADS // 2
    sls = [slice(p * LANES, (p + 1) * LANES) for p in range(n_pairs)]
    a_t, r_t, b_t, k_t, b_h, k_h, vb, p_all = ([] for _ in range(8))
    for ci in range(n_chunks):
        rows = pl.ds(ci * C, C)
        ld = ld_ref[0, rows, :]
        cum = _dot_tri3(tri, ld)
        cum_last = cum[C - 1:C, :]
        p_inv = jnp.exp(-cum)
        p_rem = jnp.exp(cum_last - cum)
        p_all.append(jnp.exp(cum_last))
        b = b_ref[0, rows, :]
        k = k_ref[0, rows, :]
        a_t.append((-kkn_ref[0, rows, :] * jnp.exp(cum - ld)).astype(BF16))
        r_t.append((r_ref[0, rows, :] * jnp.exp(cum)).astype(BF16))
        b_t.append((b * p_inv).astype(BF16))
        k_t.append((k * p_inv).astype(BF16))
        b_h.append((b * p_rem).astype(BF16))
        k_h.append((k * p_rem).astype(BF16))
        vb.append(v_ref[0, rows, :].astype(BF16))

    units = [(ci, p) for ci in range(n_chunks) for p in range(n_pairs)]
    idx = range(len(units))
    tile = lambda arr, u: arr[units[u][0]][:, sls[units[u][1]]]
    ar = [jnp.concatenate([tile(a_t, u), tile(r_t, u)], axis=0) for u in idx]
    s_b = [_dot_nt(ar[u], bdiag(tile(b_t, u))) for u in idx]
    s_k = [_dot_nt(ar[u], bdiag(tile(k_t, u))) for u in idx]
    a_ab = [jnp.where(low_strict, s_b[u][:C], 0.0) for u in idx]
    a_rb = [jnp.where(low_incl, s_b[u][C:], 0.0).astype(BF16) for u in idx]
    a_ak = [jnp.where(low_strict, s_k[u][:C], 0.0).astype(BF16) for u in idx]
    a_rk = [jnp.where(low_incl, s_k[u][C:], 0.0).astype(BF16) for u in idx]
    av = [_dot(jnp.concatenate([a_ak[u], a_rk[u]], axis=0), bdiag(tile(vb, u))) for u in idx]
    akv = [av[u][:C].astype(BF16) for u in idx]
    pw = [a_ab[u].astype(BF16) for u in idx]
    tinv = [eye + a_ab[u] for u in idx]
    pw = [_dot(pw[u], bdiag(pw[u])).astype(BF16) for u in idx]
    span = 4
    while span < C:
        both = [_dot(jnp.concatenate([pw[u], tinv[u].astype(BF16)], axis=0), bdiag(pw[u]))
                for u in idx]
        tinv = [tinv[u] + both[u][C:] for u in idx]
        pw = [both[u][:C].astype(BF16) for u in idx]
        span *= 2
    tinv = [tinv[u] + _dot(tinv[u].astype(BF16), bdiag(pw[u])) for u in idx]
    tb = [tinv[u].astype(BF16) for u in idx]
    a_hat = [_dot(tb[u], bdiag(tile(a_t, u))).astype(BF16) for u in idx]
    u_loc = [_dot(tb[u], bdiag(akv[u])) for u in idx]

    pairs = range(n_pairs)
    st = [st_ref[p] for p in pairs]
    for ci in range(n_chunks):
        u0 = ci * n_pairs
        on_state = [_dot_nt(jnp.concatenate([a_hat[u0 + p], r_t[ci][:, sls[p]]], axis=0),
                            st[p].astype(BF16)) for p in pairs]
        ub = [(on_state[p][:C] + u_loc[u0 + p]).astype(BF16) for p in pairs]
        for p in pairs:
            y_ref[0, pl.ds(ci * C, C), sls[p]] = (on_state[p][C:] + _dot(a_rb[u0 + p], bdiag(ub[p]))
                                                  + av[u0 + p][C:])
        upd = [_dot_tn(jnp.concatenate([ub[p], vb[ci][:, sls[p]]], axis=0),
                       jnp.concatenate([b_h[ci][:, sls[p]], k_h[ci][:, sls[p]]], axis=0))
               for p in pairs]
        st = [st[p] * p_all[ci][:, sls[p]] + jnp.where(same_head, upd[p], 0.0) for p in pairs]
    for p in pairs:
        st_ref[p] = st[p]


def _rwkv_scan(r, ld, k, v, kkn, b):
    B, S, D = r.shape
    C = RWKV_CHUNK * RWKV_CHUNKS_PER_STEP
    return pl.pallas_call(
        _rwkv_scan_kernel,
        grid=(B, S // C),
        in_specs=[_tok_spec(C, D)] * 6,
        out_specs=_tok_spec(C, D),
        out_shape=jax.ShapeDtypeStruct((B, S, D), F32),
        scratch_shapes=[pltpu.VMEM((RWKV_HEADS // 2, LANES, LANES), F32)],
        compiler_params=_params(),
        name="rwkv_scan",
    )(r, ld, k, v, kkn, b)


def _rwkv_post_kernel(h_ref, y_ref, bonus_ref, g_ref, vec_ref, red_ref, avg_ref, wo_ref, gpost_ref,
                      out_ref):
    y = y_ref[0]
    d = y - _head_sum(y, red_ref, avg_ref)
    var = _head_sum(d * d, red_ref, avg_ref)
    vec = vec_ref[...]
    yn = d * lax.rsqrt(var + RWKV_GN_EPS) * vec[0:1] + vec[1:2]
    o = ((yn + bonus_ref[0]) * g_ref[0]).astype(BF16)
    out_ref[0] = h_ref[0] + _rms(_dot(o, wo_ref[...]), gpost_ref[...], NORM_EPS)


def _rwkv_post(h, y, bonus, g, ln_w, ln_b, w_o, g_post):
    B, S, D = h.shape
    tm = min(PROJ_TM, S)
    vec = jnp.stack([ln_w, ln_b]).astype(F32)
    red, avg = _head_reduce_expand(D, RWKV_HEAD_DIM, 1.0 / RWKV_HEAD_DIM)
    return pl.pallas_call(
        _rwkv_post_kernel,
        grid=(B, S // tm),
        in_specs=[_tok_spec(tm, D)] * 4 + [_const_spec((2, D)), _const_spec((D, LANES)),
                                           _const_spec((LANES, D)), _const_spec((D, D)),
                                           _const_spec((1, D))],
        out_specs=_tok_spec(tm, D),
        out_shape=jax.ShapeDtypeStruct((B, S, D), F32),
        compiler_params=_params(),
        name="rwkv_out",
    )(h, y, bonus, g, vec, red, avg, w_o.astype(BF16), _row(g_post))


def _rwkv_layer(h, g_pre, g_post, mu, w_rkv, w0, w1, w2, a0, a1, a2, g1, g2, k_k, k_a, r_k,
                ln_w, ln_b, w_o, v_first, vres):
    r, ld, k, v, kkn, b, g, bonus = _rwkv_proj(h, g_pre, mu, w_rkv, w0, w1, w2, a0, a1, a2, g1, g2,
                                               k_k, k_a, r_k, v_first, vres)
    if vres is None:
        v_first = v
    y = _rwkv_scan(r, ld, k, v, kkn, b)
    return _rwkv_post(h, y, bonus, g, ln_w, ln_b, w_o, g_post), v_first


def _gla_proj_kernel(h_ref, gn_ref, w_ref, wa_ref, aup_ref, abias_ref,
                     q_ref, k_ref, v_ref, r_ref, la_ref):
    xn = _rms(h_ref[0], gn_ref[...], NORM_EPS).astype(BF16)
    qkvr = _dot(xn, w_ref[...])
    dk = GLA_KEY_DIM
    dv = GLA_VAL_DIM
    q_ref[0] = qkvr[:, :dk] * np.float32((dk // GLA_HEADS) ** -0.5)
    k_ref[0] = qkvr[:, dk:2 * dk]
    v_ref[0] = qkvr[:, 2 * dk:2 * dk + dv]
    r_ref[0] = qkvr[:, 2 * dk + dv:]
    z = _dot(_dot(xn, wa_ref[...]).astype(BF16), aup_ref[...]) + abias_ref[...]
    la_ref[0] = -_softplus(-z) * np.float32(1.0 / GLA_TAU)


def _gla_chunk_kernel(q_ref, k_ref, v_ref, r_ref, la_ref, nw_ref, o_ref, st_ref, *, chunks):
    @pl.when(pl.program_id(1) == 0)
    def _():
        st_ref[...] = jnp.zeros_like(st_ref)

    C = GLA_CHUNK
    dk = GLA_KEY_DIM // GLA_HEADS
    dv = GLA_VAL_DIM // GLA_HEADS
    rowi = lax.broadcasted_iota(jnp.int32, (C, C), 0)
    coli = lax.broadcasted_iota(jnp.int32, (C, C), 1)
    low_incl = rowi >= coli
    tri = low_incl.astype(BF16)
    nw = nw_ref[...]

    heads = range(GLA_HEADS)
    kss = [slice(h * dk, (h + 1) * dk) for h in heads]
    vss = [slice(h * dv, (h + 1) * dv) for h in heads]
    qe, kd, dec, vb, o_intra, upd = [], [], [], [], [], []
    for ci in range(chunks):
        rows = pl.ds(ci * C, C)
        bcum = _dot_tri3(tri, la_ref[0, rows, :])
        b_last = bcum[C - 1:C, :]
        k = k_ref[0, rows, :]
        qe.append((q_ref[0, rows, :] * jnp.exp(bcum)).astype(BF16))
        ke = (k * jnp.exp(-bcum)).astype(BF16)
        kd.append((k * jnp.exp(b_last - bcum)).astype(BF16))
        dec.append(jnp.exp(b_last))
        vb.append(v_ref[0, rows, :].astype(BF16))
        a = [jnp.where(low_incl, _dot_nt(qe[ci][:, kss[h]], ke[:, kss[h]]), 0.0).astype(BF16)
             for h in heads]
        o_intra.append([_dot(a[h], vb[ci][:, vss[h]]) for h in heads])
        upd.append([_dot_tn(vb[ci][:, vss[h]], kd[ci][:, kss[h]]) for h in heads])
    st = [st_ref[h] for h in heads]
    for ci in range(chunks):
        rows = pl.ds(ci * C, C)
        r = r_ref[0, rows, :]
        for h in heads:
            o = o_intra[ci][h] + _dot_nt(qe[ci][:, kss[h]], st[h].astype(BF16))
            st[h] = st[h] * dec[ci][:, kss[h]] + upd[ci][h]
            o = o * lax.rsqrt(jnp.mean(o * o, axis=-1, keepdims=True) + GLA_NORM_EPS) * nw
            rh = r[:, vss[h]]
            o_ref[0, rows, vss[h]] = (o * (rh * _sigmoid(rh))).astype(o_ref.dtype)
    for h in heads:
        st_ref[h] = st[h]


def _gla_layer(h, g_pre, g_post, w_in, a_up, a_bias, norm_w, w_o):
    B, S, D = h.shape
    dk, dv = GLA_KEY_DIM, GLA_VAL_DIM
    tm = min(PROJ_TM, S)
    n_main = 2 * dk + 2 * dv
    w_main = w_in[:, :n_main].astype(BF16)
    w_a = _pad_cols(w_in[:, n_main:], LORA_PAD).astype(BF16)
    aup = _pad_rows(a_up, LORA_PAD).astype(BF16)
    q, k, v, r, la = pl.pallas_call(
        _gla_proj_kernel,
        grid=(B, S // tm),
        in_specs=[_tok_spec(tm, D), _const_spec((1, D)), _const_spec((D, n_main)),
                  _const_spec((D, LORA_PAD)), _const_spec((LORA_PAD, dk)), _const_spec((1, dk))],
        out_specs=[_tok_spec(tm, dk), _tok_spec(tm, dk), _tok_spec(tm, dv), _tok_spec(tm, dv),
                   _tok_spec(tm, dk)],
        out_shape=[jax.ShapeDtypeStruct((B, S, w), F32) for w in (dk, dk, dv, dv, dk)],
        compiler_params=_params(),
        name="gla_proj",
    )(h, _row(g_pre), w_main, w_a, aup, _row(a_bias))

    chunks = 4 if S % (4 * GLA_CHUNK) == 0 else 1
    tc = chunks * GLA_CHUNK
    o = pl.pallas_call(
        functools.partial(_gla_chunk_kernel, chunks=chunks),
        grid=(B, S // tc),
        in_specs=[_tok_spec(tc, dk), _tok_spec(tc, dk), _tok_spec(tc, dv), _tok_spec(tc, dv),
                  _tok_spec(tc, dk), _const_spec((1, dv // GLA_HEADS))],
        out_specs=_tok_spec(tc, dv),
        out_shape=jax.ShapeDtypeStruct((B, S, dv), BF16),
        scratch_shapes=[pltpu.VMEM((GLA_HEADS, dv // GLA_HEADS, dk // GLA_HEADS), F32)],
        compiler_params=_params(),
        name="gla_chunk",
    )(q, k, v, r, la, _row(norm_w))
    return _post(h, o, w_o, g_post)


def _nsa_head_perm():
    perm = np.zeros((D_MODEL,), np.int32)
    for p in range(NSA_HPG):
        for g in range(NSA_KV_GROUPS):
            src = (g * NSA_HPG + p) * NSA_HEAD_DIM
            dst = p * LANES + g * NSA_HEAD_DIM
            perm[dst:dst + NSA_HEAD_DIM] = np.arange(src, src + NSA_HEAD_DIM)
    return perm


def _nsa_proj_kernel(h_ref, gn_ref, wqt_ref, wkv_ref, wg_ref, qt_ref, kv_ref, gates_ref):
    xn = _rms(h_ref[0], gn_ref[...], NORM_EPS).astype(BF16)
    qt_ref[0] = _dot_nt(wqt_ref[...], xn).astype(BF16)
    kv_ref[0] = _dot(xn, wkv_ref[...]).astype(BF16)
    gates_ref[0] = _sigmoid(_dot(xn, wg_ref[...]))


def _nsa_compress_kernel(r_ref, w1_ref, pe_ref, b1_ref, w2_ref, out_ref, *, n_cmp):
    r = r_ref[0, 0].astype(F32)
    rows = r.shape[0]
    pe = pe_ref[0]
    ra = (r + pe[0]).astype(BF16)
    rb = (pltpu.roll(r, rows - 1, 0) + pe[1]).astype(BF16)
    out = jnp.zeros((rows, LANES), F32)
    for g in range(NSA_KV_GROUPS):
        hid = _gelu_tanh(_dot(ra, w1_ref[0, g, 0]) + _dot(rb, w1_ref[0, g, 1]) + b1_ref[0])
        out = out + _dot(hid.astype(BF16), w2_ref[0, g])
    rowi = lax.broadcasted_iota(jnp.int32, out.shape, 0)
    out_ref[0, 0] = jnp.where(rowi < n_cmp, out, 0.0).astype(BF16)


def _query_chains(qt, heads_per_chain):
    row = lax.broadcasted_iota(jnp.int32, (LANES, qt.shape[1]), 0)
    zero = jnp.zeros((), qt.dtype)
    out = []
    for g in range(NSA_KV_GROUPS):
        keep = (row < NSA_HEAD_DIM) if g == 0 else (row >= NSA_HEAD_DIM)
        tiles = [jnp.where(keep, qt[p * LANES:(p + 1) * LANES, :], zero) for p in range(NSA_HPG)]
        out += [jnp.concatenate(tiles[c:c + heads_per_chain], axis=1)
                for c in range(0, NSA_HPG, heads_per_chain)]
    return out


def _ones_in_other_group(v, g):
    lane = lax.broadcasted_iota(jnp.int32, v.shape, 1)
    other = (lane >= NSA_HEAD_DIM) if g == 0 else (lane < NSA_HEAD_DIM)
    return jnp.where(other, jnp.ones((), v.dtype), v)


def _nsa_cmp_kernel(qt_ref, kc_ref, vc_ref, ov_ref, ocmp_ref, score_ref, *, n_cmp, n_slc):
    TQ = qt_ref.shape[2]
    ncp = kc_ref.shape[1]
    nbp = ov_ref.shape[0]
    G = NSA_KV_GROUPS
    R = NSA_HPG * TQ
    CW = NSA_CHAIN_COLS
    n_cw = R // CW
    chains = range(G * n_cw)
    i = pl.program_id(1)
    n_q = pl.num_programs(1)
    t0 = i * TQ

    def run(nr):
        kc = kc_ref[0, :nr, :]
        vc = vc_ref[0, :nr, :]
        ov = ov_ref[:, :nr]
        pos = t0 + lax.broadcasted_iota(jnp.int32, (1, TQ), 1)
        blk_c = lax.broadcasted_iota(jnp.int32, (nr, 1), 0)
        vis = (blk_c * CMP_STRIDE + (CMP_BLOCK - 1) <= pos) & (blk_c < n_cmp)
        bias = jnp.where(vis, 0.0, NEG_INF)
        bias = jnp.concatenate([bias] * (CW // TQ), axis=1)
        any_vis = (pos >= CMP_BLOCK - 1).astype(F32)
        any_vis = jnp.concatenate([any_vis] * (CW // TQ), axis=1)
        qch = _query_chains(qt_ref[0], CW // TQ)
        s = [_dot(kc, qch[j]) + bias for j in chains]
        e = [jnp.exp2(s[j] - jnp.max(s[j], axis=0, keepdims=True)) for j in chains]
        p = [e[j] * (any_vis / jnp.sum(e[j], axis=0, keepdims=True)) for j in chains]
        o = [_dot_tn(vc, p[j].astype(BF16)) for j in chains]

        blk_s = lax.broadcasted_iota(jnp.int32, (nbp, 1), 0)
        cur = lax.shift_right_logical(pos, 6)
        causal = (blk_s * SLC_BLOCK <= pos) & (blk_s < n_slc)
        forced = (blk_s == 0) | (blk_s == cur) | (blk_s == cur - 1)
        bonus = np.float32(FORCE_BONUS) * forced.astype(F32)
        for g in range(G):
            psum = jnp.zeros((nr, TQ), F32)
            for j in range(g * n_cw, (g + 1) * n_cw):
                for c in range(CW // TQ):
                    psum = psum + p[j][:, c * TQ:(c + 1) * TQ]
            hi, lo = _split2(psum)
            imp = _dot(ov, hi) + _dot(ov, lo)
            score_ref[0, g * nbp:(g + 1) * nbp, :] = jnp.where(causal, imp + bonus, NEG_INF)

        half = NSA_HEAD_DIM
        for p_i in range(NSA_HPG):
            j, c = divmod(p_i, CW // TQ)
            cols = slice(c * TQ, (c + 1) * TQ)
            o_pair = jnp.concatenate([o[j][:half, cols], o[n_cw + j][half:, cols]], axis=0)
            ocmp_ref[0, :, p_i * LANES:(p_i + 1) * LANES] = o_pair.T

    parts = CMP_ROW_PARTS if ncp % (CMP_ROW_PARTS * LANES) == 0 else 1
    for k in range(parts):
        in_part = (i * parts >= k * n_q) & (i * parts < (k + 1) * n_q)
        pl.when(in_part)(functools.partial(run, (k + 1) * ncp // parts))


def _nsa_topk_kernel(score_ref, bias_ref, *, top_n):
    G = NSA_KV_GROUPS
    nbp = score_ref.shape[1] // G
    sc0 = score_ref[0].reshape(G, nbp, score_ref.shape[2])
    row = lax.broadcasted_iota(jnp.int32, sc0.shape, 1).astype(F32)

    def body(_, carry):
        sc, bias = carry
        m = jnp.max(sc, axis=1, keepdims=True)
        idx = jnp.min(jnp.where(sc == m, row, np.float32(nbp)), axis=1, keepdims=True)
        hit = row == idx
        return jnp.where(hit, np.float32(-3e38), sc), jnp.where(hit, 0.0, bias)

    _, bias = lax.fori_loop(0, top_n, body, (sc0, jnp.full(sc0.shape, NEG_INF, F32)))
    bias_ref[0] = bias.reshape(G * nbp, score_ref.shape[2])


def _nsa_attn_kernel(qt_ref, ks_ref, vs_ref, kw_ref, vw_ref, bias_ref, ocmp_ref, gates_ref, ge_ref,
                     o_ref, m_ref, acc_ref, *, win_rows):
    TQ = qt_ref.shape[2]
    S = ks_ref.shape[1]
    TK = min(NSA_TK, S)
    n_blk = TK // SLC_BLOCK
    nbp = bias_ref.shape[1] // NSA_KV_GROUPS
    R = NSA_HPG * TQ
    CW = NSA_CHAIN_COLS
    n_cw = R // CW
    t0 = pl.program_id(1) * TQ
    pos = t0 + lax.broadcasted_iota(jnp.int32, (1, TQ), 1)
    n_kt = (t0 + TQ + TK - 1) // TK
    key_s = lax.broadcasted_iota(jnp.int32, (TK, 1), 0)
    w_start = pl.multiple_of(jnp.maximum(t0 - WINDOW, 0), TQ)
    wkey = w_start + lax.broadcasted_iota(jnp.int32, (win_rows, 1), 0)
    wbias = jnp.where((wkey <= pos) & (wkey > pos - WINDOW), 0.0, NEG_INF)
    wbias = jnp.concatenate([wbias] * (CW // TQ), axis=1)

    G = NSA_KV_GROUPS
    chains = range(G * n_cw)
    qch = _query_chains(qt_ref[0], CW // TQ)

    def scores(k_tile, bias):
        return [_dot(k_tile, qch[j]) + bias[j // n_cw] for j in chains]

    m_ref[...] = jnp.full(m_ref.shape, NEG_INF, F32)
    acc_ref[...] = jnp.zeros_like(acc_ref)

    def key_tile(kt, diagonal, rows=TK):
        nb = rows // SLC_BLOCK
        k0 = pl.multiple_of(kt * TK, TK)
        b0 = pl.multiple_of(kt * n_blk, n_blk)
        k_tile = ks_ref[0, pl.ds(k0, rows), :]
        s = []
        for g in range(G):
            blk_bias = bias_ref[0, pl.ds(g * nbp + b0, nb), :]
            blk_bias = jnp.concatenate([blk_bias] * (CW // TQ), axis=1).reshape(nb, 1, CW)
            if diagonal:
                cb = jnp.where(k0 + key_s[:rows] <= pos, 0.0, NEG_INF)
                cb = jnp.concatenate([cb] * (CW // TQ), axis=1).reshape(nb, SLC_BLOCK, CW)
                blk_bias = blk_bias + cb
            for j in range(g * n_cw, (g + 1) * n_cw):
                sj = _dot(k_tile, qch[j]).reshape(nb, SLC_BLOCK, CW) + blk_bias
                s.append(sj.reshape(rows, CW))
        v_raw = vs_ref[0, pl.ds(k0, rows), :]
        v_tile = [_ones_in_other_group(v_raw, g) for g in range(G)]
        for g in range(G):
            grp = range(g * n_cw, (g + 1) * n_cw)
            m_old = {j: m_ref[:, j * CW:(j + 1) * CW] for j in grp}
            m_new = {j: jnp.maximum(m_old[j], jnp.max(s[j], axis=0, keepdims=True)) for j in grp}
            p = {j: jnp.exp2(s[j] - m_new[j]).astype(BF16) for j in grp}
            pv = {j: _dot_tn(v_tile[g], p[j]) for j in grp}
            for j in grp:
                cols = slice(j * CW, (j + 1) * CW)
                acc_ref[:, cols] = acc_ref[:, cols] * jnp.exp2(m_old[j] - m_new[j]) + pv[j]
                m_ref[:, cols] = m_new[j]

    def full_tile(kt, carry):
        key_tile(kt, diagonal=False)
        return carry

    lax.fori_loop(0, n_kt - 1, full_tile, 0)
    diag_variants = TK // TQ
    for d in range(diag_variants):
        at_offset = (pl.program_id(1) % diag_variants) == d
        pl.when(at_offset)(functools.partial(key_tile, n_kt - 1, True, (d + 1) * TQ))

    vw_raw = vw_ref[0, pl.ds(w_start, win_rows), :]
    vw = [_ones_in_other_group(vw_raw, g) for g in range(G)]
    s = scores(kw_ref[0, pl.ds(w_start, win_rows), :], [wbias] * G)
    p = [jnp.exp2(s[j] - jnp.max(s[j], axis=0, keepdims=True)).astype(BF16) for j in chains]
    acc_win = jnp.concatenate([_dot_tn(vw[j // n_cw], p[j]) for j in chains], axis=1)

    half = NSA_HEAD_DIM
    normed = {}
    for name, a in (("slc", acc_ref[...]), ("win", acc_win)):
        a0, a1 = a[:, :R], a[:, R:]
        normed[name] = jnp.concatenate([a0[:half] / a0[half:half + 1], a1[half:] / a1[0:1]], axis=0)
    gates = gates_ref[0]
    g_cmp = _dot_hilo(gates, ge_ref[0])
    g_slc = _dot_hilo(gates, ge_ref[1])
    g_win = _dot_hilo(gates, ge_ref[2])
    for p_i in range(NSA_HPG):
        cols = slice(p_i * LANES, (p_i + 1) * LANES)
        slc = normed["slc"][:, p_i * TQ:(p_i + 1) * TQ].T
        win = normed["win"][:, p_i * TQ:(p_i + 1) * TQ].T
        o_ref[0, :, cols] = (g_cmp[:, cols] * ocmp_ref[0, :, cols] + g_slc[:, cols] * slc
                             + g_win[:, cols] * win).astype(o_ref.dtype)


def _nsa_layer(h, g_pre, g_post, w_in, cmp_w1, cmp_b1, cmp_w2, cmp_pe, w_o):
    B, S, D = h.shape
    H, G, dh, hpg = NSA_HEADS, NSA_KV_GROUPS, NSA_HEAD_DIM, NSA_HPG
    perm = _nsa_head_perm()
    n_q = H * dh
    n_kv = 6 * G * dh
    wqt = (w_in[:, :n_q] * np.float32(dh ** -0.5 * np.log2(np.e)))[:, perm].T.astype(BF16)
    wkv = w_in[:, n_q:n_q + n_kv].astype(BF16)
    wgt = _pad_cols(w_in[:, n_q + n_kv:], LANES).astype(BF16)
    tm = min(PROJ_TM, S)
    qt_spec = lambda t: pl.BlockSpec((1, n_q, t), lambda b, i: (b, 0, i))
    qt, kv, gates = pl.pallas_call(
        _nsa_proj_kernel,
        grid=(B, S // tm),
        in_specs=[_tok_spec(tm, D), _const_spec((1, D)), _const_spec((n_q, D)),
                  _const_spec((D, n_kv)), _const_spec((D, LANES))],
        out_specs=[qt_spec(tm), _tok_spec(tm, n_kv), _tok_spec(tm, LANES)],
        out_shape=[jax.ShapeDtypeStruct((B, n_q, S), BF16), jax.ShapeDtypeStruct((B, S, n_kv), BF16),
                   jax.ShapeDtypeStruct((B, S, LANES), F32)],
        compiler_params=_params(),
        name="nsa_proj",
    )(h, _row(g_pre), wqt, wkv, wgt)

    n_cmp = (S - CMP_BLOCK) // CMP_STRIDE + 1
    ncp = S // CMP_STRIDE
    half = CMP_BLOCK // 2
    assert CMP_STRIDE == half
    rk = jnp.stack([kv[:, :, :LANES], kv[:, :, LANES:2 * LANES]]).reshape(2, B, ncp, half * LANES)
    w1 = cmp_w1.reshape(2, 2, half, dh, CMP_HIDDEN)
    w1x = jnp.zeros((2, G, 2, half, G, dh, CMP_HIDDEN), F32)
    for g in range(G):
        w1x = w1x.at[:, g, :, :, g].set(w1)
    w1x = w1x.reshape(2, G, 2, half * LANES, CMP_HIDDEN).astype(BF16)
    pe = jnp.broadcast_to(cmp_pe.reshape(2, 2, half, 1, dh), (2, 2, half, G, dh))
    pe = pe.reshape(2, 2, 1, half * LANES).astype(F32)
    w2x = jnp.zeros((2, G, CMP_HIDDEN, G, dh), F32)
    for g in range(G):
        w2x = w2x.at[:, g, :, g].set(cmp_w2)
    w2x = w2x.reshape(2, G, CMP_HIDDEN, LANES).astype(BF16)
    kvc = pl.pallas_call(
        functools.partial(_nsa_compress_kernel, n_cmp=n_cmp),
        grid=(2, B),
        in_specs=[pl.BlockSpec((1, 1, ncp, half * LANES), lambda c, b: (c, b, 0, 0)),
                  pl.BlockSpec((1, G, 2, half * LANES, CMP_HIDDEN), lambda c, b: (c, 0, 0, 0, 0)),
                  pl.BlockSpec((1, 2, 1, half * LANES), lambda c, b: (c, 0, 0, 0)),
                  pl.BlockSpec((1, 1, CMP_HIDDEN), lambda c, b: (c, 0, 0)),
                  pl.BlockSpec((1, G, CMP_HIDDEN, LANES), lambda c, b: (c, 0, 0, 0))],
        out_specs=pl.BlockSpec((1, 1, ncp, LANES), lambda c, b: (c, b, 0, 0)),
        out_shape=jax.ShapeDtypeStruct((2, B, ncp, LANES), BF16),
        compiler_params=_params(),
        name="nsa_compress",
    )(rk, w1x, pe, cmp_b1.reshape(2, 1, CMP_HIDDEN).astype(F32), w2x)

    n_slc = S // SLC_BLOCK
    top_n = min(SLC_TOPK, n_slc)
    nbp = max(LANES, n_slc)
    start = np.arange(ncp) * CMP_STRIDE
    end = start + CMP_BLOCK - 1
    s_start = np.arange(nbp) * SLC_BLOCK
    s_end = s_start + SLC_BLOCK - 1
    overlap = ((end[:, None] >= s_start[None, :]) & (start[:, None] <= s_end[None, :])
               & (np.arange(ncp)[:, None] < n_cmp) & (np.arange(nbp)[None, :] < n_slc))
    overlap_t = jnp.asarray(overlap.astype(np.float32).T, BF16)
    TQ = min(NSA_TQ, S)
    blk_tok_spec = lambda t: pl.BlockSpec((1, G * nbp, t), lambda b, i: (b, 0, i))
    o_cmp, score = pl.pallas_call(
        functools.partial(_nsa_cmp_kernel, n_cmp=n_cmp, n_slc=n_slc),
        grid=(B, S // TQ),
        in_specs=[qt_spec(TQ),
                  pl.BlockSpec((None, 1, ncp, LANES), lambda b, i: (0, b, 0, 0)),
                  pl.BlockSpec((None, 1, ncp, LANES), lambda b, i: (1, b, 0, 0)),
                  _const_spec((nbp, ncp))],
        out_specs=[_tok_spec(TQ, n_q), blk_tok_spec(TQ)],
        out_shape=[jax.ShapeDtypeStruct((B, S, n_q), F32),
                   jax.ShapeDtypeStruct((B, G * nbp, S), F32)],
        compiler_params=_params(),
        name="nsa_compressed_attn",
    )(qt, kvc, kvc, overlap_t)
    tr = min(NSA_TOPK_TOKENS, S)
    sel_bias = pl.pallas_call(
        functools.partial(_nsa_topk_kernel, top_n=top_n),
        grid=(B, S // tr),
        in_specs=[blk_tok_spec(tr)],
        out_specs=blk_tok_spec(tr),
        out_shape=jax.ShapeDtypeStruct((B, G * nbp, S), F32),
        compiler_params=_params(),
        name="nsa_topk",
    )(score)

    ge = np.zeros((3, LANES, n_q), np.float32)
    for g in range(G):
        for p in range(hpg):
            for j in range(3):
                ge[j, g * hpg * 3 + p * 3 + j, p * LANES + g * dh:p * LANES + (g + 1) * dh] = 1.0
    ge = jnp.asarray(ge, BF16)
    win_rows = min(WINDOW + TQ, S)
    kv_spec = lambda c: pl.BlockSpec((1, S, LANES), lambda b, i, c=c: (b, 0, c))
    o = pl.pallas_call(
        functools.partial(_nsa_attn_kernel, win_rows=win_rows),
        grid=(B, S // TQ),
        in_specs=[qt_spec(TQ), kv_spec(2), kv_spec(3), kv_spec(4), kv_spec(5),
                  blk_tok_spec(TQ), _tok_spec(TQ, n_q), _tok_spec(TQ, LANES),
                  _const_spec((3, LANES, n_q))],
        out_specs=_tok_spec(TQ, n_q),
        out_shape=jax.ShapeDtypeStruct((B, S, n_q), BF16),
        scratch_shapes=[pltpu.VMEM((1, G * hpg * TQ), F32), pltpu.VMEM((LANES, G * hpg * TQ), F32)],
        compiler_params=_params(),
        name="nsa_selected_window_attn",
    )(qt, kv, kv, kv, kv, sel_bias, o_cmp, gates, ge)
    return _post(h, o, w_o[perm, :], g_post)


def kernel(x, norm_g, ffn_w_in, ffn_conv_w, ffn_conv_b, ffn_w_out, rwkv_mu, rwkv_w_rkv, rwkv_w0, rwkv_w1, rwkv_w2, rwkv_a0, rwkv_a1, rwkv_a2, rwkv_g1, rwkv_g2, rwkv_k_k, rwkv_k_a, rwkv_r_k, rwkv_ln_w, rwkv_ln_b, rwkv_w_o, rwkv_v0, rwkv_v1, rwkv_v2, nsa_w_in, nsa_cmp_w1, nsa_cmp_b1, nsa_cmp_w2, nsa_cmp_pe, nsa_w_o, gla_w_in, gla_a_up, gla_a_bias, gla_norm_w, gla_w_o):
    depth = norm_g.shape[0]
    h = x
    v_first = None
    for i in range(depth):
        kind, j = i % 3, i // 3
        if kind == 0:
            vres = None if j == 0 else (rwkv_v0[j - 1], rwkv_v1[j - 1], rwkv_v2[j - 1])
            h, v_first = _rwkv_layer(
                h, norm_g[i, 0], norm_g[i, 1], rwkv_mu[j], rwkv_w_rkv[j], rwkv_w0[j], rwkv_w1[j],
                rwkv_w2[j], rwkv_a0[j], rwkv_a1[j], rwkv_a2[j], rwkv_g1[j], rwkv_g2[j], rwkv_k_k[j],
                rwkv_k_a[j], rwkv_r_k[j], rwkv_ln_w[j], rwkv_ln_b[j], rwkv_w_o[j], v_first, vres)
        elif kind == 1:
            h = _nsa_layer(h, norm_g[i, 0], norm_g[i, 1], nsa_w_in[j], nsa_cmp_w1[j], nsa_cmp_b1[j],
                           nsa_cmp_w2[j], nsa_cmp_pe[j], nsa_w_o[j])
        else:
            h = _gla_layer(h, norm_g[i, 0], norm_g[i, 1], gla_w_in[j], gla_a_up[j], gla_a_bias[j],
                           gla_norm_w[j], gla_w_o[j])
        h = _ffn(h, norm_g[i, 2], ffn_w_in[i], ffn_conv_w[i], ffn_conv_b[i], ffn_w_out[i],
                 norm_g[i, 3])
    return h
```
